```python
import math
import jax, jax.numpy as jnp
from jax import lax
import numpy as np

D_MODEL = 1024
BATCH = 2
SEQ = 8192
DEPTH = 2

MLA_HEADS = 8
MLA_Q_RANK = 256
MLA_KV_RANK = 128
MLA_NOPE = 64
MLA_ROPE = 32
MLA_V = 64
ROPE_BASE = 10000.0
ATTN_BLOCK = 128
MAX_POS_OFFSET = 4096
MASK_VALUE = -1e30
DN_HEADS = 4
DN_DK = 128
DN_DV = 128
DN_CONV = 4
DN_CHUNK = 64
HG_HEADS = 4
HG_DK = 128
HG_DV = 128
HG_CHUNK = 64
MIN_FORGET = 1e-30
N_GROUPS = 4
EXPERTS_PER_GROUP = 8
N_EXPERTS = N_GROUPS * EXPERTS_PER_GROUP
TOP_K_IN_GROUP = 2
EXPERT_FF = 256
MOE_TOKEN_BLOCK = 128
N_BRANCHES = 3
DEEPNORM_ALPHA = (2 * DEPTH) ** 0.25
DEEPNORM_BETA = (8 * DEPTH) ** -0.25
NORM_EPS = 1e-6

SPLIT_SIZES = (
    MLA_Q_RANK, MLA_KV_RANK, MLA_ROPE,
    DN_HEADS * (2 * DN_DK + DN_DV), DN_HEADS, DN_HEADS,
    DN_HEADS * DN_DV,
    HG_HEADS * HG_DK, HG_HEADS * HG_DK, HG_HEADS * HG_DV,
    HG_HEADS * HG_DV,
    N_BRANCHES * D_MODEL,
)
SPLIT_POINTS = tuple(int(v) for v in np.cumsum(SPLIT_SIZES)[:-1])
IN_COLS = int(sum(SPLIT_SIZES))

kernel_name = "hybrid_mla_gdn_hgrn2_hmoe_deepnorm"


def layer_norm(x, g, b):
    xf = x.astype(jnp.float32)
    mu = jnp.mean(xf, -1, keepdims=True)
    var = jnp.mean(jnp.square(xf - mu), -1, keepdims=True)
    return ((xf - mu) * lax.rsqrt(var + NORM_EPS) * g.astype(jnp.float32)
            + b.astype(jnp.float32)).astype(x.dtype)


def rms_norm(x, g):
    xf = x.astype(jnp.float32)
    y = xf * lax.rsqrt(jnp.mean(jnp.square(xf), -1, keepdims=True) + NORM_EPS)
    return (y * g.astype(jnp.float32)).astype(x.dtype)


def l2norm(t):
    return t * lax.rsqrt(jnp.sum(t * t, -1, keepdims=True) + NORM_EPS)


def masked_exp(mask, t):
    return jnp.where(mask, jnp.exp(jnp.where(mask, t, 0.0)), 0.0)


def rope_tables(positions):
    half = MLA_ROPE // 2
    inv_freq = ROPE_BASE ** (-jnp.arange(half, dtype=jnp.float32) / half)
    ang = positions.astype(jnp.float32)[..., None] * inv_freq
    return jnp.cos(ang), jnp.sin(ang)


def apply_rope(x, cos, sin):
    half = x.shape[-1] // 2
    xf = x.astype(jnp.float32)
    x1, x2 = xf[..., :half], xf[..., half:]
    return jnp.concatenate([x1 * cos - x2 * sin, x2 * cos + x1 * sin], -1).astype(x.dtype)


def mla_branch(c_q, c_kv, k_rope, q_norm, w_uq, kv_norm, w_ukv, cos, sin):
    B, S, _ = c_q.shape
    q = (rms_norm(c_q, q_norm) @ w_uq).reshape(B, S, MLA_HEADS, MLA_NOPE + MLA_ROPE)
    kv = (rms_norm(c_kv, kv_norm) @ w_ukv).reshape(B, S, MLA_HEADS, MLA_NOPE + MLA_V)
    q_nope, q_rope = q[..., :MLA_NOPE], q[..., MLA_NOPE:]
    k_nope, v = kv[..., :MLA_NOPE], kv[..., MLA_NOPE:]
    q_rope = apply_rope(q_rope, cos[:, :, None, :], sin[:, :, None, :])
    k_rope = apply_rope(k_rope, cos, sin)
    q = jnp.concatenate([q_nope, q_rope], -1)
    k = jnp.concatenate(
        [k_nope, jnp.broadcast_to(k_rope[:, :, None, :], (B, S, MLA_HEADS, MLA_ROPE))], -1)
    scale = (MLA_NOPE + MLA_ROPE) ** -0.5
    n_blk = S // ATTN_BLOCK
    qb = (q * scale).reshape(B, n_blk, ATTN_BLOCK, MLA_HEADS, -1).transpose(1, 0, 2, 3, 4)
    key_pos = jnp.arange(S)

    def attend(args):
        q_blk, blk = args
        s = jnp.einsum('bqhd,bkhd->bhqk', q_blk, k).astype(jnp.float32)
        q_pos = blk * ATTN_BLOCK + jnp.arange(ATTN_BLOCK)
        s = jnp.where(key_pos[None, :] <= q_pos[:, None], s, MASK_VALUE)
        p = jax.nn.softmax(s, axis=-1).astype(v.dtype)
        return jnp.einsum('bhqk,bkhd->bqhd', p, v)

    o = lax.map(attend, (qb, jnp.arange(n_blk)))
    return o.transpose(1, 0, 2, 3, 4).reshape(B, S, MLA_HEADS * MLA_V)


def causal_conv_silu(x, w):
    K, C = w.shape
    y = lax.conv_general_dilated(x, w[:, None, :], window_strides=(1,), padding=[(K - 1, 0)],
                                 dimension_numbers=('NWC', 'WIO', 'NWC'), feature_group_count=C)
    return jax.nn.silu(y)


def gated_deltanet_branch(qkv, beta_logit, a_logit, gate, conv_w, a_log, dt_bias, o_norm):
    B, S, _ = qkv.shape
    f32 = jnp.float32
    H, C = DN_HEADS, DN_CHUNK
    N = S // C
    qkv = causal_conv_silu(qkv, conv_w)
    q, k, v = jnp.split(qkv, [H * DN_DK, 2 * H * DN_DK], axis=-1)
    q = l2norm(q.reshape(B, S, H, DN_DK).astype(f32)) * DN_DK ** -0.5
    k = l2norm(k.reshape(B, S, H, DN_DK).astype(f32))
    v = v.reshape(B, S, H, DN_DV).astype(f32)
    beta = jax.nn.sigmoid(beta_logit.astype(f32))
    g = -jnp.exp(a_log.astype(f32)) * jax.nn.softplus(a_logit.astype(f32) + dt_bias.astype(f32))

    def chunks(t):
        return jnp.moveaxis(t.reshape((B, N, C, H) + t.shape[3:]), 3, 1)

    q, k, v, beta, g = chunks(q), chunks(k), chunks(v), chunks(beta), chunks(g)
    g = jnp.cumsum(g, axis=-1)
    tri_incl = jnp.tril(jnp.ones((C, C), bool))
    tri_strict = jnp.tril(jnp.ones((C, C), bool), -1)
    decay = masked_exp(tri_incl, g[..., :, None] - g[..., None, :])
    kb = k * beta[..., None]
    m = jnp.where(tri_strict, jnp.einsum('bhnid,bhnjd->bhnij', kb, k) * decay, 0.0)
    a = m + jnp.eye(C, dtype=f32)
    rhs = jnp.concatenate([v * beta[..., None], kb * jnp.exp(g)[..., None]], -1)
    sol = lax.linalg.triangular_solve(a, rhs, left_side=True, lower=True, unit_diagonal=True)
    u, w = sol[..., :DN_DV], sol[..., DN_DV:]
    qk = jnp.einsum('bhnid,bhnjd->bhnij', q, k) * decay
    q_dec = q * jnp.exp(g)[..., None]
    k_dec = k * jnp.exp(g[..., -1:] - g)[..., None]
    g_last = jnp.exp(g[..., -1])
    xs = tuple(jnp.moveaxis(t, 2, 0) for t in (q_dec, k_dec, u, w, qk, g_last))

    def step(state, inp):
        qd, kd, uc, wc, qkc, gl = inp
        v_new = uc - jnp.einsum('bhcd,bhde->bhce', wc, state)
        o = jnp.einsum('bhcd,bhde->bhce', qd, state) + jnp.einsum('bhij,bhje->bhie', qkc, v_new)
        state = state * gl[..., None, None] + jnp.einsum('bhcd,bhce->bhde', kd, v_new)
        return state, o

    s0 = jnp.zeros((B, H, DN_DK, DN_DV), f32)
    _, o = lax.scan(step, s0, xs)
    o = o.transpose(1, 0, 3, 2, 4).reshape(B, S, H, DN_DV)
    o = rms_norm(o, o_norm) * jax.nn.silu(gate.reshape(B, S, H, DN_DV).astype(f32))
    return o.reshape(B, S, H * DN_DV).astype(qkv.dtype)


def hgrn2_branch(q, f_logit, i_val, gate, lower_bound, o_norm):
    B, S, _ = q.shape
    f32 = jnp.float32
    H, C = HG_HEADS, HG_CHUNK
    N = S // C
    q = jax.nn.silu(q.astype(f32)).reshape(B, S, H, HG_DK)
    lb = lower_bound.reshape(H, HG_DK)
    z = f_logit.astype(f32).reshape(B, S, H, HG_DK)
    f = lb + (1.0 - lb) * jax.nn.sigmoid(z)
    k = (1.0 - lb) * jax.nn.sigmoid(-z)
    log_f = jnp.log(jnp.maximum(f, MIN_FORGET))
    v = i_val.astype(f32).reshape(B, S, H, HG_DV)

    def chunks(t):
        return t.reshape(B, N, C, H, t.shape[-1]).transpose(1, 0, 3, 2, 4)

    qc, kc, vc, bc = chunks(q), chunks(k), chunks(v), chunks(log_f)
    bc = jnp.cumsum(bc, axis=3)
    q_dec = qc * jnp.exp(bc)
    k_dec = kc * jnp.exp(bc[:, :, :, -1:, :] - bc)
    d_last = jnp.exp(bc[:, :, :, -1, :])
    tri_incl = jnp.tril(jnp.ones((C, C), bool))[:, :, None]

    def step(state, inp):
        qs, ks, vs, bs, qd, kd, dl = inp
        dec = masked_exp(tri_incl, bs[:, :, :, None, :] - bs[:, :, None, :, :])
        att = jnp.einsum('bhid,bhjd,bhijd->bhij', qs, ks, dec)
        o = jnp.einsum('bhcd,bhde->bhce', qd, state) + jnp.einsum('bhij,bhje->bhie', att, vs)
        state = state * dl[..., :, None] + jnp.einsum('bhcd,bhce->bhde', kd, vs)
        return state, o

    s0 = jnp.zeros((B, H, HG_DK, HG_DV), f32)
    _, o = lax.scan(step, s0, (qc, kc, vc, bc, q_dec, k_dec, d_last))
    o = o.transpose(1, 0, 3, 2, 4).reshape(B, S, H, HG_DV)
    o = rms_norm(o, o_norm) * jax.nn.silu(gate.reshape(B, S, H, HG_DV).astype(f32))
    return o.reshape(B, S, H * HG_DV).astype(i_val.dtype)


def hier_moe(x, wg, bg, we, be, w_gate, w_up, w_down):
    B, S, D = x.shape
    f32 = jnp.float32
    xt = x.reshape(-1, D)
    group_logits = (xt @ wg + bg).astype(f32)
    p_group = jnp.max(jax.nn.softmax(group_logits, -1), -1, keepdims=True)
    top_group = jnp.argmax(group_logits, -1)
    expert_logits = (xt @ we + be).astype(f32).reshape(-1, N_GROUPS, EXPERTS_PER_GROUP)
    in_group = jnp.einsum('tg,tge->te', jax.nn.one_hot(top_group, N_GROUPS, dtype=f32),
                          expert_logits)
    top_val, top_idx = lax.top_k(in_group, TOP_K_IN_GROUP)
    w_top = jax.nn.softmax(top_val, -1) * p_group
    expert_id = top_group[:, None] * EXPERTS_PER_GROUP + top_idx
    combine = jnp.einsum('tk,tke->te', w_top,
                         jax.nn.one_hot(expert_id, N_EXPERTS, dtype=f32)).astype(x.dtype)
    xb = xt.reshape(-1, MOE_TOKEN_BLOCK, D)
    cb = combine.reshape(-1, MOE_TOKEN_BLOCK, N_EXPERTS)

    def expert_block(args):
        xs, cs = args
        h = jax.nn.silu(jnp.einsum('td,edf->tef', xs, w_gate)) * jnp.einsum('td,edf->tef', xs, w_up)
        return jnp.einsum('tef,efd->td', h * cs[..., None], w_down)

    return lax.map(expert_block, (xb, cb)).reshape(B, S, D)


def setup_inputs(seed: int = 0) -> dict:
    key = jax.random.key(seed)
    keys = jax.random.split(key, 32)
    L, D = DEPTH, D_MODEL
    f32 = jnp.float32

    def nrm(i, shape, scale):
        return jax.random.normal(keys[i], shape, f32) * scale

    x = nrm(0, (BATCH, SEQ, D), 1.0)
    offs = jax.random.randint(keys[1], (BATCH, 1), 0, MAX_POS_OFFSET, dtype=jnp.int32)
    positions = offs + jnp.arange(SEQ, dtype=jnp.int32)[None, :]
    a_init = jax.random.uniform(keys[11], (L, DN_HEADS), f32, 1.0, 16.0)
    dt = jnp.exp(jax.random.uniform(keys[12], (L, DN_HEADS), f32, math.log(1e-3), math.log(1e-1)))
    mla_v_width = MLA_HEADS * MLA_V
    dn_width = DN_HEADS * DN_DV
    hg_width = HG_HEADS * HG_DV
    return {
        "x": x,
        "positions": positions,
        "ln_in_g": 1.0 + nrm(2, (D,), 0.02),
        "ln_in_b": nrm(3, (D,), 0.02),
        "hg_lower_bounds": 1.0 + nrm(4, (L, HG_HEADS * HG_DK), 0.1),
        "w_in": nrm(5, (L, D, IN_COLS), D ** -0.5),
        "mla_q_norm": 1.0 + nrm(6, (L, MLA_Q_RANK), 0.02),
        "mla_w_uq": nrm(7, (L, MLA_Q_RANK, MLA_HEADS * (MLA_NOPE + MLA_ROPE)), MLA_Q_RANK ** -0.5),
        "mla_kv_norm": 1.0 + nrm(8, (L, MLA_KV_RANK), 0.02),
        "mla_w_ukv": nrm(9, (L, MLA_KV_RANK, MLA_HEADS * (MLA_NOPE + MLA_V)), MLA_KV_RANK ** -0.5),
        "dn_conv": nrm(10, (L, DN_CONV, DN_HEADS * (2 * DN_DK + DN_DV)), DN_CONV ** -0.5),
        "dn_a_log": jnp.log(a_init),
        "dn_dt_bias": dt + jnp.log(-jnp.expm1(-dt)),
        "dn_o_norm": 1.0 + nrm(13, (L, DN_DV), 0.02),
        "hg_o_norm": 1.0 + nrm(14, (L, HG_DV), 0.02),
        "w_br_a": nrm(15, (L, mla_v_width, D), mla_v_width ** -0.5),
        "w_br_b": nrm(16, (L, dn_width, D), dn_width ** -0.5),
        "w_br_c": nrm(17, (L, hg_width, D), hg_width ** -0.5),
        "w_out": nrm(18, (L, D, D), D ** -0.5 * DEEPNORM_BETA),
        "ln1_g": 1.0 + nrm(19, (L, D), 0.02),
        "ln1_b": nrm(20, (L, D), 0.02),
        "router_group_w": nrm(21, (L, D, N_GROUPS), D ** -0.5),
        "router_group_b": nrm(22, (L, N_GROUPS), 0.01),
        "router_expert_w": nrm(23, (L, D, N_EXPERTS), D ** -0.5),
        "router_expert_b": nrm(24, (L, N_EXPERTS), 0.01),
        "exp_w_gate": nrm(25, (L, N_EXPERTS, D, EXPERT_FF), D ** -0.5),
        "exp_w_up": nrm(26, (L, N_EXPERTS, D, EXPERT_FF), D ** -0.5),
        "exp_w_down": nrm(27, (L, N_EXPERTS, EXPERT_FF, D), EXPERT_FF ** -0.5 * DEEPNORM_BETA),
        "ln2_g": 1.0 + nrm(28, (L, D), 0.02),
        "ln2_b": nrm(29, (L, D), 0.02),
    }


def reference(x, positions, ln_in_g, ln_in_b, hg_lower_bounds, w_in, mla_q_norm, mla_w_uq,
              mla_kv_norm, mla_w_ukv, dn_conv, dn_a_log, dn_dt_bias, dn_o_norm, hg_o_norm,
              w_br_a, w_br_b, w_br_c, w_out, ln1_g, ln1_b, router_group_w, router_group_b,
              router_expert_w, router_expert_b, exp_w_gate, exp_w_up, exp_w_down, ln2_g, ln2_b):
    cos, sin = rope_tables(positions)
    lb_soft = jax.nn.softmax(hg_lower_bounds.astype(jnp.float32), axis=0)
    lb_all = jnp.cumsum(lb_soft, axis=0) - lb_soft[0]
    h = layer_norm(x, ln_in_g, ln_in_b)
    for l in range(DEPTH):
        proj = h @ w_in[l]
        (c_q, c_kv, k_rope, dn_qkv, dn_beta, dn_a, dn_gate,
         hg_q, hg_f, hg_i, hg_gate, gate_logits) = jnp.split(proj, SPLIT_POINTS, axis=-1)
        y_a = mla_branch(c_q, c_kv, k_rope, mla_q_norm[l], mla_w_uq[l], mla_kv_norm[l],
                         mla_w_ukv[l], cos, sin) @ w_br_a[l]
        y_b = gated_deltanet_branch(dn_qkv, dn_beta, dn_a, dn_gate, dn_conv[l], dn_a_log[l],
                                    dn_dt_bias[l], dn_o_norm[l]) @ w_br_b[l]
        y_c = hgrn2_branch(hg_q, hg_f, hg_i, hg_gate, lb_all[l], hg_o_norm[l]) @ w_br_c[l]
        g_a, g_b, g_c = jnp.split(jax.nn.sigmoid(gate_logits), N_BRANCHES, axis=-1)
        mixed = (g_a * y_a + g_b * y_b + g_c * y_c) @ w_out[l]
        h = layer_norm(DEEPNORM_ALPHA * h + mixed, ln1_g[l], ln1_b[l])
        moe_out = hier_moe(h, router_group_w[l], router_group_b[l], router_expert_w[l],
                           router_expert_b[l], exp_w_gate[l], exp_w_up[l], exp_w_down[l])
        h = layer_norm(DEEPNORM_ALPHA * h + moe_out, ln2_g[l], ln2_b[l])
    return h
```

```python
import functools
import math

import jax
import jax.numpy as jnp
import numpy as np
from jax import lax
from jax.experimental import pallas as pl
from jax.experimental.pallas import tpu as pltpu

D_MODEL = 1024
DEPTH = 2
MLA_HEADS = 8
MLA_Q_RANK = 256
MLA_KV_RANK = 128
MLA_NOPE = 64
MLA_ROPE = 32
MLA_V = 64
ROPE_BASE = 10000.0
MASK_VALUE = -1e30
DN_HEADS = 4
DN_DK = 128
DN_DV = 128
DN_CONV = 4
HG_HEADS = 4
HG_DK = 128
HG_DV = 128
CHUNK = 64
MIN_FORGET = 1e-30
N_GROUPS = 4
EXPERTS_PER_GROUP = 8
N_EXPERTS = N_GROUPS * EXPERTS_PER_GROUP
EXPERT_FF = 256
DEEPNORM_ALPHA = (2 * DEPTH) ** 0.25
NORM_EPS = 1e-6

LANES = 128
HEAD_PAD = 128
ROPE_LO = MLA_NOPE
ROPE_MID = MLA_NOPE + MLA_ROPE // 2
ROPE_HI = MLA_NOPE + MLA_ROPE
SCAN_ROWS = 256
VMEM_LIMIT = 56 * 1024 * 1024

BF16 = jnp.bfloat16
F32 = jnp.float32


def _dot(a, b):
    return jnp.dot(a.astype(BF16), b.astype(BF16), preferred_element_type=F32)


def _dot_nt(a, b):
    return lax.dot_general(a.astype(BF16), b.astype(BF16), (((1,), (1,)), ((), ())),
                           preferred_element_type=F32)


def _dot_tn(a, b):
    return lax.dot_general(a.astype(BF16), b.astype(BF16), (((0,), (0,)), ((), ())),
                           preferred_element_type=F32)


def _sigmoid(x):
    return 1.0 / (1.0 + jnp.exp(-x))


def _silu(x):
    return x * _sigmoid(x)


def _layer_norm(x, g, b):
    mu = jnp.mean(x, axis=-1, keepdims=True)
    xc = x - mu
    var = jnp.mean(xc * xc, axis=-1, keepdims=True)
    return xc * lax.rsqrt(var + NORM_EPS) * g + b


def _rms_norm(x, g):
    return x * lax.rsqrt(jnp.mean(x * x, axis=-1, keepdims=True) + NORM_EPS) * g


def _params(sem):
    return pltpu.CompilerParams(dimension_semantics=sem, vmem_limit_bytes=VMEM_LIMIT)


def _full(shape):
    n = len(shape)
    return pl.BlockSpec(shape, lambda *_: (0,) * n)


def _ln_kernel(x_ref, g_ref, b_ref, o_ref):
    o_ref[...] = _layer_norm(x_ref[...], g_ref[...], b_ref[...])


def _ln_call(x, g, b, tm=1024):
    t, d = x.shape
    return pl.pallas_call(
        _ln_kernel,
        out_shape=jax.ShapeDtypeStruct((t, d), F32),
        grid=(t // tm,),
        in_specs=[pl.BlockSpec((tm, d), lambda i: (i, 0)), _full((1, d)), _full((1, d))],
        out_specs=pl.BlockSpec((tm, d), lambda i: (i, 0)),
        compiler_params=_params(("parallel",)),
        name="ln_in",
    )(x, g.reshape(1, d), b.reshape(1, d))


def _rope_kernel(pos_ref, inv_ref, c_ref, sa_ref, sb_ref):
    ang = pos_ref[...].astype(F32) * inv_ref[...]
    lane = lax.broadcasted_iota(jnp.int32, ang.shape, 1)
    sin = jnp.sin(ang)
    c_ref[...] = jnp.cos(ang)
    sa_ref[...] = jnp.where((lane >= ROPE_MID) & (lane < ROPE_HI), sin, 0.0)
    sb_ref[...] = jnp.where((lane >= ROPE_LO) & (lane < ROPE_MID), -sin, 0.0)


def _rope_call(pos_col, inv128, tm=1024):
    t = pos_col.shape[0]
    spec = pl.BlockSpec((tm, LANES), lambda i: (i, 0))
    shp = jax.ShapeDtypeStruct((t, LANES), F32)
    return pl.pallas_call(
        _rope_kernel,
        out_shape=(shp, shp, shp),
        grid=(t // tm,),
        in_specs=[pl.BlockSpec((tm, 1), lambda i: (i, 0)), _full((1, LANES))],
        out_specs=(spec, spec, spec),
        compiler_params=_params(("parallel",)),
        name="rope_tables",
    )(pos_col, inv128)


def _rope_apply(x, c, sa, sb):
    return x * c + pltpu.roll(x, 16, 1) * sa + pltpu.roll(x, LANES - 16, 1) * sb


def _mla_proj_kernel(h_ref, wcq_ref, wckv_ref, wkr_ref, qn_ref, kvn_ref, wuq_ref, wuk_ref,
                     wuv_ref, c_ref, sa_ref, sb_ref, q_ref, k_ref, v_ref):
    hb = h_ref[...].astype(BF16)
    cq = _rms_norm(_dot(hb, wcq_ref[...]), qn_ref[...])
    ckv = _rms_norm(_dot(hb, wckv_ref[...]), kvn_ref[...])
    kr = _dot(hb, wkr_ref[...])
    q = _dot(cq, wuq_ref[...])
    k = _dot(ckv, wuk_ref[...])
    v_ref[...] = _dot(ckv, wuv_ref[...]).astype(BF16)
    c, sa, sb = c_ref[...], sa_ref[...], sb_ref[...]
    scale = (MLA_NOPE + MLA_ROPE) ** -0.5
    kr = _rope_apply(kr, c, sa, sb)
    for hd in range(MLA_HEADS):
        sl = slice(hd * HEAD_PAD, (hd + 1) * HEAD_PAD)
        q_ref[:, sl] = (_rope_apply(q[:, sl], c, sa, sb) * scale).astype(BF16)
        k_ref[:, sl] = (k[:, sl] + kr).astype(BF16)


def _mla_proj_call(h, w, tables, tm=512):
    t, d = h.shape
    hq = MLA_HEADS * HEAD_PAD
    hv = MLA_HEADS * MLA_V
    row = lambda n: pl.BlockSpec((tm, n), lambda i: (i, 0))
    return pl.pallas_call(
        _mla_proj_kernel,
        out_shape=(jax.ShapeDtypeStruct((t, hq), BF16), jax.ShapeDtypeStruct((t, hq), BF16),
                   jax.ShapeDtypeStruct((t, hv), BF16)),
        grid=(t // tm,),
        in_specs=[row(d), _full((d, MLA_Q_RANK)), _full((d, MLA_KV_RANK)), _full((d, LANES)),
                  _full((1, MLA_Q_RANK)), _full((1, MLA_KV_RANK)), _full((MLA_Q_RANK, hq)),
                  _full((MLA_KV_RANK, hq)), _full((MLA_KV_RANK, hv)),
                  row(LANES), row(LANES), row(LANES)],
        out_specs=(row(hq), row(hq), row(hv)),
        compiler_params=_params(("parallel",)),
        name="mla_proj",
    )(h, w["wcq"], w["wckv"], w["wkr"], w["qn"], w["kvn"], w["wuq"], w["wuk"], w["wuv"], *tables)


def _flash_kernel(q_ref, k_ref, v_ref, o_ref, m_ref, l_ref, acc_ref, *, tq):
    qi = pl.program_id(2)
    m_ref[...] = jnp.full(m_ref.shape, MASK_VALUE, F32)
    l_ref[...] = jnp.zeros(l_ref.shape, F32)
    acc_ref[...] = jnp.zeros(acc_ref.shape, F32)

    def block(j, masked):
        rows = pl.ds(pl.multiple_of(j * tq, tq), tq)
        vblk = v_ref[0, rows, :]
        for hh in range(2):
            sl = slice(hh * HEAD_PAD, (hh + 1) * HEAD_PAD)
            s = _dot_nt(q_ref[0, :, sl], k_ref[0, rows, sl])
            if masked:
                r = lax.broadcasted_iota(jnp.int32, s.shape, 0)
                c = lax.broadcasted_iota(jnp.int32, s.shape, 1)
                s = jnp.where(c <= r, s, MASK_VALUE)
            m_old = m_ref[hh]
            m_new = jnp.maximum(m_old, jnp.max(s, axis=-1, keepdims=True))
            alpha = jnp.exp(m_old - m_new)
            p = jnp.exp(s - m_new)
            l_ref[hh] = alpha * l_ref[hh] + jnp.sum(p, axis=-1, keepdims=True)
            acc_ref[hh] = alpha * acc_ref[hh] + _dot(p, vblk)
            m_ref[hh] = m_new

    def body(j, carry):
        block(j, False)
        return carry

    lax.fori_loop(0, qi, body, 0)
    block(qi, True)
    lane = lax.broadcasted_iota(jnp.int32, (tq, LANES), 1)
    o0 = acc_ref[0] / l_ref[0]
    o1 = acc_ref[1] / l_ref[1]
    o_ref[0] = jnp.where(lane < MLA_V, o0, o1).astype(BF16)


def _flash_call(q, k, v, batch, tq=512):
    t = q.shape[0]
    s = t // batch
    pairs = MLA_HEADS // 2
    q3 = q.reshape(batch, s, MLA_HEADS * HEAD_PAD)
    k3 = k.reshape(batch, s, MLA_HEADS * HEAD_PAD)
    v3 = v.reshape(batch, s, MLA_HEADS * MLA_V)
    out = pl.pallas_call(
        functools.partial(_flash_kernel, tq=tq),
        out_shape=jax.ShapeDtypeStruct((batch, s, MLA_HEADS * MLA_V), BF16),
        grid=(batch, pairs, s // tq),
        in_specs=[pl.BlockSpec((1, tq, 2 * HEAD_PAD), lambda b, p, i: (b, i, p)),
                  pl.BlockSpec((1, s, 2 * HEAD_PAD), lambda b, p, i: (b, 0, p)),
                  pl.BlockSpec((1, s, 2 * MLA_V), lambda b, p, i: (b, 0, p))],
        out_specs=pl.BlockSpec((1, tq, 2 * MLA_V), lambda b, p, i: (b, i, p)),
        scratch_shapes=[pltpu.VMEM((2, tq, 1), F32), pltpu.VMEM((2, tq, 1), F32),
                        pltpu.VMEM((2, tq, LANES), F32)],
        compiler_params=_params(("parallel", "parallel", "arbitrary")),
        name="flash_attn",
    )(q3, k3, v3)
    return out.reshape(t, MLA_HEADS * MLA_V)


def _proj_kernel(h_ref, w_ref, o_ref):
    o_ref[...] = _dot(h_ref[...], w_ref[...])


def _proj_call(h, w, name, tm=512):
    t, d = h.shape
    n = w.shape[1]
    return pl.pallas_call(
        _proj_kernel,
        out_shape=jax.ShapeDtypeStruct((t, n), F32),
        grid=(t // tm,),
        in_specs=[pl.BlockSpec((tm, d), lambda i: (i, 0)), _full((d, n))],
        out_specs=pl.BlockSpec((tm, n), lambda i: (i, 0)),
        compiler_params=_params(("parallel",)),
        name=name,
    )(h, w)


def _chunk_cumsum(x):
    rows = lax.broadcasted_iota(jnp.int32, x.shape, 0) % CHUNK
    shift = 1
    while shift < CHUNK:
        x = x + jnp.where(rows >= shift, pltpu.roll(x, shift, 0), 0.0)
        shift *= 2
    return x


def _chunk_last(x):
    r, n = x.shape
    x3 = x.reshape(r // CHUNK, CHUNK, n)
    return jnp.broadcast_to(x3[:, CHUNK - 1:CHUNK, :], x3.shape).reshape(r, n)


def _chunk_masks(r):
    i = lax.broadcasted_iota(jnp.int32, (r, r), 0)
    j = lax.broadcasted_iota(jnp.int32, (r, r), 1)
    same = (i // CHUNK) == (j // CHUNK)
    return same & (j <= i), same & (j < i), i == j


def _dn_kernel(x_ref, conv_ref, alog_ref, dtb_ref, onorm_ref, o_ref, halo_ref, state_ref,
               *, batch):
    r = SCAN_ROWS
    nqkv = DN_HEADS * (2 * DN_DK + DN_DV)
    gate_off = nqkv
    ba_off = nqkv + DN_HEADS * DN_DV

    @pl.when(pl.program_id(0) == 0)
    def _():
        halo_ref[...] = jnp.zeros(halo_ref.shape, F32)
        state_ref[...] = jnp.zeros(state_ref.shape, F32)

    incl, strict, diag = _chunk_masks(r)
    eye = jnp.where(diag, 1.0, 0.0)
    cw = conv_ref[...]
    pre = []
    for b in range(batch):
        xq = x_ref[b, :, :nqkv]
        xe = jnp.concatenate([halo_ref[b], xq], axis=0)
        halo_ref[b] = xq[r - 8:, :]
        y = xe[8:, :] * cw[DN_CONV - 1:DN_CONV, :]
        for i in range(DN_CONV - 1):
            y = y + pltpu.roll(xe, DN_CONV - 1 - i, 0)[8:, :] * cw[i:i + 1, :]
        y = _silu(y)
        ba = x_ref[b, :, ba_off:ba_off + LANES]
        beta_all = _sigmoid(ba)
        av = ba + dtb_ref[...]
        softplus = jnp.maximum(av, 0.0) + jnp.log(1.0 + jnp.exp(-jnp.abs(av)))
        g_all = _chunk_cumsum(-jnp.exp(alog_ref[...]) * softplus)
        g_all_t = g_all.T
        g_last_all = _chunk_last(g_all)
        for hd in range(DN_HEADS):
            q = y[:, hd * DN_DK:(hd + 1) * DN_DK]
            k = y[:, DN_HEADS * DN_DK + hd * DN_DK:DN_HEADS * DN_DK + (hd + 1) * DN_DK]
            v = y[:, 2 * DN_HEADS * DN_DK + hd * DN_DV:2 * DN_HEADS * DN_DK + (hd + 1) * DN_DV]
            q = q * lax.rsqrt(jnp.sum(q * q, axis=-1, keepdims=True) + NORM_EPS) * DN_DK ** -0.5
            k = k * lax.rsqrt(jnp.sum(k * k, axis=-1, keepdims=True) + NORM_EPS)
            beta = beta_all[:, hd:hd + 1]
            g_col = g_all[:, DN_HEADS + hd:DN_HEADS + hd + 1]
            g_row = g_all_t[DN_HEADS + hd:DN_HEADS + hd + 1, :]
            g_last = g_last_all[:, DN_HEADS + hd:DN_HEADS + hd + 1]
            decay = jnp.where(incl, jnp.exp(jnp.where(incl, g_col - g_row, 0.0)), 0.0)
            kb = k * beta
            kk = _dot_nt(kb, k)
            m = jnp.where(strict, kk * decay, 0.0)
            qk = jnp.where(incl, _dot_nt(q, k) * decay, 0.0)
            inv = eye - m
            pw = m
            for _ in range(5):
                pw = _dot(pw, pw)
                inv = inv + _dot(inv, pw)
            eg = jnp.exp(g_col)
            rhs = jnp.concatenate([v * beta, kb * eg], axis=1)
            uw = _dot(inv, rhs)
            pre.append(dict(u=uw[:, :DN_DV], w=uw[:, DN_DV:], qd=q * eg,
                            kd=k * jnp.exp(g_last - g_col), qk=qk, gl=jnp.exp(g_last)))
    outs = [[] for _ in pre]
    for c in range(r // CHUNK):
        rc = slice(c * CHUNK, (c + 1) * CHUNK)
        for idx, p in enumerate(pre):
            st = state_ref[idx]
            v_new = p["u"][rc] - _dot(p["w"][rc], st)
            outs[idx].append(_dot(p["qd"][rc], st) + _dot(p["qk"][rc, rc], v_new))
            state_ref[idx] = st * p["gl"][c * CHUNK:c * CHUNK + 1, :] + _dot_tn(p["kd"][rc], v_new)
    for idx in range(len(pre)):
        b, hd = divmod(idx, DN_HEADS)
        o = jnp.concatenate(outs[idx], axis=0)
        gate = x_ref[b, :, gate_off + hd * DN_DV:gate_off + (hd + 1) * DN_DV]
        o_ref[b, :, hd * DN_DV:(hd + 1) * DN_DV] = (
            _rms_norm(o, onorm_ref[...]) * _silu(gate)).astype(BF16)


def _dn_call(x3, conv_w, alog128, dtb128, onorm):
    batch, s, n = x3.shape
    r = SCAN_ROWS
    nqkv = DN_HEADS * (2 * DN_DK + DN_DV)
    width = DN_HEADS * DN_DV
    return pl.pallas_call(
        functools.partial(_dn_kernel, batch=batch),
        out_shape=jax.ShapeDtypeStruct((batch, s, width), BF16),
        grid=(s // r,),
        in_specs=[pl.BlockSpec((batch, r, n), lambda i: (0, i, 0)), _full((DN_CONV, nqkv)),
                  _full((1, LANES)), _full((1, LANES)), _full((1, DN_DV))],
        out_specs=pl.BlockSpec((batch, r, width), lambda i: (0, i, 0)),
        scratch_shapes=[pltpu.VMEM((batch, 8, nqkv), F32),
                        pltpu.VMEM((batch * DN_HEADS, DN_DK, DN_DV), F32)],
        compiler_params=_params(("arbitrary",)),
        name="deltanet",
    )(x3, conv_w, alog128, dtb128, onorm)


HG_LEVELS = (32, 16, 8, 4, 2, 1)


def _hg_kernel(x_ref, lbp_ref, onorm_ref, sel_ref, o_ref, state_ref, *, batch, layer):
    r = SCAN_ROWS
    width = HG_HEADS * HG_DK

    @pl.when(pl.program_id(0) == 0)
    def _():
        state_ref[...] = jnp.zeros(state_ref.shape, F32)

    lbp = lbp_ref[...]
    e = jnp.exp(lbp - jnp.max(lbp, axis=0, keepdims=True))
    soft = e / jnp.sum(e, axis=0, keepdims=True)
    lb_all = jnp.zeros((1, width), F32)
    for i in range(1, layer + 1):
        lb_all = lb_all + soft[i:i + 1, :]

    i_idx = lax.broadcasted_iota(jnp.int32, (r, r), 0)
    j_idx = lax.broadcasted_iota(jnp.int32, (r, r), 1)
    level_masks = [((i_idx // (2 * s)) == (j_idx // (2 * s))) & ((i_idx % (2 * s)) >= s)
                   & ((j_idx % (2 * s)) < s) for s in HG_LEVELS]
    diag = i_idx == j_idx

    for b in range(batch):
        for hd in range(HG_HEADS):
            idx = b * HG_HEADS + hd
            hs = slice(hd * HG_DK, (hd + 1) * HG_DK)
            lb = lb_all[:, hs]
            q = _silu(x_ref[b, :, hd * HG_DK:(hd + 1) * HG_DK])
            z = x_ref[b, :, width + hd * HG_DK:width + (hd + 1) * HG_DK]
            v = x_ref[b, :, 2 * width + hd * HG_DV:2 * width + (hd + 1) * HG_DV]
            gate = x_ref[b, :, 3 * width + hd * HG_DV:3 * width + (hd + 1) * HG_DV]
            f = lb + (1.0 - lb) * _sigmoid(z)
            k = (1.0 - lb) * _sigmoid(-z)
            cum = _chunk_cumsum(jnp.log(jnp.maximum(f, MIN_FORGET)))
            cum_last = _chunk_last(cum)
            qd = q * jnp.exp(cum)
            kd = k * jnp.exp(cum_last - cum)
            dl = jnp.exp(cum_last)
            att = jnp.where(diag, _dot_nt(q, k), 0.0)
            cum_b = cum.astype(BF16)
            for lv, mask in enumerate(level_masks):
                ref = jnp.dot(sel_ref[lv], cum_b, preferred_element_type=F32)
                qs = q * jnp.exp(jnp.minimum(cum - ref, 60.0))
                ks = k * jnp.exp(jnp.minimum(ref - cum, 60.0))
                att = att + jnp.where(mask, _dot_nt(qs, ks), 0.0)
            o_intra = _dot(att, v)
            outs = []
            st = state_ref[idx]
            for c in range(r // CHUNK):
                rc = slice(c * CHUNK, (c + 1) * CHUNK)
                outs.append(_dot_nt(qd[rc], st) + o_intra[rc])
                st = st * dl[c * CHUNK:c * CHUNK + 1, :] + _dot_tn(v[rc], kd[rc])
            state_ref[idx] = st
            o = jnp.concatenate(outs, axis=0)
            o_ref[b, :, hs] = (_rms_norm(o, onorm_ref[...]) * _silu(gate)).astype(BF16)


def _hg_sel():
    r = SCAN_ROWS
    i = np.arange(r)
    sel = np.zeros((len(HG_LEVELS), r, r), np.float32)
    for lv, s in enumerate(HG_LEVELS):
        sel[lv, i, (i // (2 * s)) * (2 * s) + s] = 1.0
    return jnp.asarray(sel, BF16)


def _hg_call(x3, lbp, onorm, layer):
    batch, s, n = x3.shape
    r = SCAN_ROWS
    width = HG_HEADS * HG_DV
    return pl.pallas_call(
        functools.partial(_hg_kernel, batch=batch, layer=layer),
        out_shape=jax.ShapeDtypeStruct((batch, s, width), BF16),
        grid=(s // r,),
        in_specs=[pl.BlockSpec((batch, r, n), lambda i: (0, i, 0)), _full(lbp.shape),
                  _full((1, HG_DV)), _full((len(HG_LEVELS), r, r))],
        out_specs=pl.BlockSpec((batch, r, width), lambda i: (0, i, 0)),
        scratch_shapes=[pltpu.VMEM((batch * HG_HEADS, HG_DV, HG_DK), F32)],
        compiler_params=_params(("arbitrary",)),
        name="hgrn2",
    )(x3, lbp, onorm, _hg_sel())


def _merge_kernel(h_ref, oa_ref, ob_ref, oc_ref, wgate_ref, wa_ref, wb_ref, wc_ref, wout_ref,
                  g_ref, b_ref, rhi_ref, rlo_ref, rb_ref, h1_ref, cmb_ref):
    h = h_ref[...]
    gates = _sigmoid(_dot(h, wgate_ref[...]))
    d = D_MODEL
    mixed = (gates[:, :d] * _dot(oa_ref[...], wa_ref[...])
             + gates[:, d:2 * d] * _dot(ob_ref[...], wb_ref[...])
             + gates[:, 2 * d:] * _dot(oc_ref[...], wc_ref[...]))
    h1 = _layer_norm(DEEPNORM_ALPHA * h + _dot(mixed, wout_ref[...]), g_ref[...], b_ref[...])
    h1_ref[...] = h1
    hi = h1.astype(BF16)
    lo = (h1 - hi.astype(F32)).astype(BF16)
    logits = (jnp.dot(hi, rhi_ref[...], preferred_element_type=F32)
              + jnp.dot(hi, rlo_ref[...], preferred_element_type=F32)
              + jnp.dot(lo, rhi_ref[...], preferred_element_type=F32)) + rb_ref[...]
    lane = lax.broadcasted_iota(jnp.int32, logits.shape, 1)
    neg = jnp.float32(-jnp.inf)
    big = jnp.int32(1 << 20)
    is_g = lane < N_GROUPS
    gl = jnp.where(is_g, logits, neg)
    gmax = jnp.max(gl, axis=-1, keepdims=True)
    gidx = jnp.min(jnp.where(gl == gmax, lane, big), axis=-1, keepdims=True)
    p_group = 1.0 / jnp.sum(jnp.where(is_g, jnp.exp(gl - gmax), 0.0), axis=-1, keepdims=True)
    lo_lane = N_GROUPS + gidx * EXPERTS_PER_GROUP
    in_g = (lane >= lo_lane) & (lane < lo_lane + EXPERTS_PER_GROUP)
    el = jnp.where(in_g, logits, neg)
    v1 = jnp.max(el, axis=-1, keepdims=True)
    i1 = jnp.min(jnp.where(el == v1, lane, big), axis=-1, keepdims=True)
    el2 = jnp.where(lane == i1, neg, el)
    v2 = jnp.max(el2, axis=-1, keepdims=True)
    i2 = jnp.min(jnp.where(el2 == v2, lane, big), axis=-1, keepdims=True)
    e21 = jnp.exp(v2 - v1)
    w1 = p_group / (1.0 + e21)
    w2 = p_group * e21 / (1.0 + e21)
    cmb_ref[...] = jnp.where(lane == i1, w1, 0.0) + jnp.where(lane == i2, w2, 0.0)


def _merge_call(h, oa, ob, oc, w, tm=256):
    t, d = h.shape
    row = lambda n: pl.BlockSpec((tm, n), lambda i: (i, 0))
    wa, wb, wc = w["wbr_a"], w["wbr_b"], w["wbr_c"]
    return pl.pallas_call(
        _merge_kernel,
        out_shape=(jax.ShapeDtypeStruct((t, d), F32), jax.ShapeDtypeStruct((t, LANES), F32)),
        grid=(t // tm,),
        in_specs=[row(d), row(oa.shape[1]), row(ob.shape[1]), row(oc.shape[1]),
                  _full((d, 3 * d)), _full(wa.shape), _full(wb.shape), _full(wc.shape),
                  _full((d, d)), _full((1, d)), _full((1, d)),
                  _full((d, LANES)), _full((d, LANES)), _full((1, LANES))],
        out_specs=(row(d), row(LANES)),
        compiler_params=_params(("parallel",)),
        name="merge_ln_router",
    )(h, oa, ob, oc, w["wgate"], wa, wb, wc, w["wout"], w["ln1_g"], w["ln1_b"],
      w["r_hi"], w["r_lo"], w["r_b"])


def _moe_kernel(h_ref, cmb_ref, wg_ref, wu_ref, wd_ref, g_ref, b_ref, o_ref, acc_ref):
    e = pl.program_id(1)

    @pl.when(e == 0)
    def _():
        acc_ref[...] = jnp.zeros(acc_ref.shape, F32)

    hb = h_ref[...].astype(BF16)
    lane = lax.broadcasted_iota(jnp.int32, cmb_ref.shape, 1)
    w_col = jnp.sum(jnp.where(lane == N_GROUPS + e, cmb_ref[...], 0.0), axis=-1, keepdims=True)
    act = _silu(_dot(hb, wg_ref[0])) * _dot(hb, wu_ref[0]) * w_col
    acc_ref[...] += _dot(act, wd_ref[0])

    @pl.when(e == N_EXPERTS - 1)
    def _():
        o_ref[...] = _layer_norm(DEEPNORM_ALPHA * h_ref[...] + acc_ref[...], g_ref[...], b_ref[...])


def _moe_call(h1, cmb, w, tm=1024):
    t, d = h1.shape
    return pl.pallas_call(
        _moe_kernel,
        out_shape=jax.ShapeDtypeStruct((t, d), F32),
        grid=(t // tm, N_EXPERTS),
        in_specs=[pl.BlockSpec((tm, d), lambda i, e: (i, 0)),
                  pl.BlockSpec((tm, LANES), lambda i, e: (i, 0)),
                  pl.BlockSpec((1, d, EXPERT_FF), lambda i, e: (e, 0, 0)),
                  pl.BlockSpec((1, d, EXPERT_FF), lambda i, e: (e, 0, 0)),
                  pl.BlockSpec((1, EXPERT_FF, d), lambda i, e: (e, 0, 0)),
                  pl.BlockSpec((1, d), lambda i, e: (0, 0)),
                  pl.BlockSpec((1, d), lambda i, e: (0, 0))],
        out_specs=pl.BlockSpec((tm, d), lambda i, e: (i, 0)),
        scratch_shapes=[pltpu.VMEM((tm, d), F32)],
        compiler_params=_params(("parallel", "arbitrary")),
        name="moe_experts",
    )(h1, cmb, w["e_gate"], w["e_up"], w["e_down"], w["ln2_g"], w["ln2_b"])


def _pad_cols(w, lo, total):
    return jnp.zeros((w.shape[0], total), w.dtype).at[:, lo:lo + w.shape[1]].set(w)


def _layer_weights(l, p):
    d = D_MODEL
    w_in = p["w_in"][l]
    sizes = (MLA_Q_RANK, MLA_KV_RANK, MLA_ROPE, DN_HEADS * (2 * DN_DK + DN_DV), DN_HEADS, DN_HEADS,
             DN_HEADS * DN_DV, HG_HEADS * HG_DK, HG_HEADS * HG_DK, HG_HEADS * HG_DV,
             HG_HEADS * HG_DV, 3 * d)
    offs = np.concatenate([[0], np.cumsum(sizes)])
    col = lambda i: w_in[:, offs[i]:offs[i + 1]]
    qk_w = MLA_NOPE + MLA_ROPE
    uq = p["mla_w_uq"][l].reshape(MLA_Q_RANK, MLA_HEADS, qk_w)
    uq = jnp.pad(uq, ((0, 0), (0, 0), (0, HEAD_PAD - qk_w))).reshape(MLA_Q_RANK, MLA_HEADS * HEAD_PAD)
    ukv = p["mla_w_ukv"][l].reshape(MLA_KV_RANK, MLA_HEADS, MLA_NOPE + MLA_V)
    uk = jnp.pad(ukv[:, :, :MLA_NOPE], ((0, 0), (0, 0), (0, HEAD_PAD - MLA_NOPE)))
    uk = uk.reshape(MLA_KV_RANK, MLA_HEADS * HEAD_PAD)
    uv = ukv[:, :, MLA_NOPE:].reshape(MLA_KV_RANK, MLA_HEADS * MLA_V)
    ba = jnp.concatenate([col(4), col(5)], axis=1)
    w_dn = jnp.concatenate([col(3), col(6), _pad_cols(ba, 0, LANES)], axis=1)
    w_hg = jnp.concatenate([col(7), col(8), col(9), col(10)], axis=1)
    router = jnp.concatenate([p["router_group_w"][l], p["router_expert_w"][l]], axis=1)
    router = _pad_cols(router, 0, LANES)
    r_hi = router.astype(BF16)
    r_b = jnp.concatenate([p["router_group_b"][l], p["router_expert_b"][l]])
    bf = lambda a: a.astype(BF16)
    return dict(
        wcq=bf(col(0)), wckv=bf(col(1)), wkr=bf(_pad_cols(col(2), ROPE_LO, LANES)),
        qn=p["mla_q_norm"][l].reshape(1, -1), kvn=p["mla_kv_norm"][l].reshape(1, -1),
        wuq=bf(uq), wuk=bf(uk), wuv=bf(uv),
        w_dn=bf(w_dn), w_hg=bf(w_hg), wgate=bf(col(11)),
        dn_conv=p["dn_conv"][l],
        alog128=_pad_cols(p["dn_a_log"][l].reshape(1, -1), DN_HEADS, LANES),
        dtb128=_pad_cols(p["dn_dt_bias"][l].reshape(1, -1), DN_HEADS, LANES),
        dn_onorm=p["dn_o_norm"][l].reshape(1, -1), hg_onorm=p["hg_o_norm"][l].reshape(1, -1),
        wbr_a=bf(p["w_br_a"][l]), wbr_b=bf(p["w_br_b"][l]), wbr_c=bf(p["w_br_c"][l]),
        wout=bf(p["w_out"][l]),
        ln1_g=p["ln1_g"][l].reshape(1, d), ln1_b=p["ln1_b"][l].reshape(1, d),
        r_hi=r_hi, r_lo=bf(router - r_hi.astype(F32)), r_b=_pad_cols(r_b.reshape(1, -1), 0, LANES),
        e_gate=bf(p["exp_w_gate"][l]), e_up=bf(p["exp_w_up"][l]), e_down=bf(p["exp_w_down"][l]),
        ln2_g=p["ln2_g"][l].reshape(1, d), ln2_b=p["ln2_b"][l].reshape(1, d),
    )


def _forward(p):
    x = p["x"]
    batch, s, d = x.shape
    t = batch * s
    half = MLA_ROPE // 2
    inv_freq = ROPE_BASE ** (-jnp.arange(half, dtype=F32) / half)
    inv128 = _pad_cols(jnp.concatenate([inv_freq, inv_freq]).reshape(1, -1), ROPE_LO, LANES)
    tables = _rope_call(p["positions"].reshape(t, 1), inv128)
    h = _ln_call(x.reshape(t, d), p["ln_in_g"], p["ln_in_b"])
    for l in range(DEPTH):
        w = _layer_weights(l, p)
        q, k, v = _mla_proj_call(h, w, tables)
        oa = _flash_call(q, k, v, batch)
        x_dn = _proj_call(h, w["w_dn"], "proj_deltanet")
        ob = _dn_call(x_dn.reshape(batch, s, -1), w["dn_conv"], w["alog128"], w["dtb128"],
                      w["dn_onorm"])
        x_hg = _proj_call(h, w["w_hg"], "proj_hgrn2")
        oc = _hg_call(x_hg.reshape(batch, s, -1), p["hg_lower_bounds"], w["hg_onorm"], l)
        h1, cmb = _merge_call(h, oa, ob.reshape(t, -1), oc.reshape(t, -1), w)
        h = _moe_call(h1, cmb, w)
    return h.reshape(batch, s, d)


def kernel(x, positions, ln_in_g, ln_in_b, hg_lower_bounds, w_in, mla_q_norm, mla_w_uq, mla_kv_norm, mla_w_ukv, dn_conv, dn_a_log, dn_dt_bias, dn_o_norm, hg_o_norm, w_br_a, w_br_b, w_br_c, w_out, ln1_g, ln1_b, router_group_w, router_group_b, router_expert_w, router_expert_b, exp_w_gate, exp_w_up, exp_w_down, ln2_g, ln2_b):
    return _forward(dict(
        x=x, positions=positions, ln_in_g=ln_in_g, ln_in_b=ln_in_b,
        hg_lower_bounds=hg_lower_bounds, w_in=w_in, mla_q_norm=mla_q_norm, mla_w_uq=mla_w_uq,
        mla_kv_norm=mla_kv_norm, mla_w_ukv=mla_w_ukv, dn_conv=dn_conv, dn_a_log=dn_a_log,
        dn_dt_bias=dn_dt_bias, dn_o_norm=dn_o_norm, hg_o_norm=hg_o_norm, w_br_a=w_br_a,
        w_br_b=w_br_b, w_br_c=w_br_c, w_out=w_out, ln1_g=ln1_g, ln1_b=ln1_b,
        router_group_w=router_group_w, router_group_b=router_group_b,
        router_expert_w=router_expert_w, router_expert_b=router_expert_b,
        exp_w_gate=exp_w_gate, exp_w_up=exp_w_up, exp_w_down=exp_w_down, ln2_g=ln2_g, ln2_b=ln2_b))
```

```python
import functools
import math

import jax
import jax.numpy as jnp
import numpy as np
from jax import lax
from jax.experimental import pallas as pl
from jax.experimental.pallas import tpu as pltpu

D_MODEL = 1024
DEPTH = 2
MLA_HEADS = 8
MLA_Q_RANK = 256
MLA_KV_RANK = 128
MLA_NOPE = 64
MLA_ROPE = 32
MLA_V = 64
ROPE_BASE = 10000.0
MASK_VALUE = -1e30
DN_HEADS = 4
DN_DK = 128
DN_DV = 128
DN_CONV = 4
HG_HEADS = 4
HG_DK = 128
HG_DV = 128
CHUNK = 64
MIN_FORGET = 1e-30
N_GROUPS = 4
EXPERTS_PER_GROUP = 8
N_EXPERTS = N_GROUPS * EXPERTS_PER_GROUP
EXPERT_FF = 256
DEEPNORM_ALPHA = (2 * DEPTH) ** 0.25
NORM_EPS = 1e-6

LANES = 128
HEAD_PAD = 128
ROPE_LO = MLA_NOPE
ROPE_MID = MLA_NOPE + MLA_ROPE // 2
ROPE_HI = MLA_NOPE + MLA_ROPE
SCAN_ROWS = 256
ATTN_TILE = 512
VMEM_LIMIT = 56 * 1024 * 1024

BF16 = jnp.bfloat16
F32 = jnp.float32


def _dot(a, b):
    return jnp.dot(a.astype(BF16), b.astype(BF16), preferred_element_type=F32)


def _dot_nt(a, b):
    return lax.dot_general(a.astype(BF16), b.astype(BF16), (((1,), (1,)), ((), ())),
                           preferred_element_type=F32)


def _dot_tn(a, b):
    return lax.dot_general(a.astype(BF16), b.astype(BF16), (((0,), (0,)), ((), ())),
                           preferred_element_type=F32)


def _sigmoid(x):
    return 1.0 / (1.0 + jnp.exp(-x))


def _silu(x):
    return x * _sigmoid(x)


def _layer_norm(x, g, b):
    mu = jnp.mean(x, axis=-1, keepdims=True)
    xc = x - mu
    var = jnp.mean(xc * xc, axis=-1, keepdims=True)
    return xc * lax.rsqrt(var + NORM_EPS) * g + b


def _rms_norm(x, g):
    return x * lax.rsqrt(jnp.mean(x * x, axis=-1, keepdims=True) + NORM_EPS) * g


def _params(sem):
    return pltpu.CompilerParams(dimension_semantics=sem, vmem_limit_bytes=VMEM_LIMIT)


def _full(shape):
    n = len(shape)
    return pl.BlockSpec(shape, lambda *_: (0,) * n)


def _ln_kernel(x_ref, g_ref, b_ref, o_ref):
    o_ref[...] = _layer_norm(x_ref[...], g_ref[...], b_ref[...])


def _ln_call(x, g, b, tm=1024):
    t, d = x.shape
    return pl.pallas_call(
        _ln_kernel,
        out_shape=jax.ShapeDtypeStruct((t, d), F32),
        grid=(t // tm,),
        in_specs=[pl.BlockSpec((tm, d), lambda i: (i, 0)), _full((1, d)), _full((1, d))],
        out_specs=pl.BlockSpec((tm, d), lambda i: (i, 0)),
        compiler_params=_params(("parallel",)),
        name="ln_in",
    )(x, g.reshape(1, d), b.reshape(1, d))


def _rope_kernel(pos_ref, inv_ref, c_ref, sa_ref, sb_ref):
    ang = pos_ref[...].astype(F32) * inv_ref[...]
    lane = lax.broadcasted_iota(jnp.int32, ang.shape, 1)
    sin = jnp.sin(ang)
    c_ref[...] = jnp.cos(ang)
    sa_ref[...] = jnp.where((lane >= ROPE_MID) & (lane < ROPE_HI), sin, 0.0)
    sb_ref[...] = jnp.where((lane >= ROPE_LO) & (lane < ROPE_MID), -sin, 0.0)


def _rope_call(pos_col, inv128, tm=1024):
    t = pos_col.shape[0]
    spec = pl.BlockSpec((tm, LANES), lambda i: (i, 0))
    shp = jax.ShapeDtypeStruct((t, LANES), F32)
    return pl.pallas_call(
        _rope_kernel,
        out_shape=(shp, shp, shp),
        grid=(t // tm,),
        in_specs=[pl.BlockSpec((tm, 1), lambda i: (i, 0)), _full((1, LANES))],
        out_specs=(spec, spec, spec),
        compiler_params=_params(("parallel",)),
        name="rope_tables",
    )(pos_col, inv128)


def _rope_apply(x, c, sa, sb):
    return x * c + pltpu.roll(x, 16, 1) * sa + pltpu.roll(x, LANES - 16, 1) * sb


def _mla_proj_kernel(h_ref, wcq_ref, wckv_ref, wkr_ref, qn_ref, kvn_ref, wuq_ref, wuk_ref,
                     wuv_ref, c_ref, sa_ref, sb_ref, qt_ref, k_ref, vt_ref):
    hb = h_ref[...].astype(BF16)
    cq = _rms_norm(_dot(hb, wcq_ref[...]), qn_ref[...])
    ckv = _rms_norm(_dot(hb, wckv_ref[...]), kvn_ref[...])
    kr = _dot(hb, wkr_ref[...])
    q = _dot(cq, wuq_ref[...])
    k = _dot(ckv, wuk_ref[...])
    v = _dot(ckv, wuv_ref[...])
    for pr in range(MLA_HEADS // 2):
        sl = slice(pr * 2 * MLA_V, (pr + 1) * 2 * MLA_V)
        vt_ref[0, sl, :] = v[:, sl].T.astype(BF16)
    c, sa, sb = c_ref[...], sa_ref[...], sb_ref[...]
    scale = (MLA_NOPE + MLA_ROPE) ** -0.5
    kr = _rope_apply(kr, c, sa, sb)
    for hd in range(MLA_HEADS):
        sl = slice(hd * HEAD_PAD, (hd + 1) * HEAD_PAD)
        qt_ref[0, sl, :] = (_rope_apply(q[:, sl], c, sa, sb) * scale).T.astype(BF16)
        k_ref[:, sl] = (k[:, sl] + kr).astype(BF16)


def _mla_proj_call(h, w, tables, tm=ATTN_TILE):
    t, d = h.shape
    hq = MLA_HEADS * HEAD_PAD
    hv = MLA_HEADS * MLA_V
    row = lambda n: pl.BlockSpec((tm, n), lambda i: (i, 0))
    tile = lambda n: pl.BlockSpec((1, n, tm), lambda i: (i, 0, 0))
    return pl.pallas_call(
        _mla_proj_kernel,
        out_shape=(jax.ShapeDtypeStruct((t // tm, hq, tm), BF16),
                   jax.ShapeDtypeStruct((t, hq), BF16),
                   jax.ShapeDtypeStruct((t // tm, hv, tm), BF16)),
        grid=(t // tm,),
        in_specs=[row(d), _full((d, MLA_Q_RANK)), _full((d, MLA_KV_RANK)), _full((d, LANES)),
                  _full((1, MLA_Q_RANK)), _full((1, MLA_KV_RANK)), _full((MLA_Q_RANK, hq)),
                  _full((MLA_KV_RANK, hq)), _full((MLA_KV_RANK, hv)),
                  row(LANES), row(LANES), row(LANES)],
        out_specs=(tile(hq), row(hq), tile(hv)),
        compiler_params=_params(("parallel",)),
        name="mla_proj",
    )(h, w["wcq"], w["wckv"], w["wkr"], w["qn"], w["kvn"], w["wuq"], w["wuk"], w["wuv"], *tables)


def _flash_kernel(qt_ref, k_ref, vt_ref, o_ref, m0, l0, a0, m1, l1, a1, *, tq):
    qi = pl.program_id(2)
    stats = ((m0, l0, a0), (m1, l1, a1))
    for m_ref, l_ref, acc_ref in stats:
        m_ref[...] = jnp.full(m_ref.shape, MASK_VALUE, F32)
        l_ref[...] = jnp.zeros(l_ref.shape, F32)
        acc_ref[...] = jnp.zeros(acc_ref.shape, F32)

    def block(j, masked):
        rows = pl.ds(pl.multiple_of(j * tq, tq), tq)
        vt = vt_ref[0, j, 0]
        for hh, (m_ref, l_ref, acc_ref) in enumerate(stats):
            sl = slice(hh * HEAD_PAD, (hh + 1) * HEAD_PAD)
            st = jnp.dot(k_ref[0, rows, sl], qt_ref[0, 0, sl, :],
                         preferred_element_type=F32)
            if masked:
                key = lax.broadcasted_iota(jnp.int32, st.shape, 0)
                qry = lax.broadcasted_iota(jnp.int32, st.shape, 1)
                st = jnp.where(key <= qry, st, MASK_VALUE)
            m_old = m_ref[...]
            m_new = jnp.maximum(m_old, jnp.max(st, axis=0, keepdims=True))
            alpha = jnp.exp(m_old - m_new)
            p = jnp.exp(st - m_new)
            l_ref[...] = alpha * l_ref[...] + jnp.sum(p, axis=0, keepdims=True)
            acc_ref[...] = alpha * acc_ref[...] + jnp.dot(vt, p.astype(BF16),
                                                          preferred_element_type=F32)
            m_ref[...] = m_new

    def body(j, carry):
        block(j, False)
        return carry

    lax.fori_loop(0, qi, body, 0)
    block(qi, True)
    row = lax.broadcasted_iota(jnp.int32, a0.shape, 0)
    ot = jnp.where(row < MLA_V, a0[...] * (1.0 / l0[...]), a1[...] * (1.0 / l1[...]))
    o_ref[0] = ot.T.astype(BF16)


def _flash_call(qt, k, vt, batch, tq=ATTN_TILE):
    nt, hq, _ = qt.shape
    t = nt * tq
    s = t // batch
    ns = s // tq
    pairs = MLA_HEADS // 2
    hv = MLA_HEADS * MLA_V
    qt5 = qt.reshape(batch, ns, hq, tq)
    k3 = k.reshape(batch, s, hq)
    vt5 = vt.reshape(batch, ns, pairs, 2 * MLA_V, tq)
    stat = lambda: pltpu.VMEM((1, tq), F32)
    acc = lambda: pltpu.VMEM((2 * MLA_V, tq), F32)
    out = pl.pallas_call(
        functools.partial(_flash_kernel, tq=tq),
        out_shape=jax.ShapeDtypeStruct((batch, s, hv), BF16),
        grid=(batch, pairs, ns),
        in_specs=[pl.BlockSpec((1, 1, 2 * HEAD_PAD, tq), lambda b, p, i: (b, i, p, 0)),
                  pl.BlockSpec((1, s, 2 * HEAD_PAD), lambda b, p, i: (b, 0, p)),
                  pl.BlockSpec((1, ns, 1, 2 * MLA_V, tq), lambda b, p, i: (b, 0, p, 0, 0))],
        out_specs=pl.BlockSpec((1, tq, 2 * MLA_V), lambda b, p, i: (b, i, p)),
        scratch_shapes=[stat(), stat(), acc(), stat(), stat(), acc()],
        compiler_params=_params(("parallel", "parallel", "arbitrary")),
        name="flash_attn",
    )(qt5, k3, vt5)
    return out.reshape(t, hv)


def _proj_kernel(h_ref, w_ref, o_ref):
    o_ref[...] = _dot(h_ref[...], w_ref[...])


def _proj_call(h, w, name, tm=512):
    t, d = h.shape
    n = w.shape[1]
    return pl.pallas_call(
        _proj_kernel,
        out_shape=jax.ShapeDtypeStruct((t, n), F32),
        grid=(t // tm,),
        in_specs=[pl.BlockSpec((tm, d), lambda i: (i, 0)), _full((d, n))],
        out_specs=pl.BlockSpec((tm, n), lambda i: (i, 0)),
        compiler_params=_params(("parallel",)),
        name=name,
    )(h, w)


def _chunk_cumsum(x):
    rows = lax.broadcasted_iota(jnp.int32, x.shape, 0) % CHUNK
    shift = 1
    while shift < CHUNK:
        x = x + jnp.where(rows >= shift, pltpu.roll(x, shift, 0), 0.0)
        shift *= 2
    return x


def _chunk_last(x):
    r, n = x.shape
    x3 = x.reshape(r // CHUNK, CHUNK, n)
    return jnp.broadcast_to(x3[:, CHUNK - 1:CHUNK, :], x3.shape).reshape(r, n)


def _chunk_masks(r):
    i = lax.broadcasted_iota(jnp.int32, (r, r), 0)
    j = lax.broadcasted_iota(jnp.int32, (r, r), 1)
    same = (i // CHUNK) == (j // CHUNK)
    return same & (j <= i), same & (j < i), i == j


def _dn_kernel(x_ref, conv_ref, alog_ref, dtb_ref, onorm_ref, o_ref, halo_ref, state_ref,
               *, batch):
    r = SCAN_ROWS
    nqkv = DN_HEADS * (2 * DN_DK + DN_DV)
    gate_off = nqkv
    ba_off = nqkv + DN_HEADS * DN_DV

    @pl.when(pl.program_id(0) == 0)
    def _():
        halo_ref[...] = jnp.zeros(halo_ref.shape, F32)
        state_ref[...] = jnp.zeros(state_ref.shape, F32)

    incl, strict, diag = _chunk_masks(r)
    eye = jnp.where(diag, 1.0, 0.0)
    cw = conv_ref[...]
    pre = []
    for b in range(batch):
        xq = x_ref[b, :, :nqkv]
        xe = jnp.concatenate([halo_ref[b], xq], axis=0)
        halo_ref[b] = xq[r - 8:, :]
        y = xe[8:, :] * cw[DN_CONV - 1:DN_CONV, :]
        for i in range(DN_CONV - 1):
            y = y + pltpu.roll(xe, DN_CONV - 1 - i, 0)[8:, :] * cw[i:i + 1, :]
        y = _silu(y)
        ba = x_ref[b, :, ba_off:ba_off + LANES]
        beta_all = _sigmoid(ba)
        av = ba + dtb_ref[...]
        softplus = jnp.maximum(av, 0.0) + jnp.log(1.0 + jnp.exp(-jnp.abs(av)))
        g_all = _chunk_cumsum(-jnp.exp(alog_ref[...]) * softplus)
        g_all_t = g_all.T
        g_last_all = _chunk_last(g_all)
        for hd in range(DN_HEADS):
            q = y[:, hd * DN_DK:(hd + 1) * DN_DK]
            k = y[:, DN_HEADS * DN_DK + hd * DN_DK:DN_HEADS * DN_DK + (hd + 1) * DN_DK]
            v = y[:, 2 * DN_HEADS * DN_DK + hd * DN_DV:2 * DN_HEADS * DN_DK + (hd + 1) * DN_DV]
            q = q * lax.rsqrt(jnp.sum(q * q, axis=-1, keepdims=True) + NORM_EPS) * DN_DK ** -0.5
            k = k * lax.rsqrt(jnp.sum(k * k, axis=-1, keepdims=True) + NORM_EPS)
            beta = beta_all[:, hd:hd + 1]
            g_col = g_all[:, DN_HEADS + hd:DN_HEADS + hd + 1]
            g_row = g_all_t[DN_HEADS + hd:DN_HEADS + hd + 1, :]
            g_last = g_last_all[:, DN_HEADS + hd:DN_HEADS + hd + 1]
            decay = jnp.where(incl, jnp.exp(jnp.where(incl, g_col - g_row, 0.0)), 0.0)
            kb = k * beta
            kk = _dot_nt(kb, k)
            m = jnp.where(strict, kk * decay, 0.0)
            qk = jnp.where(incl, _dot_nt(q, k) * decay, 0.0)
            inv = eye - m
            pw = m
            for _ in range(5):
                pw = _dot(pw, pw)
                inv = inv + _dot(inv, pw)
            eg = jnp.exp(g_col)
            rhs = jnp.concatenate([v * beta, kb * eg], axis=1)
            uw = _dot(inv, rhs)
            pre.append(dict(u=uw[:, :DN_DV], w=uw[:, DN_DV:], qd=q * eg,
                            kd=k * jnp.exp(g_last - g_col), qk=qk, gl=jnp.exp(g_last)))
    outs = [[] for _ in pre]
    for c in range(r // CHUNK):
        rc = slice(c * CHUNK, (c + 1) * CHUNK)
        for idx, p in enumerate(pre):
            st = state_ref[idx]
            v_new = p["u"][rc] - _dot(p["w"][rc], st)
            outs[idx].append(_dot(p["qd"][rc], st) + _dot(p["qk"][rc, rc], v_new))
            state_ref[idx] = st * p["gl"][c * CHUNK:c * CHUNK + 1, :] + _dot_tn(p["kd"][rc], v_new)
    for idx in range(len(pre)):
        b, hd = divmod(idx, DN_HEADS)
        o = jnp.concatenate(outs[idx], axis=0)
        gate = x_ref[b, :, gate_off + hd * DN_DV:gate_off + (hd + 1) * DN_DV]
        o_ref[b, :, hd * DN_DV:(hd + 1) * DN_DV] = (
            _rms_norm(o, onorm_ref[...]) * _silu(gate)).astype(BF16)


def _dn_call(x3, conv_w, alog128, dtb128, onorm):
    batch, s, n = x3.shape
    r = SCAN_ROWS
    nqkv = DN_HEADS * (2 * DN_DK + DN_DV)
    width = DN_HEADS * DN_DV
    return pl.pallas_call(
        functools.partial(_dn_kernel, batch=batch),
        out_shape=jax.ShapeDtypeStruct((batch, s, width), BF16),
        grid=(s // r,),
        in_specs=[pl.BlockSpec((batch, r, n), lambda i: (0, i, 0)), _full((DN_CONV, nqkv)),
                  _full((1, LANES)), _full((1, LANES)), _full((1, DN_DV))],
        out_specs=pl.BlockSpec((batch, r, width), lambda i: (0, i, 0)),
        scratch_shapes=[pltpu.VMEM((batch, 8, nqkv), F32),
                        pltpu.VMEM((batch * DN_HEADS, DN_DK, DN_DV), F32)],
        compiler_params=_params(("arbitrary",)),
        name="deltanet",
    )(x3, conv_w, alog128, dtb128, onorm)


HG_LEVELS = (32, 16, 8, 4, 2, 1)


def _hg_kernel(x_ref, lbp_ref, onorm_ref, sel_ref, o_ref, state_ref, *, batch, layer):
    r = SCAN_ROWS
    width = HG_HEADS * HG_DK

    @pl.when(pl.program_id(0) == 0)
    def _():
        state_ref[...] = jnp.zeros(state_ref.shape, F32)

    lbp = lbp_ref[...]
    e = jnp.exp(lbp - jnp.max(lbp, axis=0, keepdims=True))
    soft = e / jnp.sum(e, axis=0, keepdims=True)
    lb_all = jnp.zeros((1, width), F32)
    for i in range(1, layer + 1):
        lb_all = lb_all + soft[i:i + 1, :]

    i_idx = lax.broadcasted_iota(jnp.int32, (r, r), 0)
    j_idx = lax.broadcasted_iota(jnp.int32, (r, r), 1)
    level_masks = [((i_idx // (2 * s)) == (j_idx // (2 * s))) & ((i_idx % (2 * s)) >= s)
                   & ((j_idx % (2 * s)) < s) for s in HG_LEVELS]
    diag = i_idx == j_idx

    for b in range(batch):
        for hd in range(HG_HEADS):
            idx = b * HG_HEADS + hd
            hs = slice(hd * HG_DK, (hd + 1) * HG_DK)
            lb = lb_all[:, hs]
            q = _silu(x_ref[b, :, hd * HG_DK:(hd + 1) * HG_DK])
            z = x_ref[b, :, width + hd * HG_DK:width + (hd + 1) * HG_DK]
            v = x_ref[b, :, 2 * width + hd * HG_DV:2 * width + (hd + 1) * HG_DV]
            gate = x_ref[b, :, 3 * width + hd * HG_DV:3 * width + (hd + 1) * HG_DV]
            f = lb + (1.0 - lb) * _sigmoid(z)
            k = (1.0 - lb) * _sigmoid(-z)
            cum = _chunk_cumsum(jnp.log(jnp.maximum(f, MIN_FORGET)))
            cum_last = _chunk_last(cum)
            qd = q * jnp.exp(cum)
            kd = k * jnp.exp(cum_last - cum)
            dl = jnp.exp(cum_last)
            att = jnp.where(diag, _dot_nt(q, k), 0.0)
            cum_b = cum.astype(BF16)
            for lv, mask in enumerate(level_masks):
                ref = jnp.dot(sel_ref[lv], cum_b, preferred_element_type=F32)
                qs = q * jnp.exp(jnp.minimum(cum - ref, 60.0))
                ks = k * jnp.exp(jnp.minimum(ref - cum, 60.0))
                att = att + jnp.where(mask, _dot_nt(qs, ks), 0.0)
            o_intra = _dot(att, v)
            outs = []
            st = state_ref[idx]
            for c in range(r // CHUNK):
                rc = slice(c * CHUNK, (c + 1) * CHUNK)
                outs.append(_dot_nt(qd[rc], st) + o_intra[rc])
                st = st * dl[c * CHUNK:c * CHUNK + 1, :] + _dot_tn(v[rc], kd[rc])
            state_ref[idx] = st
            o = jnp.concatenate(outs, axis=0)
            o_ref[b, :, hs] = (_rms_norm(o, onorm_ref[...]) * _silu(gate)).astype(BF16)


def _hg_sel():
    r = SCAN_ROWS
    i = np.arange(r)
    sel = np.zeros((len(HG_LEVELS), r, r), np.float32)
    for lv, s in enumerate(HG_LEVELS):
        sel[lv, i, (i // (2 * s)) * (2 * s) + s] = 1.0
    return jnp.asarray(sel, BF16)


def _hg_call(x3, lbp, onorm, layer):
    batch, s, n = x3.shape
    r = SCAN_ROWS
    width = HG_HEADS * HG_DV
    return pl.pallas_call(
        functools.partial(_hg_kernel, batch=batch, layer=layer),
        out_shape=jax.ShapeDtypeStruct((batch, s, width), BF16),
        grid=(s // r,),
        in_specs=[pl.BlockSpec((batch, r, n), lambda i: (0, i, 0)), _full(lbp.shape),
                  _full((1, HG_DV)), _full((len(HG_LEVELS), r, r))],
        out_specs=pl.BlockSpec((batch, r, width), lambda i: (0, i, 0)),
        scratch_shapes=[pltpu.VMEM((batch * HG_HEADS, HG_DV, HG_DK), F32)],
        compiler_params=_params(("arbitrary",)),
        name="hgrn2",
    )(x3, lbp, onorm, _hg_sel())


def _merge_kernel(h_ref, oa_ref, ob_ref, oc_ref, wgate_ref, wa_ref, wb_ref, wc_ref, wout_ref,
                  g_ref, b_ref, rhi_ref, rlo_ref, rb_ref, h1_ref, cmb_ref):
    h = h_ref[...]
    gates = _sigmoid(_dot(h, wgate_ref[...]))
    d = D_MODEL
    mixed = (gates[:, :d] * _dot(oa_ref[...], wa_ref[...])
             + gates[:, d:2 * d] * _dot(ob_ref[...], wb_ref[...])
             + gates[:, 2 * d:] * _dot(oc_ref[...], wc_ref[...]))
    h1 = _layer_norm(DEEPNORM_ALPHA * h + _dot(mixed, wout_ref[...]), g_ref[...], b_ref[...])
    h1_ref[...] = h1
    hi = h1.astype(BF16)
    lo = (h1 - hi.astype(F32)).astype(BF16)
    logits = (jnp.dot(hi, rhi_ref[...], preferred_element_type=F32)
              + jnp.dot(hi, rlo_ref[...], preferred_element_type=F32)
              + jnp.dot(lo, rhi_ref[...], preferred_element_type=F32)) + rb_ref[...]
    lane = lax.broadcasted_iota(jnp.int32, logits.shape, 1)
    neg = jnp.float32(-jnp.inf)
    big = jnp.int32(1 << 20)
    is_g = lane < N_GROUPS
    gl = jnp.where(is_g, logits, neg)
    gmax = jnp.max(gl, axis=-1, keepdims=True)
    gidx = jnp.min(jnp.where(gl == gmax, lane, big), axis=-1, keepdims=True)
    p_group = 1.0 / jnp.sum(jnp.where(is_g, jnp.exp(gl - gmax), 0.0), axis=-1, keepdims=True)
    lo_lane = N_GROUPS + gidx * EXPERTS_PER_GROUP
    in_g = (lane >= lo_lane) & (lane < lo_lane + EXPERTS_PER_GROUP)
    el = jnp.where(in_g, logits, neg)
    v1 = jnp.max(el, axis=-1, keepdims=True)
    i1 = jnp.min(jnp.where(el == v1, lane, big), axis=-1, keepdims=True)
    el2 = jnp.where(lane == i1, neg, el)
    v2 = jnp.max(el2, axis=-1, keepdims=True)
    i2 = jnp.min(jnp.where(el2 == v2, lane, big), axis=-1, keepdims=True)
    e21 = jnp.exp(v2 - v1)
    w1 = p_group / (1.0 + e21)
    w2 = p_group * e21 / (1.0 + e21)
    cmb_ref[...] = jnp.where(lane == i1, w1, 0.0) + jnp.where(lane == i2, w2, 0.0)


def _merge_call(h, oa, ob, oc, w, tm=256):
    t, d = h.shape
    row = lambda n: pl.BlockSpec((tm, n), lambda i: (i, 0))
    wa, wb, wc = w["wbr_a"], w["wbr_b"], w["wbr_c"]
    return pl.pallas_call(
        _merge_kernel,
        out_shape=(jax.ShapeDtypeStruct((t, d), F32), jax.ShapeDtypeStruct((t, LANES), F32)),
        grid=(t // tm,),
        in_specs=[row(d), row(oa.shape[1]), row(ob.shape[1]), row(oc.shape[1]),
                  _full((d, 3 * d)), _full(wa.shape), _full(wb.shape), _full(wc.shape),
                  _full((d, d)), _full((1, d)), _full((1, d)),
                  _full((d, LANES)), _full((d, LANES)), _full((1, LANES))],
        out_specs=(row(d), row(LANES)),
        compiler_params=_params(("parallel",)),
        name="merge_ln_router",
    )(h, oa, ob, oc, w["wgate"], wa, wb, wc, w["wout"], w["ln1_g"], w["ln1_b"],
      w["r_hi"], w["r_lo"], w["r_b"])


def _moe_kernel(h_ref, cmb_ref, wg_ref, wu_ref, wd_ref, g_ref, b_ref, o_ref, acc_ref):
    e = pl.program_id(1)

    @pl.when(e == 0)
    def _():
        acc_ref[...] = jnp.zeros(acc_ref.shape, F32)

    hb = h_ref[...].astype(BF16)
    lane = lax.broadcasted_iota(jnp.int32, cmb_ref.shape, 1)
    w_col = jnp.sum(jnp.where(lane == N_GROUPS + e, cmb_ref[...], 0.0), axis=-1, keepdims=True)
    act = _silu(_dot(hb, wg_ref[0])) * _dot(hb, wu_ref[0]) * w_col
    acc_ref[...] += _dot(act, wd_ref[0])

    @pl.when(e == N_EXPERTS - 1)
    def _():
        o_ref[...] = _layer_norm(DEEPNORM_ALPHA * h_ref[...] + acc_ref[...], g_ref[...], b_ref[...])


def _moe_call(h1, cmb, w, tm=1024):
    t, d = h1.shape
    return pl.pallas_call(
        _moe_kernel,
        out_shape=jax.ShapeDtypeStruct((t, d), F32),
        grid=(t // tm, N_EXPERTS),
        in_specs=[pl.BlockSpec((tm, d), lambda i, e: (i, 0)),
                  pl.BlockSpec((tm, LANES), lambda i, e: (i, 0)),
                  pl.BlockSpec((1, d, EXPERT_FF), lambda i, e: (e, 0, 0)),
                  pl.BlockSpec((1, d, EXPERT_FF), lambda i, e: (e, 0, 0)),
                  pl.BlockSpec((1, EXPERT_FF, d), lambda i, e: (e, 0, 0)),
                  pl.BlockSpec((1, d), lambda i, e: (0, 0)),
                  pl.BlockSpec((1, d), lambda i, e: (0, 0))],
        out_specs=pl.BlockSpec((tm, d), lambda i, e: (i, 0)),
        scratch_shapes=[pltpu.VMEM((tm, d), F32)],
        compiler_params=_params(("parallel", "arbitrary")),
        name="moe_experts",
    )(h1, cmb, w["e_gate"], w["e_up"], w["e_down"], w["ln2_g"], w["ln2_b"])


def _pad_cols(w, lo, total):
    return jnp.zeros((w.shape[0], total), w.dtype).at[:, lo:lo + w.shape[1]].set(w)


def _layer_weights(l, p):
    d = D_MODEL
    w_in = p["w_in"][l]
    sizes = (MLA_Q_RANK, MLA_KV_RANK, MLA_ROPE, DN_HEADS * (2 * DN_DK + DN_DV), DN_HEADS, DN_HEADS,
             DN_HEADS * DN_DV, HG_HEADS * HG_DK, HG_HEADS * HG_DK, HG_HEADS * HG_DV,
             HG_HEADS * HG_DV, 3 * d)
    offs = np.concatenate([[0], np.cumsum(sizes)])
    col = lambda i: w_in[:, offs[i]:offs[i + 1]]
    qk_w = MLA_NOPE + MLA_ROPE
    uq = p["mla_w_uq"][l].reshape(MLA_Q_RANK, MLA_HEADS, qk_w)
    uq = jnp.pad(uq, ((0, 0), (0, 0), (0, HEAD_PAD - qk_w))).reshape(MLA_Q_RANK, MLA_HEADS * HEAD_PAD)
    ukv = p["mla_w_ukv"][l].reshape(MLA_KV_RANK, MLA_HEADS, MLA_NOPE + MLA_V)
    uk = jnp.pad(ukv[:, :, :MLA_NOPE], ((0, 0), (0, 0), (0, HEAD_PAD - MLA_NOPE)))
    uk = uk.reshape(MLA_KV_RANK, MLA_HEADS * HEAD_PAD)
    uv = ukv[:, :, MLA_NOPE:].reshape(MLA_KV_RANK, MLA_HEADS * MLA_V)
    ba = jnp.concatenate([col(4), col(5)], axis=1)
    w_dn = jnp.concatenate([col(3), col(6), _pad_cols(ba, 0, LANES)], axis=1)
    w_hg = jnp.concatenate([col(7), col(8), col(9), col(10)], axis=1)
    router = jnp.concatenate([p["router_group_w"][l], p["router_expert_w"][l]], axis=1)
    router = _pad_cols(router, 0, LANES)
    r_hi = router.astype(BF16)
    r_b = jnp.concatenate([p["router_group_b"][l], p["router_expert_b"][l]])
    bf = lambda a: a.astype(BF16)
    return dict(
        wcq=bf(col(0)), wckv=bf(col(1)), wkr=bf(_pad_cols(col(2), ROPE_LO, LANES)),
        qn=p["mla_q_norm"][l].reshape(1, -1), kvn=p["mla_kv_norm"][l].reshape(1, -1),
        wuq=bf(uq), wuk=bf(uk), wuv=bf(uv),
        w_dn=bf(w_dn), w_hg=bf(w_hg), wgate=bf(col(11)),
        dn_conv=p["dn_conv"][l],
        alog128=_pad_cols(p["dn_a_log"][l].reshape(1, -1), DN_HEADS, LANES),
        dtb128=_pad_cols(p["dn_dt_bias"][l].reshape(1, -1), DN_HEADS, LANES),
        dn_onorm=p["dn_o_norm"][l].reshape(1, -1), hg_onorm=p["hg_o_norm"][l].reshape(1, -1),
        wbr_a=bf(p["w_br_a"][l]), wbr_b=bf(p["w_br_b"][l]), wbr_c=bf(p["w_br_c"][l]),
        wout=bf(p["w_out"][l]),
        ln1_g=p["ln1_g"][l].reshape(1, d), ln1_b=p["ln1_b"][l].reshape(1, d),
        r_hi=r_hi, r_lo=bf(router - r_hi.astype(F32)), r_b=_pad_cols(r_b.reshape(1, -1), 0, LANES),
        e_gate=bf(p["exp_w_gate"][l]), e_up=bf(p["exp_w_up"][l]), e_down=bf(p["exp_w_down"][l]),
        ln2_g=p["ln2_g"][l].reshape(1, d), ln2_b=p["ln2_b"][l].reshape(1, d),
    )


def _forward(p):
    x = p["x"]
    batch, s, d = x.shape
    t = batch * s
    half = MLA_ROPE // 2
    inv_freq = ROPE_BASE ** (-jnp.arange(half, dtype=F32) / half)
    inv128 = _pad_cols(jnp.concatenate([inv_freq, inv_freq]).reshape(1, -1), ROPE_LO, LANES)
    tables = _rope_call(p["positions"].reshape(t, 1), inv128)
    h = _ln_call(x.reshape(t, d), p["ln_in_g"], p["ln_in_b"])
    for l in range(DEPTH):
        w = _layer_weights(l, p)
        q, k, v = _mla_proj_call(h, w, tables)
        oa = _flash_call(q, k, v, batch)
        x_dn = _proj_call(h, w["w_dn"], "proj_deltanet")
        ob = _dn_call(x_dn.reshape(batch, s, -1), w["dn_conv"], w["alog128"], w["dtb128"],
                      w["dn_onorm"])
        x_hg = _proj_call(h, w["w_hg"], "proj_hgrn2")
        oc = _hg_call(x_hg.reshape(batch, s, -1), p["hg_lower_bounds"], w["hg_onorm"], l)
        h1, cmb = _merge_call(h, oa, ob.reshape(t, -1), oc.reshape(t, -1), w)
        h = _moe_call(h1, cmb, w)
    return h.reshape(batch, s, d)


def kernel(x, positions, ln_in_g, ln_in_b, hg_lower_bounds, w_in, mla_q_norm, mla_w_uq, mla_kv_norm, mla_w_ukv, dn_conv, dn_a_log, dn_dt_bias, dn_o_norm, hg_o_norm, w_br_a, w_br_b, w_br_c, w_out, ln1_g, ln1_b, router_group_w, router_group_b, router_expert_w, router_expert_b, exp_w_gate, exp_w_up, exp_w_down, ln2_g, ln2_b):
    return _forward(dict(
        x=x, positions=positions, ln_in_g=ln_in_g, ln_in_b=ln_in_b,
        hg_lower_bounds=hg_lower_bounds, w_in=w_in, mla_q_norm=mla_q_norm, mla_w_uq=mla_w_uq,
        mla_kv_norm=mla_kv_norm, mla_w_ukv=mla_w_ukv, dn_conv=dn_conv, dn_a_log=dn_a_log,
        dn_dt_bias=dn_dt_bias, dn_o_norm=dn_o_norm, hg_o_norm=hg_o_norm, w_br_a=w_br_a,
        w_br_b=w_br_b, w_br_c=w_br_c, w_out=w_out, ln1_g=ln1_g, ln1_b=ln1_b,
        router_group_w=router_group_w, router_group_b=router_group_b,
        router_expert_w=router_expert_w, router_expert_b=router_expert_b,
        exp_w_gate=exp_w_gate, exp_w_up=exp_w_up, exp_w_down=exp_w_down, ln2_g=ln2_g, ln2_b=ln2_b))
```

```python
import functools
import math

import jax
import jax.numpy as jnp
import numpy as np
from jax import lax
from jax.experimental import pallas as pl
from jax.experimental.pallas import tpu as pltpu

D_MODEL = 1024
DEPTH = 2
MLA_HEADS = 8
MLA_Q_RANK = 256
MLA_KV_RANK = 128
MLA_NOPE = 64
MLA_ROPE = 32
MLA_V = 64
ROPE_BASE = 10000.0
MASK_VALUE = -1e30
DN_HEADS = 4
DN_DK = 128
DN_DV = 128
DN_CONV = 4
HG_HEADS = 4
HG_DK = 128
HG_DV = 128
CHUNK = 64
MIN_FORGET = 1e-30
N_GROUPS = 4
EXPERTS_PER_GROUP = 8
N_EXPERTS = N_GROUPS * EXPERTS_PER_GROUP
EXPERT_FF = 256
DEEPNORM_ALPHA = (2 * DEPTH) ** 0.25
NORM_EPS = 1e-6

LANES = 128
HEAD_PAD = 128
ROPE_LO = MLA_NOPE
ROPE_MID = MLA_NOPE + MLA_ROPE // 2
ROPE_HI = MLA_NOPE + MLA_ROPE
SCAN_ROWS = 256
ATTN_TILE = 512
VMEM_LIMIT = 56 * 1024 * 1024

BF16 = jnp.bfloat16
F32 = jnp.float32


def _dot(a, b):
    return jnp.dot(a.astype(BF16), b.astype(BF16), preferred_element_type=F32)


def _dot_nt(a, b):
    return lax.dot_general(a.astype(BF16), b.astype(BF16), (((1,), (1,)), ((), ())),
                           preferred_element_type=F32)


def _dot_tn(a, b):
    return lax.dot_general(a.astype(BF16), b.astype(BF16), (((0,), (0,)), ((), ())),
                           preferred_element_type=F32)


def _sigmoid(x):
    return 1.0 / (1.0 + jnp.exp(-x))


def _silu(x):
    return x * _sigmoid(x)


def _layer_norm(x, g, b):
    mu = jnp.mean(x, axis=-1, keepdims=True)
    xc = x - mu
    var = jnp.mean(xc * xc, axis=-1, keepdims=True)
    return xc * lax.rsqrt(var + NORM_EPS) * g + b


def _rms_norm(x, g):
    return x * lax.rsqrt(jnp.mean(x * x, axis=-1, keepdims=True) + NORM_EPS) * g


def _params(sem):
    return pltpu.CompilerParams(dimension_semantics=sem, vmem_limit_bytes=VMEM_LIMIT)


def _full(shape):
    n = len(shape)
    return pl.BlockSpec(shape, lambda *_: (0,) * n)


def _ln_kernel(x_ref, g_ref, b_ref, o_ref):
    o_ref[...] = _layer_norm(x_ref[...], g_ref[...], b_ref[...])


def _ln_call(x, g, b, tm=1024):
    t, d = x.shape
    return pl.pallas_call(
        _ln_kernel,
        out_shape=jax.ShapeDtypeStruct((t, d), F32),
        grid=(t // tm,),
        in_specs=[pl.BlockSpec((tm, d), lambda i: (i, 0)), _full((1, d)), _full((1, d))],
        out_specs=pl.BlockSpec((tm, d), lambda i: (i, 0)),
        compiler_params=_params(("parallel",)),
        name="ln_in",
    )(x, g.reshape(1, d), b.reshape(1, d))


def _rope_kernel(pos_ref, inv_ref, c_ref, sa_ref, sb_ref):
    ang = pos_ref[...].astype(F32) * inv_ref[...]
    lane = lax.broadcasted_iota(jnp.int32, ang.shape, 1)
    sin = jnp.sin(ang)
    c_ref[...] = jnp.cos(ang)
    sa_ref[...] = jnp.where((lane >= ROPE_MID) & (lane < ROPE_HI), sin, 0.0)
    sb_ref[...] = jnp.where((lane >= ROPE_LO) & (lane < ROPE_MID), -sin, 0.0)


def _rope_call(pos_col, inv128, tm=1024):
    t = pos_col.shape[0]
    spec = pl.BlockSpec((tm, LANES), lambda i: (i, 0))
    shp = jax.ShapeDtypeStruct((t, LANES), F32)
    return pl.pallas_call(
        _rope_kernel,
        out_shape=(shp, shp, shp),
        grid=(t // tm,),
        in_specs=[pl.BlockSpec((tm, 1), lambda i: (i, 0)), _full((1, LANES))],
        out_specs=(spec, spec, spec),
        compiler_params=_params(("parallel",)),
        name="rope_tables",
    )(pos_col, inv128)


def _rope_apply(x, c, sa, sb):
    return x * c + pltpu.roll(x, 16, 1) * sa + pltpu.roll(x, LANES - 16, 1) * sb


def _mla_proj_kernel(h_ref, wcq_ref, wckv_ref, wkr_ref, qn_ref, kvn_ref, wuq_ref, wuk_ref,
                     wuv_ref, c_ref, sa_ref, sb_ref, qt_ref, k_ref, vt_ref):
    hb = h_ref[...].astype(BF16)
    cq = _rms_norm(_dot(hb, wcq_ref[...]), qn_ref[...])
    ckv = _rms_norm(_dot(hb, wckv_ref[...]), kvn_ref[...])
    kr = _dot(hb, wkr_ref[...])
    q = _dot(cq, wuq_ref[...])
    k = _dot(ckv, wuk_ref[...])
    v = _dot(ckv, wuv_ref[...])
    for pr in range(MLA_HEADS // 2):
        sl = slice(pr * 2 * MLA_V, (pr + 1) * 2 * MLA_V)
        vt_ref[0, sl, :] = v[:, sl].T.astype(BF16)
    c, sa, sb = c_ref[...], sa_ref[...], sb_ref[...]
    scale = (MLA_NOPE + MLA_ROPE) ** -0.5 * math.log2(math.e)
    kr = _rope_apply(kr, c, sa, sb)
    for hd in range(MLA_HEADS):
        sl = slice(hd * HEAD_PAD, (hd + 1) * HEAD_PAD)
        qt_ref[0, sl, :] = (_rope_apply(q[:, sl], c, sa, sb) * scale).T.astype(BF16)
        k_ref[:, sl] = (k[:, sl] + kr).astype(BF16)


def _mla_proj_call(h, w, tables, tm=ATTN_TILE):
    t, d = h.shape
    hq = MLA_HEADS * HEAD_PAD
    hv = MLA_HEADS * MLA_V
    row = lambda n: pl.BlockSpec((tm, n), lambda i: (i, 0))
    tile = lambda n: pl.BlockSpec((1, n, tm), lambda i: (i, 0, 0))
    return pl.pallas_call(
        _mla_proj_kernel,
        out_shape=(jax.ShapeDtypeStruct((t // tm, hq, tm), BF16),
                   jax.ShapeDtypeStruct((t, hq), BF16),
                   jax.ShapeDtypeStruct((t // tm, hv, tm), BF16)),
        grid=(t // tm,),
        in_specs=[row(d), _full((d, MLA_Q_RANK)), _full((d, MLA_KV_RANK)), _full((d, LANES)),
                  _full((1, MLA_Q_RANK)), _full((1, MLA_KV_RANK)), _full((MLA_Q_RANK, hq)),
                  _full((MLA_KV_RANK, hq)), _full((MLA_KV_RANK, hv)),
                  row(LANES), row(LANES), row(LANES)],
        out_specs=(tile(hq), row(hq), tile(hv)),
        compiler_params=_params(("parallel",)),
        name="mla_proj",
    )(h, w["wcq"], w["wckv"], w["wkr"], w["qn"], w["kvn"], w["wuq"], w["wuk"], w["wuv"], *tables)


def _flash_kernel(qt_ref, k_ref, vt_ref, o_ref, m0, l0, a0, m1, l1, a1, sa_ref, sb_ref,
                  *, tq):
    qi = pl.program_id(2)
    stats = ((m0, l0, a0), (m1, l1, a1))
    for m_ref, l_ref, acc_ref in stats:
        m_ref[...] = jnp.full(m_ref.shape, MASK_VALUE, F32)
        l_ref[...] = jnp.zeros(l_ref.shape, F32)
        acc_ref[...] = jnp.zeros(acc_ref.shape, F32)

    def score(j, s_ref):
        rows = pl.ds(pl.multiple_of(j * tq, tq), tq)
        for hh in range(2):
            sl = slice(hh * HEAD_PAD, (hh + 1) * HEAD_PAD)
            s_ref[hh] = jnp.dot(k_ref[0, rows, sl], qt_ref[0, 0, sl, :],
                                preferred_element_type=F32)

    def consume(j, s_ref, masked):
        vt = vt_ref[0, j, 0]
        for hh, (m_ref, l_ref, acc_ref) in enumerate(stats):
            st = s_ref[hh]
            if masked:
                key = lax.broadcasted_iota(jnp.int32, st.shape, 0)
                qry = lax.broadcasted_iota(jnp.int32, st.shape, 1)
                st = jnp.where(key <= qry, st, MASK_VALUE)
            m_old = m_ref[...]
            m_new = jnp.maximum(m_old, jnp.max(st, axis=0, keepdims=True))
            alpha = jnp.exp2(m_old - m_new)
            p = jnp.exp2(st - m_new)
            l_ref[...] = alpha * l_ref[...] + jnp.sum(p, axis=0, keepdims=True)
            acc_ref[...] = alpha * acc_ref[...] + jnp.dot(vt, p.astype(BF16),
                                                          preferred_element_type=F32)
            m_ref[...] = m_new

    score(0, sa_ref)

    def body(i, carry):
        j = 2 * i
        score(j + 1, sb_ref)
        consume(j, sa_ref, False)
        score(j + 2, sa_ref)
        consume(j + 1, sb_ref, False)
        return carry

    lax.fori_loop(0, qi // 2, body, 0)

    @pl.when(qi % 2 == 0)
    def _():
        consume(qi, sa_ref, True)

    @pl.when(qi % 2 == 1)
    def _():
        score(qi, sb_ref)
        consume(qi - 1, sa_ref, False)
        consume(qi, sb_ref, True)
    row = lax.broadcasted_iota(jnp.int32, a0.shape, 0)
    ot = jnp.where(row < MLA_V, a0[...] * (1.0 / l0[...]), a1[...] * (1.0 / l1[...]))
    o_ref[0] = ot.T.astype(BF16)


def _flash_call(qt, k, vt, batch, tq=ATTN_TILE):
    nt, hq, _ = qt.shape
    t = nt * tq
    s = t // batch
    ns = s // tq
    pairs = MLA_HEADS // 2
    hv = MLA_HEADS * MLA_V
    qt5 = qt.reshape(batch, ns, hq, tq)
    k3 = k.reshape(batch, s, hq)
    vt5 = vt.reshape(batch, ns, pairs, 2 * MLA_V, tq)
    stat = lambda: pltpu.VMEM((1, tq), F32)
    acc = lambda: pltpu.VMEM((2 * MLA_V, tq), F32)
    out = pl.pallas_call(
        functools.partial(_flash_kernel, tq=tq),
        out_shape=jax.ShapeDtypeStruct((batch, s, hv), BF16),
        grid=(batch, pairs, ns),
        in_specs=[pl.BlockSpec((1, 1, 2 * HEAD_PAD, tq), lambda b, p, i: (b, i, p, 0)),
                  pl.BlockSpec((1, s, 2 * HEAD_PAD), lambda b, p, i: (b, 0, p)),
                  pl.BlockSpec((1, ns, 1, 2 * MLA_V, tq), lambda b, p, i: (b, 0, p, 0, 0))],
        out_specs=pl.BlockSpec((1, tq, 2 * MLA_V), lambda b, p, i: (b, i, p)),
        scratch_shapes=[stat(), stat(), acc(), stat(), stat(), acc(),
                        pltpu.VMEM((2, tq, tq), F32), pltpu.VMEM((2, tq, tq), F32)],
        compiler_params=_params(("parallel", "parallel", "arbitrary")),
        name="flash_attn",
    )(qt5, k3, vt5)
    return out.reshape(t, hv)


def _proj_kernel(h_ref, w_ref, o_ref):
    o_ref[...] = _dot(h_ref[...], w_ref[...])


def _proj_call(h, w, name, tm=512):
    t, d = h.shape
    n = w.shape[1]
    return pl.pallas_call(
        _proj_kernel,
        out_shape=jax.ShapeDtypeStruct((t, n), F32),
        grid=(t // tm,),
        in_specs=[pl.BlockSpec((tm, d), lambda i: (i, 0)), _full((d, n))],
        out_specs=pl.BlockSpec((tm, n), lambda i: (i, 0)),
        compiler_params=_params(("parallel",)),
        name=name,
    )(h, w)


def _chunk_cumsum(x):
    rows = lax.broadcasted_iota(jnp.int32, x.shape, 0) % CHUNK
    shift = 1
    while shift < CHUNK:
        x = x + jnp.where(rows >= shift, pltpu.roll(x, shift, 0), 0.0)
        shift *= 2
    return x


def _chunk_last(x):
    r, n = x.shape
    x3 = x.reshape(r // CHUNK, CHUNK, n)
    return jnp.broadcast_to(x3[:, CHUNK - 1:CHUNK, :], x3.shape).reshape(r, n)


def _chunk_masks(r):
    i = lax.broadcasted_iota(jnp.int32, (r, r), 0)
    j = lax.broadcasted_iota(jnp.int32, (r, r), 1)
    same = (i // CHUNK) == (j // CHUNK)
    return same & (j <= i), same & (j < i), i == j


def _dn_kernel(x_ref, conv_ref, alog_ref, dtb_ref, onorm_ref, o_ref, halo_ref, state_ref,
               *, batch):
    r = SCAN_ROWS
    nqkv = DN_HEADS * (2 * DN_DK + DN_DV)
    gate_off = nqkv
    ba_off = nqkv + DN_HEADS * DN_DV

    @pl.when(pl.program_id(0) == 0)
    def _():
        halo_ref[...] = jnp.zeros(halo_ref.shape, F32)
        state_ref[...] = jnp.zeros(state_ref.shape, F32)

    incl, strict, diag = _chunk_masks(r)
    eye = jnp.where(diag, 1.0, 0.0)
    cw = conv_ref[...]
    pre = []
    for b in range(batch):
        xq = x_ref[b, :, :nqkv]
        xe = jnp.concatenate([halo_ref[b], xq], axis=0)
        halo_ref[b] = xq[r - 8:, :]
        y = xe[8:, :] * cw[DN_CONV - 1:DN_CONV, :]
        for i in range(DN_CONV - 1):
            y = y + pltpu.roll(xe, DN_CONV - 1 - i, 0)[8:, :] * cw[i:i + 1, :]
        y = _silu(y)
        ba = x_ref[b, :, ba_off:ba_off + LANES]
        beta_all = _sigmoid(ba)
        av = ba + dtb_ref[...]
        softplus = jnp.maximum(av, 0.0) + jnp.log(1.0 + jnp.exp(-jnp.abs(av)))
        g_all = _chunk_cumsum(-jnp.exp(alog_ref[...]) * softplus)
        g_all_t = g_all.T
        g_last_all = _chunk_last(g_all)
        for hd in range(DN_HEADS):
            q = y[:, hd * DN_DK:(hd + 1) * DN_DK]
            k = y[:, DN_HEADS * DN_DK + hd * DN_DK:DN_HEADS * DN_DK + (hd + 1) * DN_DK]
            v = y[:, 2 * DN_HEADS * DN_DK + hd * DN_DV:2 * DN_HEADS * DN_DK + (hd + 1) * DN_DV]
            q = q * lax.rsqrt(jnp.sum(q * q, axis=-1, keepdims=True) + NORM_EPS) * DN_DK ** -0.5
            k = k * lax.rsqrt(jnp.sum(k * k, axis=-1, keepdims=True) + NORM_EPS)
            beta = beta_all[:, hd:hd + 1]
            g_col = g_all[:, DN_HEADS + hd:DN_HEADS + hd + 1]
            g_row = g_all_t[DN_HEADS + hd:DN_HEADS + hd + 1, :]
            g_last = g_last_all[:, DN_HEADS + hd:DN_HEADS + hd + 1]
            decay = jnp.where(incl, jnp.exp(jnp.where(incl, g_col - g_row, 0.0)), 0.0)
            kb = k * beta
            kk = _dot_nt(kb, k)
            m = jnp.where(strict, kk * decay, 0.0)
            qk = jnp.where(incl, _dot_nt(q, k) * decay, 0.0)
            inv = eye - m
            pw = m
            for _ in range(5):
                pw = _dot(pw, pw)
                inv = inv + _dot(inv, pw)
            eg = jnp.exp(g_col)
            rhs = jnp.concatenate([v * beta, kb * eg], axis=1)
            uw = _dot(inv, rhs)
            pre.append(dict(u=uw[:, :DN_DV], w=uw[:, DN_DV:], qd=q * eg,
                            kd=k * jnp.exp(g_last - g_col), qk=qk, gl=jnp.exp(g_last)))
    outs = [[] for _ in pre]
    for c in range(r // CHUNK):
        rc = slice(c * CHUNK, (c + 1) * CHUNK)
        for idx, p in enumerate(pre):
            st = state_ref[idx]
            v_new = p["u"][rc] - _dot(p["w"][rc], st)
            outs[idx].append(_dot(p["qd"][rc], st) + _dot(p["qk"][rc, rc], v_new))
            state_ref[idx] = st * p["gl"][c * CHUNK:c * CHUNK + 1, :] + _dot_tn(p["kd"][rc], v_new)
    for idx in range(len(pre)):
        b, hd = divmod(idx, DN_HEADS)
        o = jnp.concatenate(outs[idx], axis=0)
        gate = x_ref[b, :, gate_off + hd * DN_DV:gate_off + (hd + 1) * DN_DV]
        o_ref[b, :, hd * DN_DV:(hd + 1) * DN_DV] = (
            _rms_norm(o, onorm_ref[...]) * _silu(gate)).astype(BF16)


def _dn_call(x3, conv_w, alog128, dtb128, onorm):
    batch, s, n = x3.shape
    r = SCAN_ROWS
    nqkv = DN_HEADS * (2 * DN_DK + DN_DV)
    width = DN_HEADS * DN_DV
    return pl.pallas_call(
        functools.partial(_dn_kernel, batch=batch),
        out_shape=jax.ShapeDtypeStruct((batch, s, width), BF16),
        grid=(s // r,),
        in_specs=[pl.BlockSpec((batch, r, n), lambda i: (0, i, 0)), _full((DN_CONV, nqkv)),
                  _full((1, LANES)), _full((1, LANES)), _full((1, DN_DV))],
        out_specs=pl.BlockSpec((batch, r, width), lambda i: (0, i, 0)),
        scratch_shapes=[pltpu.VMEM((batch, 8, nqkv), F32),
                        pltpu.VMEM((batch * DN_HEADS, DN_DK, DN_DV), F32)],
        compiler_params=_params(("arbitrary",)),
        name="deltanet",
    )(x3, conv_w, alog128, dtb128, onorm)


HG_LEVELS = (32, 16, 8, 4, 2, 1)


def _hg_kernel(x_ref, lbp_ref, onorm_ref, sel_ref, o_ref, state_ref, *, batch, layer):
    r = SCAN_ROWS
    width = HG_HEADS * HG_DK

    @pl.when(pl.program_id(0) == 0)
    def _():
        state_ref[...] = jnp.zeros(state_ref.shape, F32)

    lbp = lbp_ref[...]
    e = jnp.exp(lbp - jnp.max(lbp, axis=0, keepdims=True))
    soft = e / jnp.sum(e, axis=0, keepdims=True)
    lb_all = jnp.zeros((1, width), F32)
    for i in range(1, layer + 1):
        lb_all = lb_all + soft[i:i + 1, :]

    i_idx = lax.broadcasted_iota(jnp.int32, (r, r), 0)
    j_idx = lax.broadcasted_iota(jnp.int32, (r, r), 1)
    level_masks = [((i_idx // (2 * s)) == (j_idx // (2 * s))) & ((i_idx % (2 * s)) >= s)
                   & ((j_idx % (2 * s)) < s) for s in HG_LEVELS]
    diag = i_idx == j_idx

    for b in range(batch):
        for hd in range(HG_HEADS):
            idx = b * HG_HEADS + hd
            hs = slice(hd * HG_DK, (hd + 1) * HG_DK)
            lb = lb_all[:, hs]
            q = _silu(x_ref[b, :, hd * HG_DK:(hd + 1) * HG_DK])
            z = x_ref[b, :, width + hd * HG_DK:width + (hd + 1) * HG_DK]
            v = x_ref[b, :, 2 * width + hd * HG_DV:2 * width + (hd + 1) * HG_DV]
            gate = x_ref[b, :, 3 * width + hd * HG_DV:3 * width + (hd + 1) * HG_DV]
            f = lb + (1.0 - lb) * _sigmoid(z)
            k = (1.0 - lb) * _sigmoid(-z)
            cum = _chunk_cumsum(jnp.log(jnp.maximum(f, MIN_FORGET)))
            cum_last = _chunk_last(cum)
            qd = q * jnp.exp(cum)
            kd = k * jnp.exp(cum_last - cum)
            dl = jnp.exp(cum_last)
            att = jnp.where(diag, _dot_nt(q, k), 0.0)
            cum_b = cum.astype(BF16)
            for lv, mask in enumerate(level_masks):
                ref = jnp.dot(sel_ref[lv], cum_b, preferred_element_type=F32)
                qs = q * jnp.exp(jnp.minimum(cum - ref, 60.0))
                ks = k * jnp.exp(jnp.minimum(ref - cum, 60.0))
                att = att + jnp.where(mask, _dot_nt(qs, ks), 0.0)
            o_intra = _dot(att, v)
            outs = []
            st = state_ref[idx]
            for c in range(r // CHUNK):
                rc = slice(c * CHUNK, (c + 1) * CHUNK)
                outs.append(_dot_nt(qd[rc], st) + o_intra[rc])
                st = st * dl[c * CHUNK:c * CHUNK + 1, :] + _dot_tn(v[rc], kd[rc])
            state_ref[idx] = st
            o = jnp.concatenate(outs, axis=0)
            o_ref[b, :, hs] = (_rms_norm(o, onorm_ref[...]) * _silu(gate)).astype(BF16)


def _hg_sel():
    r = SCAN_ROWS
    i = np.arange(r)
    sel = np.zeros((len(HG_LEVELS), r, r), np.float32)
    for lv, s in enumerate(HG_LEVELS):
        sel[lv, i, (i // (2 * s)) * (2 * s) + s] = 1.0
    return jnp.asarray(sel, BF16)


def _hg_call(x3, lbp, onorm, layer):
    batch, s, n = x3.shape
    r = SCAN_ROWS
    width = HG_HEADS * HG_DV
    return pl.pallas_call(
        functools.partial(_hg_kernel, batch=batch, layer=layer),
        out_shape=jax.ShapeDtypeStruct((batch, s, width), BF16),
        grid=(s // r,),
        in_specs=[pl.BlockSpec((batch, r, n), lambda i: (0, i, 0)), _full(lbp.shape),
                  _full((1, HG_DV)), _full((len(HG_LEVELS), r, r))],
        out_specs=pl.BlockSpec((batch, r, width), lambda i: (0, i, 0)),
        scratch_shapes=[pltpu.VMEM((batch * HG_HEADS, HG_DV, HG_DK), F32)],
        compiler_params=_params(("arbitrary",)),
        name="hgrn2",
    )(x3, lbp, onorm, _hg_sel())


def _merge_kernel(h_ref, oa_ref, ob_ref, oc_ref, wgate_ref, wa_ref, wb_ref, wc_ref, wout_ref,
                  g_ref, b_ref, rhi_ref, rlo_ref, rb_ref, h1_ref, cmb_ref):
    h = h_ref[...]
    gates = _sigmoid(_dot(h, wgate_ref[...]))
    d = D_MODEL
    mixed = (gates[:, :d] * _dot(oa_ref[...], wa_ref[...])
             + gates[:, d:2 * d] * _dot(ob_ref[...], wb_ref[...])
             + gates[:, 2 * d:] * _dot(oc_ref[...], wc_ref[...]))
    h1 = _layer_norm(DEEPNORM_ALPHA * h + _dot(mixed, wout_ref[...]), g_ref[...], b_ref[...])
    h1_ref[...] = h1
    hi = h1.astype(BF16)
    lo = (h1 - hi.astype(F32)).astype(BF16)
    logits = (jnp.dot(hi, rhi_ref[...], preferred_element_type=F32)
              + jnp.dot(hi, rlo_ref[...], preferred_element_type=F32)
              + jnp.dot(lo, rhi_ref[...], preferred_element_type=F32)) + rb_ref[...]
    lane = lax.broadcasted_iota(jnp.int32, logits.shape, 1)
    neg = jnp.float32(-jnp.inf)
    big = jnp.int32(1 << 20)
    is_g = lane < N_GROUPS
    gl = jnp.where(is_g, logits, neg)
    gmax = jnp.max(gl, axis=-1, keepdims=True)
    gidx = jnp.min(jnp.where(gl == gmax, lane, big), axis=-1, keepdims=True)
    p_group = 1.0 / jnp.sum(jnp.where(is_g, jnp.exp(gl - gmax), 0.0), axis=-1, keepdims=True)
    lo_lane = N_GROUPS + gidx * EXPERTS_PER_GROUP
    in_g = (lane >= lo_lane) & (lane < lo_lane + EXPERTS_PER_GROUP)
    el = jnp.where(in_g, logits, neg)
    v1 = jnp.max(el, axis=-1, keepdims=True)
    i1 = jnp.min(jnp.where(el == v1, lane, big), axis=-1, keepdims=True)
    el2 = jnp.where(lane == i1, neg, el)
    v2 = jnp.max(el2, axis=-1, keepdims=True)
    i2 = jnp.min(jnp.where(el2 == v2, lane, big), axis=-1, keepdims=True)
    e21 = jnp.exp(v2 - v1)
    w1 = p_group / (1.0 + e21)
    w2 = p_group * e21 / (1.0 + e21)
    cmb_ref[...] = jnp.where(lane == i1, w1, 0.0) + jnp.where(lane == i2, w2, 0.0)


def _merge_call(h, oa, ob, oc, w, tm=512):
    t, d = h.shape
    row = lambda n: pl.BlockSpec((tm, n), lambda i: (i, 0))
    wa, wb, wc = w["wbr_a"], w["wbr_b"], w["wbr_c"]
    return pl.pallas_call(
        _merge_kernel,
        out_shape=(jax.ShapeDtypeStruct((t, d), F32), jax.ShapeDtypeStruct((t, LANES), F32)),
        grid=(t // tm,),
        in_specs=[row(d), row(oa.shape[1]), row(ob.shape[1]), row(oc.shape[1]),
                  _full((d, 3 * d)), _full(wa.shape), _full(wb.shape), _full(wc.shape),
                  _full((d, d)), _full((1, d)), _full((1, d)),
                  _full((d, LANES)), _full((d, LANES)), _full((1, LANES))],
        out_specs=(row(d), row(LANES)),
        compiler_params=_params(("parallel",)),
        name="merge_ln_router",
    )(h, oa, ob, oc, w["wgate"], wa, wb, wc, w["wout"], w["ln1_g"], w["ln1_b"],
      w["r_hi"], w["r_lo"], w["r_b"])


def _moe_kernel(h_ref, cmb_ref, wg_ref, wu_ref, wd_ref, g_ref, b_ref, o_ref, acc_ref):
    e = pl.program_id(1)

    @pl.when(e == 0)
    def _():
        acc_ref[...] = jnp.zeros(acc_ref.shape, F32)

    hb = h_ref[...].astype(BF16)
    lane = lax.broadcasted_iota(jnp.int32, cmb_ref.shape, 1)
    w_col = jnp.sum(jnp.where(lane == N_GROUPS + e, cmb_ref[...], 0.0), axis=-1, keepdims=True)
    act = _silu(_dot(hb, wg_ref[0])) * _dot(hb, wu_ref[0]) * w_col
    acc_ref[...] += _dot(act, wd_ref[0])

    @pl.when(e == N_EXPERTS - 1)
    def _():
        o_ref[...] = _layer_norm(DEEPNORM_ALPHA * h_ref[...] + acc_ref[...], g_ref[...], b_ref[...])


def _moe_call(h1, cmb, w, tm=1024):
    t, d = h1.shape
    return pl.pallas_call(
        _moe_kernel,
        out_shape=jax.ShapeDtypeStruct((t, d), F32),
        grid=(t // tm, N_EXPERTS),
        in_specs=[pl.BlockSpec((tm, d), lambda i, e: (i, 0)),
                  pl.BlockSpec((tm, LANES), lambda i, e: (i, 0)),
                  pl.BlockSpec((1, d, EXPERT_FF), lambda i, e: (e, 0, 0)),
                  pl.BlockSpec((1, d, EXPERT_FF), lambda i, e: (e, 0, 0)),
                  pl.BlockSpec((1, EXPERT_FF, d), lambda i, e: (e, 0, 0)),
                  pl.BlockSpec((1, d), lambda i, e: (0, 0)),
                  pl.BlockSpec((1, d), lambda i, e: (0, 0))],
        out_specs=pl.BlockSpec((tm, d), lambda i, e: (i, 0)),
        scratch_shapes=[pltpu.VMEM((tm, d), F32)],
        compiler_params=_params(("parallel", "arbitrary")),
        name="moe_experts",
    )(h1, cmb, w["e_gate"], w["e_up"], w["e_down"], w["ln2_g"], w["ln2_b"])


def _pad_cols(w, lo, total):
    return jnp.zeros((w.shape[0], total), w.dtype).at[:, lo:lo + w.shape[1]].set(w)


def _layer_weights(l, p):
    d = D_MODEL
    w_in = p["w_in"][l]
    sizes = (MLA_Q_RANK, MLA_KV_RANK, MLA_ROPE, DN_HEADS * (2 * DN_DK + DN_DV), DN_HEADS, DN_HEADS,
             DN_HEADS * DN_DV, HG_HEADS * HG_DK, HG_HEADS * HG_DK, HG_HEADS * HG_DV,
             HG_HEADS * HG_DV, 3 * d)
    offs = np.concatenate([[0], np.cumsum(sizes)])
    col = lambda i: w_in[:, offs[i]:offs[i + 1]]
    qk_w = MLA_NOPE + MLA_ROPE
    uq = p["mla_w_uq"][l].reshape(MLA_Q_RANK, MLA_HEADS, qk_w)
    uq = jnp.pad(uq, ((0, 0), (0, 0), (0, HEAD_PAD - qk_w))).reshape(MLA_Q_RANK, MLA_HEADS * HEAD_PAD)
    ukv = p["mla_w_ukv"][l].reshape(MLA_KV_RANK, MLA_HEADS, MLA_NOPE + MLA_V)
    uk = jnp.pad(ukv[:, :, :MLA_NOPE], ((0, 0), (0, 0), (0, HEAD_PAD - MLA_NOPE)))
    uk = uk.reshape(MLA_KV_RANK, MLA_HEADS * HEAD_PAD)
    uv = ukv[:, :, MLA_NOPE:].reshape(MLA_KV_RANK, MLA_HEADS * MLA_V)
    ba = jnp.concatenate([col(4), col(5)], axis=1)
    w_dn = jnp.concatenate([col(3), col(6), _pad_cols(ba, 0, LANES)], axis=1)
    w_hg = jnp.concatenate([col(7), col(8), col(9), col(10)], axis=1)
    router = jnp.concatenate([p["router_group_w"][l], p["router_expert_w"][l]], axis=1)
    router = _pad_cols(router, 0, LANES)
    r_hi = router.astype(BF16)
    r_b = jnp.concatenate([p["router_group_b"][l], p["router_expert_b"][l]])
    bf = lambda a: a.astype(BF16)
    return dict(
        wcq=bf(col(0)), wckv=bf(col(1)), wkr=bf(_pad_cols(col(2), ROPE_LO, LANES)),
        qn=p["mla_q_norm"][l].reshape(1, -1), kvn=p["mla_kv_norm"][l].reshape(1, -1),
        wuq=bf(uq), wuk=bf(uk), wuv=bf(uv),
        w_dn=bf(w_dn), w_hg=bf(w_hg), wgate=bf(col(11)),
        dn_conv=p["dn_conv"][l],
        alog128=_pad_cols(p["dn_a_log"][l].reshape(1, -1), DN_HEADS, LANES),
        dtb128=_pad_cols(p["dn_dt_bias"][l].reshape(1, -1), DN_HEADS, LANES),
        dn_onorm=p["dn_o_norm"][l].reshape(1, -1), hg_onorm=p["hg_o_norm"][l].reshape(1, -1),
        wbr_a=bf(p["w_br_a"][l]), wbr_b=bf(p["w_br_b"][l]), wbr_c=bf(p["w_br_c"][l]),
        wout=bf(p["w_out"][l]),
        ln1_g=p["ln1_g"][l].reshape(1, d), ln1_b=p["ln1_b"][l].reshape(1, d),
        r_hi=r_hi, r_lo=bf(router - r_hi.astype(F32)), r_b=_pad_cols(r_b.reshape(1, -1), 0, LANES),
        e_gate=bf(p["exp_w_gate"][l]), e_up=bf(p["exp_w_up"][l]), e_down=bf(p["exp_w_down"][l]),
        ln2_g=p["ln2_g"][l].reshape(1, d), ln2_b=p["ln2_b"][l].reshape(1, d),
    )


def _forward(p):
    x = p["x"]
    batch, s, d = x.shape
    t = batch * s
    half = MLA_ROPE // 2
    inv_freq = ROPE_BASE ** (-jnp.arange(half, dtype=F32) / half)
    inv128 = _pad_cols(jnp.concatenate([inv_freq, inv_freq]).reshape(1, -1), ROPE_LO, LANES)
    tables = _rope_call(p["positions"].reshape(t, 1), inv128)
    h = _ln_call(x.reshape(t, d), p["ln_in_g"], p["ln_in_b"])
    for l in range(DEPTH):
        w = _layer_weights(l, p)
        q, k, v = _mla_proj_call(h, w, tables)
        oa = _flash_call(q, k, v, batch)
        x_dn = _proj_call(h, w["w_dn"], "proj_deltanet")
        ob = _dn_call(x_dn.reshape(batch, s, -1), w["dn_conv"], w["alog128"], w["dtb128"],
                      w["dn_onorm"])
        x_hg = _proj_call(h, w["w_hg"], "proj_hgrn2")
        oc = _hg_call(x_hg.reshape(batch, s, -1), p["hg_lower_bounds"], w["hg_onorm"], l)
        h1, cmb = _merge_call(h, oa, ob.reshape(t, -1), oc.reshape(t, -1), w)
        h = _moe_call(h1, cmb, w)
    return h.reshape(batch, s, d)


def kernel(x, positions, ln_in_g, ln_in_b, hg_lower_bounds, w_in, mla_q_norm, mla_w_uq, mla_kv_norm, mla_w_ukv, dn_conv, dn_a_log, dn_dt_bias, dn_o_norm, hg_o_norm, w_br_a, w_br_b, w_br_c, w_out, ln1_g, ln1_b, router_group_w, router_group_b, router_expert_w, router_expert_b, exp_w_gate, exp_w_up, exp_w_down, ln2_g, ln2_b):
    return _forward(dict(
        x=x, positions=positions, ln_in_g=ln_in_g, ln_in_b=ln_in_b,
        hg_lower_bounds=hg_lower_bounds, w_in=w_in, mla_q_norm=mla_q_norm, mla_w_uq=mla_w_uq,
        mla_kv_norm=mla_kv_norm, mla_w_ukv=mla_w_ukv, dn_conv=dn_conv, dn_a_log=dn_a_log,
        dn_dt_bias=dn_dt_bias, dn_o_norm=dn_o_norm, hg_o_norm=hg_o_norm, w_br_a=w_br_a,
        w_br_b=w_br_b, w_br_c=w_br_c, w_out=w_out, ln1_g=ln1_g, ln1_b=ln1_b,
        router_group_w=router_group_w, router_group_b=router_group_b,
        router_expert_w=router_expert_w, router_expert_b=router_expert_b,
        exp_w_gate=exp_w_gate, exp_w_up=exp_w_up, exp_w_down=exp_w_down, ln2_g=ln2_g, ln2_b=ln2_b))
```

```python
import functools
import math

import jax
import jax.numpy as jnp
import numpy as np
from jax import lax
from jax.experimental import pallas as pl
from jax.experimental.pallas import tpu as pltpu

D_MODEL = 1024
DEPTH = 2
MLA_HEADS = 8
MLA_Q_RANK = 256
MLA_KV_RANK = 128
MLA_NOPE = 64
MLA_ROPE = 32
MLA_V = 64
ROPE_BASE = 10000.0
MASK_VALUE = -1e30
DN_HEADS = 4
DN_DK = 128
DN_DV = 128
DN_CONV = 4
HG_HEADS = 4
HG_DK = 128
HG_DV = 128
CHUNK = 64
MIN_FORGET = 1e-30
N_GROUPS = 4
EXPERTS_PER_GROUP = 8
N_EXPERTS = N_GROUPS * EXPERTS_PER_GROUP
EXPERT_FF = 256
DEEPNORM_ALPHA = (2 * DEPTH) ** 0.25
NORM_EPS = 1e-6

LANES = 128
HEAD_PAD = 128
ROPE_LO = MLA_NOPE
ROPE_MID = MLA_NOPE + MLA_ROPE // 2
ROPE_HI = MLA_NOPE + MLA_ROPE
SCAN_ROWS = 256
ATTN_TILE = 512
EXT_COLS = D_MODEL + LANES
MOE_TILE = 256
VMEM_LIMIT = 56 * 1024 * 1024

BF16 = jnp.bfloat16
F32 = jnp.float32


def _dot(a, b):
    return jnp.dot(a.astype(BF16), b.astype(BF16), preferred_element_type=F32)


def _dot_nt(a, b):
    return lax.dot_general(a.astype(BF16), b.astype(BF16), (((1,), (1,)), ((), ())),
                           preferred_element_type=F32)


def _dot_tn(a, b):
    return lax.dot_general(a.astype(BF16), b.astype(BF16), (((0,), (0,)), ((), ())),
                           preferred_element_type=F32)


def _sigmoid(x):
    return 1.0 / (1.0 + jnp.exp(-x))


def _silu(x):
    return x * _sigmoid(x)


def _layer_norm(x, g, b):
    mu = jnp.mean(x, axis=-1, keepdims=True)
    xc = x - mu
    var = jnp.mean(xc * xc, axis=-1, keepdims=True)
    return xc * lax.rsqrt(var + NORM_EPS) * g + b


def _rms_norm(x, g):
    return x * lax.rsqrt(jnp.mean(x * x, axis=-1, keepdims=True) + NORM_EPS) * g


def _params(sem):
    return pltpu.CompilerParams(dimension_semantics=sem, vmem_limit_bytes=VMEM_LIMIT)


def _full(shape):
    n = len(shape)
    return pl.BlockSpec(shape, lambda *_: (0,) * n)


def _ln_kernel(x_ref, g_ref, b_ref, o_ref):
    o_ref[...] = _layer_norm(x_ref[...], g_ref[...], b_ref[...])


def _ln_call(x, g, b, tm=1024):
    t, d = x.shape
    return pl.pallas_call(
        _ln_kernel,
        out_shape=jax.ShapeDtypeStruct((t, d), F32),
        grid=(t // tm,),
        in_specs=[pl.BlockSpec((tm, d), lambda i: (i, 0)), _full((1, d)), _full((1, d))],
        out_specs=pl.BlockSpec((tm, d), lambda i: (i, 0)),
        compiler_params=_params(("parallel",)),
        name="ln_in",
    )(x, g.reshape(1, d), b.reshape(1, d))


def _rope_kernel(pos_ref, inv_ref, c_ref, sa_ref, sb_ref):
    ang = pos_ref[...].astype(F32) * inv_ref[...]
    lane = lax.broadcasted_iota(jnp.int32, ang.shape, 1)
    sin = jnp.sin(ang)
    c_ref[...] = jnp.cos(ang)
    sa_ref[...] = jnp.where((lane >= ROPE_MID) & (lane < ROPE_HI), sin, 0.0)
    sb_ref[...] = jnp.where((lane >= ROPE_LO) & (lane < ROPE_MID), -sin, 0.0)


def _rope_call(pos_col, inv128, tm=1024):
    t = pos_col.shape[0]
    spec = pl.BlockSpec((tm, LANES), lambda i: (i, 0))
    shp = jax.ShapeDtypeStruct((t, LANES), F32)
    return pl.pallas_call(
        _rope_kernel,
        out_shape=(shp, shp, shp),
        grid=(t // tm,),
        in_specs=[pl.BlockSpec((tm, 1), lambda i: (i, 0)), _full((1, LANES))],
        out_specs=(spec, spec, spec),
        compiler_params=_params(("parallel",)),
        name="rope_tables",
    )(pos_col, inv128)


def _rope_apply(x, c, sa, sb):
    return x * c + pltpu.roll(x, 16, 1) * sa + pltpu.roll(x, LANES - 16, 1) * sb


def _mla_proj_kernel(h_ref, wcq_ref, wckv_ref, wkr_ref, qn_ref, kvn_ref, wuq_ref, wuk_ref,
                     wuv_ref, c_ref, sa_ref, sb_ref, qt_ref, k_ref, vt_ref):
    hb = h_ref[...].astype(BF16)
    cq = _rms_norm(_dot(hb, wcq_ref[...]), qn_ref[...])
    ckv = _rms_norm(_dot(hb, wckv_ref[...]), kvn_ref[...])
    kr = _dot(hb, wkr_ref[...])
    q = _dot(cq, wuq_ref[...])
    k = _dot(ckv, wuk_ref[...])
    v = _dot(ckv, wuv_ref[...])
    for pr in range(MLA_HEADS // 2):
        sl = slice(pr * 2 * MLA_V, (pr + 1) * 2 * MLA_V)
        vt_ref[0, sl, :] = v[:, sl].T.astype(BF16)
    c, sa, sb = c_ref[...], sa_ref[...], sb_ref[...]
    scale = (MLA_NOPE + MLA_ROPE) ** -0.5 * math.log2(math.e)
    kr = _rope_apply(kr, c, sa, sb)
    for hd in range(MLA_HEADS):
        sl = slice(hd * HEAD_PAD, (hd + 1) * HEAD_PAD)
        qt_ref[0, sl, :] = (_rope_apply(q[:, sl], c, sa, sb) * scale).T.astype(BF16)
        k_ref[:, sl] = (k[:, sl] + kr).astype(BF16)


def _mla_proj_call(h, w, tables, tm=ATTN_TILE):
    t, d = h.shape
    hq = MLA_HEADS * HEAD_PAD
    hv = MLA_HEADS * MLA_V
    row = lambda n: pl.BlockSpec((tm, n), lambda i: (i, 0))
    tile = lambda n: pl.BlockSpec((1, n, tm), lambda i: (i, 0, 0))
    return pl.pallas_call(
        _mla_proj_kernel,
        out_shape=(jax.ShapeDtypeStruct((t // tm, hq, tm), BF16),
                   jax.ShapeDtypeStruct((t, hq), BF16),
                   jax.ShapeDtypeStruct((t // tm, hv, tm), BF16)),
        grid=(t // tm,),
        in_specs=[row(d), _full((d, MLA_Q_RANK)), _full((d, MLA_KV_RANK)), _full((d, LANES)),
                  _full((1, MLA_Q_RANK)), _full((1, MLA_KV_RANK)), _full((MLA_Q_RANK, hq)),
                  _full((MLA_KV_RANK, hq)), _full((MLA_KV_RANK, hv)),
                  row(LANES), row(LANES), row(LANES)],
        out_specs=(tile(hq), row(hq), tile(hv)),
        compiler_params=_params(("parallel",)),
        name="mla_proj",
    )(h, w["wcq"], w["wckv"], w["wkr"], w["qn"], w["kvn"], w["wuq"], w["wuk"], w["wuv"], *tables)


def _flash_kernel(qt_ref, k_ref, vt_ref, o_ref, m0, l0, a0, m1, l1, a1, sa_ref, sb_ref,
                  *, tq):
    qi = pl.program_id(2)
    stats = ((m0, l0, a0), (m1, l1, a1))
    for m_ref, l_ref, acc_ref in stats:
        m_ref[...] = jnp.full(m_ref.shape, MASK_VALUE, F32)
        l_ref[...] = jnp.zeros(l_ref.shape, F32)
        acc_ref[...] = jnp.zeros(acc_ref.shape, F32)

    def score(j, s_ref):
        rows = pl.ds(pl.multiple_of(j * tq, tq), tq)
        for hh in range(2):
            sl = slice(hh * HEAD_PAD, (hh + 1) * HEAD_PAD)
            s_ref[hh] = jnp.dot(k_ref[0, rows, sl], qt_ref[0, 0, sl, :],
                                preferred_element_type=F32)

    def consume(j, s_ref, masked):
        vt = vt_ref[0, j, 0]
        for hh, (m_ref, l_ref, acc_ref) in enumerate(stats):
            st = s_ref[hh]
            if masked:
                key = lax.broadcasted_iota(jnp.int32, st.shape, 0)
                qry = lax.broadcasted_iota(jnp.int32, st.shape, 1)
                st = jnp.where(key <= qry, st, MASK_VALUE)
            m_old = m_ref[...]
            m_new = jnp.maximum(m_old, jnp.max(st, axis=0, keepdims=True))
            alpha = jnp.exp2(m_old - m_new)
            p = jnp.exp2(st - m_new)
            l_ref[...] = alpha * l_ref[...] + jnp.sum(p, axis=0, keepdims=True)
            acc_ref[...] = alpha * acc_ref[...] + jnp.dot(vt, p.astype(BF16),
                                                          preferred_element_type=F32)
            m_ref[...] = m_new

    score(0, sa_ref)

    def body(i, carry):
        j = 2 * i
        score(j + 1, sb_ref)
        consume(j, sa_ref, False)
        score(j + 2, sa_ref)
        consume(j + 1, sb_ref, False)
        return carry

    lax.fori_loop(0, qi // 2, body, 0)

    @pl.when(qi % 2 == 0)
    def _():
        consume(qi, sa_ref, True)

    @pl.when(qi % 2 == 1)
    def _():
        score(qi, sb_ref)
        consume(qi - 1, sa_ref, False)
        consume(qi, sb_ref, True)
    row = lax.broadcasted_iota(jnp.int32, a0.shape, 0)
    ot = jnp.where(row < MLA_V, a0[...] * (1.0 / l0[...]), a1[...] * (1.0 / l1[...]))
    o_ref[0] = ot.T.astype(BF16)


def _flash_call(qt, k, vt, batch, tq=ATTN_TILE):
    nt, hq, _ = qt.shape
    t = nt * tq
    s = t // batch
    ns = s // tq
    pairs = MLA_HEADS // 2
    hv = MLA_HEADS * MLA_V
    qt5 = qt.reshape(batch, ns, hq, tq)
    k3 = k.reshape(batch, s, hq)
    vt5 = vt.reshape(batch, ns, pairs, 2 * MLA_V, tq)
    stat = lambda: pltpu.VMEM((1, tq), F32)
    acc = lambda: pltpu.VMEM((2 * MLA_V, tq), F32)
    out = pl.pallas_call(
        functools.partial(_flash_kernel, tq=tq),
        out_shape=jax.ShapeDtypeStruct((batch, s, hv), BF16),
        grid=(batch, pairs, ns),
        in_specs=[pl.BlockSpec((1, 1, 2 * HEAD_PAD, tq), lambda b, p, i: (b, i, p, 0)),
                  pl.BlockSpec((1, s, 2 * HEAD_PAD), lambda b, p, i: (b, 0, p)),
                  pl.BlockSpec((1, ns, 1, 2 * MLA_V, tq), lambda b, p, i: (b, 0, p, 0, 0))],
        out_specs=pl.BlockSpec((1, tq, 2 * MLA_V), lambda b, p, i: (b, i, p)),
        scratch_shapes=[stat(), stat(), acc(), stat(), stat(), acc(),
                        pltpu.VMEM((2, tq, tq), F32), pltpu.VMEM((2, tq, tq), F32)],
        compiler_params=_params(("parallel", "parallel", "arbitrary")),
        name="flash_attn",
    )(qt5, k3, vt5)
    return out.reshape(t, hv)


def _proj_kernel(h_ref, w_ref, o_ref):
    o_ref[...] = _dot(h_ref[...], w_ref[...])


def _proj_call(h, w, name, tm=512):
    t, d = h.shape
    n = w.shape[1]
    return pl.pallas_call(
        _proj_kernel,
        out_shape=jax.ShapeDtypeStruct((t, n), F32),
        grid=(t // tm,),
        in_specs=[pl.BlockSpec((tm, d), lambda i: (i, 0)), _full((d, n))],
        out_specs=pl.BlockSpec((tm, n), lambda i: (i, 0)),
        compiler_params=_params(("parallel",)),
        name=name,
    )(h, w)


def _chunk_cumsum(x):
    rows = lax.broadcasted_iota(jnp.int32, x.shape, 0) % CHUNK
    shift = 1
    while shift < CHUNK:
        x = x + jnp.where(rows >= shift, pltpu.roll(x, shift, 0), 0.0)
        shift *= 2
    return x


def _chunk_last(x):
    r, n = x.shape
    x3 = x.reshape(r // CHUNK, CHUNK, n)
    return jnp.broadcast_to(x3[:, CHUNK - 1:CHUNK, :], x3.shape).reshape(r, n)


def _chunk_masks(r):
    i = lax.broadcasted_iota(jnp.int32, (r, r), 0)
    j = lax.broadcasted_iota(jnp.int32, (r, r), 1)
    same = (i // CHUNK) == (j // CHUNK)
    return same & (j <= i), same & (j < i), i == j


def _dn_kernel(x_ref, conv_ref, alog_ref, dtb_ref, onorm_ref, o_ref, halo_ref, state_ref,
               *, batch):
    r = SCAN_ROWS
    nqkv = DN_HEADS * (2 * DN_DK + DN_DV)
    gate_off = nqkv
    ba_off = nqkv + DN_HEADS * DN_DV

    @pl.when(pl.program_id(0) == 0)
    def _():
        halo_ref[...] = jnp.zeros(halo_ref.shape, F32)
        state_ref[...] = jnp.zeros(state_ref.shape, F32)

    incl, strict, diag = _chunk_masks(r)
    eye = jnp.where(diag, 1.0, 0.0)
    cw = conv_ref[...]
    pre = []
    for b in range(batch):
        xq = x_ref[b, :, :nqkv]
        xe = jnp.concatenate([halo_ref[b], xq], axis=0)
        halo_ref[b] = xq[r - 8:, :]
        y = xe[8:, :] * cw[DN_CONV - 1:DN_CONV, :]
        for i in range(DN_CONV - 1):
            y = y + pltpu.roll(xe, DN_CONV - 1 - i, 0)[8:, :] * cw[i:i + 1, :]
        y = _silu(y)
        ba = x_ref[b, :, ba_off:ba_off + LANES]
        beta_all = _sigmoid(ba)
        av = ba + dtb_ref[...]
        softplus = jnp.maximum(av, 0.0) + jnp.log(1.0 + jnp.exp(-jnp.abs(av)))
        g_all = _chunk_cumsum(-jnp.exp(alog_ref[...]) * softplus)
        g_all_t = g_all.T
        g_last_all = _chunk_last(g_all)
        for hd in range(DN_HEADS):
            q = y[:, hd * DN_DK:(hd + 1) * DN_DK]
            k = y[:, DN_HEADS * DN_DK + hd * DN_DK:DN_HEADS * DN_DK + (hd + 1) * DN_DK]
            v = y[:, 2 * DN_HEADS * DN_DK + hd * DN_DV:2 * DN_HEADS * DN_DK + (hd + 1) * DN_DV]
            q = q * lax.rsqrt(jnp.sum(q * q, axis=-1, keepdims=True) + NORM_EPS) * DN_DK ** -0.5
            k = k * lax.rsqrt(jnp.sum(k * k, axis=-1, keepdims=True) + NORM_EPS)
            beta = beta_all[:, hd:hd + 1]
            g_col = g_all[:, DN_HEADS + hd:DN_HEADS + hd + 1]
            g_row = g_all_t[DN_HEADS + hd:DN_HEADS + hd + 1, :]
            g_last = g_last_all[:, DN_HEADS + hd:DN_HEADS + hd + 1]
            decay = jnp.where(incl, jnp.exp(jnp.where(incl, g_col - g_row, 0.0)), 0.0)
            kb = k * beta
            kk = _dot_nt(kb, k)
            m = jnp.where(strict, kk * decay, 0.0)
            qk = jnp.where(incl, _dot_nt(q, k) * decay, 0.0)
            inv = eye - m
            pw = m
            for _ in range(5):
                pw = _dot(pw, pw)
                inv = inv + _dot(inv, pw)
            eg = jnp.exp(g_col)
            rhs = jnp.concatenate([v * beta, kb * eg], axis=1)
            uw = _dot(inv, rhs)
            pre.append(dict(u=uw[:, :DN_DV], w=uw[:, DN_DV:], qd=q * eg,
                            kd=k * jnp.exp(g_last - g_col), qk=qk, gl=jnp.exp(g_last)))
    outs = [[] for _ in pre]
    for c in range(r // CHUNK):
        rc = slice(c * CHUNK, (c + 1) * CHUNK)
        for idx, p in enumerate(pre):
            st = state_ref[idx]
            v_new = p["u"][rc] - _dot(p["w"][rc], st)
            outs[idx].append(_dot(p["qd"][rc], st) + _dot(p["qk"][rc, rc], v_new))
            state_ref[idx] = st * p["gl"][c * CHUNK:c * CHUNK + 1, :] + _dot_tn(p["kd"][rc], v_new)
    for idx in range(len(pre)):
        b, hd = divmod(idx, DN_HEADS)
        o = jnp.concatenate(outs[idx], axis=0)
        gate = x_ref[b, :, gate_off + hd * DN_DV:gate_off + (hd + 1) * DN_DV]
        o_ref[b, :, hd * DN_DV:(hd + 1) * DN_DV] = (
            _rms_norm(o, onorm_ref[...]) * _silu(gate)).astype(BF16)


def _dn_call(x3, conv_w, alog128, dtb128, onorm):
    batch, s, n = x3.shape
    r = SCAN_ROWS
    nqkv = DN_HEADS * (2 * DN_DK + DN_DV)
    width = DN_HEADS * DN_DV
    return pl.pallas_call(
        functools.partial(_dn_kernel, batch=batch),
        out_shape=jax.ShapeDtypeStruct((batch, s, width), BF16),
        grid=(s // r,),
        in_specs=[pl.BlockSpec((batch, r, n), lambda i: (0, i, 0)), _full((DN_CONV, nqkv)),
                  _full((1, LANES)), _full((1, LANES)), _full((1, DN_DV))],
        out_specs=pl.BlockSpec((batch, r, width), lambda i: (0, i, 0)),
        scratch_shapes=[pltpu.VMEM((batch, 8, nqkv), F32),
                        pltpu.VMEM((batch * DN_HEADS, DN_DK, DN_DV), F32)],
        compiler_params=_params(("arbitrary",)),
        name="deltanet",
    )(x3, conv_w, alog128, dtb128, onorm)


HG_LEVELS = (32, 16, 8, 4, 2, 1)


def _hg_kernel(x_ref, lbp_ref, onorm_ref, sel_ref, o_ref, state_ref, *, batch, layer):
    r = SCAN_ROWS
    width = HG_HEADS * HG_DK

    @pl.when(pl.program_id(0) == 0)
    def _():
        state_ref[...] = jnp.zeros(state_ref.shape, F32)

    lbp = lbp_ref[...]
    e = jnp.exp(lbp - jnp.max(lbp, axis=0, keepdims=True))
    soft = e / jnp.sum(e, axis=0, keepdims=True)
    lb_all = jnp.zeros((1, width), F32)
    for i in range(1, layer + 1):
        lb_all = lb_all + soft[i:i + 1, :]

    i_idx = lax.broadcasted_iota(jnp.int32, (r, r), 0)
    j_idx = lax.broadcasted_iota(jnp.int32, (r, r), 1)
    level_masks = [((i_idx // (2 * s)) == (j_idx // (2 * s))) & ((i_idx % (2 * s)) >= s)
                   & ((j_idx % (2 * s)) < s) for s in HG_LEVELS]
    diag = i_idx == j_idx

    for b in range(batch):
        for hd in range(HG_HEADS):
            idx = b * HG_HEADS + hd
            hs = slice(hd * HG_DK, (hd + 1) * HG_DK)
            lb = lb_all[:, hs]
            q = _silu(x_ref[b, :, hd * HG_DK:(hd + 1) * HG_DK])
            z = x_ref[b, :, width + hd * HG_DK:width + (hd + 1) * HG_DK]
            v = x_ref[b, :, 2 * width + hd * HG_DV:2 * width + (hd + 1) * HG_DV]
            gate = x_ref[b, :, 3 * width + hd * HG_DV:3 * width + (hd + 1) * HG_DV]
            f = lb + (1.0 - lb) * _sigmoid(z)
            k = (1.0 - lb) * _sigmoid(-z)
            cum = _chunk_cumsum(jnp.log(jnp.maximum(f, MIN_FORGET)))
            cum_last = _chunk_last(cum)
            qd = q * jnp.exp(cum)
            kd = k * jnp.exp(cum_last - cum)
            dl = jnp.exp(cum_last)
            att = jnp.where(diag, _dot_nt(q, k), 0.0)
            cum_b = cum.astype(BF16)
            for lv, mask in enumerate(level_masks):
                ref = jnp.dot(sel_ref[lv], cum_b, preferred_element_type=F32)
                qs = q * jnp.exp(jnp.minimum(cum - ref, 60.0))
                ks = k * jnp.exp(jnp.minimum(ref - cum, 60.0))
                att = att + jnp.where(mask, _dot_nt(qs, ks), 0.0)
            o_intra = _dot(att, v)
            outs = []
            st = state_ref[idx]
            for c in range(r // CHUNK):
                rc = slice(c * CHUNK, (c + 1) * CHUNK)
                outs.append(_dot_nt(qd[rc], st) + o_intra[rc])
                st = st * dl[c * CHUNK:c * CHUNK + 1, :] + _dot_tn(v[rc], kd[rc])
            state_ref[idx] = st
            o = jnp.concatenate(outs, axis=0)
            o_ref[b, :, hs] = (_rms_norm(o, onorm_ref[...]) * _silu(gate)).astype(BF16)


def _hg_sel():
    r = SCAN_ROWS
    i = np.arange(r)
    sel = np.zeros((len(HG_LEVELS), r, r), np.float32)
    for lv, s in enumerate(HG_LEVELS):
        sel[lv, i, (i // (2 * s)) * (2 * s) + s] = 1.0
    return jnp.asarray(sel, BF16)


def _hg_call(x3, lbp, onorm, layer):
    batch, s, n = x3.shape
    r = SCAN_ROWS
    width = HG_HEADS * HG_DV
    return pl.pallas_call(
        functools.partial(_hg_kernel, batch=batch, layer=layer),
        out_shape=jax.ShapeDtypeStruct((batch, s, width), BF16),
        grid=(s // r,),
        in_specs=[pl.BlockSpec((batch, r, n), lambda i: (0, i, 0)), _full(lbp.shape),
                  _full((1, HG_DV)), _full((len(HG_LEVELS), r, r))],
        out_specs=pl.BlockSpec((batch, r, width), lambda i: (0, i, 0)),
        scratch_shapes=[pltpu.VMEM((batch * HG_HEADS, HG_DV, HG_DK), F32)],
        compiler_params=_params(("arbitrary",)),
        name="hgrn2",
    )(x3, lbp, onorm, _hg_sel())


def _merge_kernel(h_ref, oa_ref, ob_ref, oc_ref, wgate_ref, wa_ref, wb_ref, wc_ref, wout_ref,
                  g_ref, b_ref, rhi_ref, rlo_ref, rb_ref, ext_ref, info_ref):
    h = h_ref[...]
    gates = _sigmoid(_dot(h, wgate_ref[...]))
    d = D_MODEL
    mixed = (gates[:, :d] * _dot(oa_ref[...], wa_ref[...])
             + gates[:, d:2 * d] * _dot(ob_ref[...], wb_ref[...])
             + gates[:, 2 * d:] * _dot(oc_ref[...], wc_ref[...]))
    h1 = _layer_norm(DEEPNORM_ALPHA * h + _dot(mixed, wout_ref[...]), g_ref[...], b_ref[...])
    hi = h1.astype(BF16)
    lo = (h1 - hi.astype(F32)).astype(BF16)
    logits = (jnp.dot(hi, rhi_ref[...], preferred_element_type=F32)
              + jnp.dot(hi, rlo_ref[...], preferred_element_type=F32)
              + jnp.dot(lo, rhi_ref[...], preferred_element_type=F32)) + rb_ref[...]
    lane = lax.broadcasted_iota(jnp.int32, logits.shape, 1)
    neg = jnp.float32(-jnp.inf)
    big = jnp.int32(1 << 20)
    is_g = lane < N_GROUPS
    gl = jnp.where(is_g, logits, neg)
    gmax = jnp.max(gl, axis=-1, keepdims=True)
    gidx = jnp.min(jnp.where(gl == gmax, lane, big), axis=-1, keepdims=True)
    p_group = 1.0 / jnp.sum(jnp.where(is_g, jnp.exp(gl - gmax), 0.0), axis=-1, keepdims=True)
    lo_lane = N_GROUPS + gidx * EXPERTS_PER_GROUP
    in_g = (lane >= lo_lane) & (lane < lo_lane + EXPERTS_PER_GROUP)
    el = jnp.where(in_g, logits, neg)
    v1 = jnp.max(el, axis=-1, keepdims=True)
    i1 = jnp.min(jnp.where(el == v1, lane, big), axis=-1, keepdims=True)
    el2 = jnp.where(lane == i1, neg, el)
    v2 = jnp.max(el2, axis=-1, keepdims=True)
    i2 = jnp.min(jnp.where(el2 == v2, lane, big), axis=-1, keepdims=True)
    e21 = jnp.exp(v2 - v1)
    w1 = p_group / (1.0 + e21)
    w2 = p_group * e21 / (1.0 + e21)
    info = (jnp.where(lane == i1 - lo_lane, w1, 0.0) + jnp.where(lane == i2 - lo_lane, w2, 0.0)
            + jnp.where(lane == EXPERTS_PER_GROUP, gidx.astype(F32), 0.0))
    info_ref[...] = info
    ext_ref[:, :D_MODEL] = h1
    ext_ref[:, D_MODEL:] = info


def _merge_call(h, oa, ob, oc, w, tm=512):
    t, d = h.shape
    row = lambda n: pl.BlockSpec((tm, n), lambda i: (i, 0))
    wa, wb, wc = w["wbr_a"], w["wbr_b"], w["wbr_c"]
    return pl.pallas_call(
        _merge_kernel,
        out_shape=(jax.ShapeDtypeStruct((t, EXT_COLS), F32),
                   jax.ShapeDtypeStruct((t, LANES), F32)),
        grid=(t // tm,),
        in_specs=[row(d), row(oa.shape[1]), row(ob.shape[1]), row(oc.shape[1]),
                  _full((d, 3 * d)), _full(wa.shape), _full(wb.shape), _full(wc.shape),
                  _full((d, d)), _full((1, d)), _full((1, d)),
                  _full((d, LANES)), _full((d, LANES)), _full((1, LANES))],
        out_specs=(row(EXT_COLS), row(LANES)),
        compiler_params=_params(("parallel",)),
        name="merge_ln_router",
    )(h, oa, ob, oc, w["wgate"], wa, wb, wc, w["wout"], w["ln1_g"], w["ln1_b"],
      w["r_hi"], w["r_lo"], w["r_b"])


def _rank_kernel(info_ref, route_ref, cnt_ref, base_ref):
    @pl.when(pl.program_id(0) == 0)
    def _():
        base_ref[...] = jnp.zeros(base_ref.shape, F32)

    info = info_ref[...]
    tb = info.shape[0]
    lane = lax.broadcasted_iota(jnp.int32, info.shape, 1)
    gid = info[:, EXPERTS_PER_GROUP:EXPERTS_PER_GROUP + 1]
    onehot = jnp.where((lane.astype(F32) == gid) & (lane < N_GROUPS), 1.0, 0.0)
    r = lax.broadcasted_iota(jnp.int32, (tb, tb), 0)
    c = lax.broadcasted_iota(jnp.int32, (tb, tb), 1)
    earlier = jnp.where(c < r, 1.0, 0.0)
    prefix = _dot(earlier, onehot)
    base = base_ref[...]
    rank = jnp.sum(onehot * (prefix + base), axis=1, keepdims=True)
    route_ref[...] = jnp.where(lane == 0, rank, jnp.where(lane == 1, gid, 0.0))
    base = base + jnp.sum(onehot, axis=0, keepdims=True)
    base_ref[...] = base
    cnt_ref[...] = base


def _rank_call(info, tb=512):
    t = info.shape[0]
    return pl.pallas_call(
        _rank_kernel,
        out_shape=(jax.ShapeDtypeStruct((t, LANES), F32), jax.ShapeDtypeStruct((1, LANES), F32)),
        grid=(t // tb,),
        in_specs=[pl.BlockSpec((tb, LANES), lambda i: (i, 0))],
        out_specs=(pl.BlockSpec((tb, LANES), lambda i: (i, 0)),
                   pl.BlockSpec((1, LANES), lambda i: (0, 0))),
        scratch_shapes=[pltpu.VMEM((1, LANES), F32)],
        compiler_params=_params(("arbitrary",)),
        name="moe_rank",
    )(info)


def _dispatch_plan(route, cnt, t, tm):
    n_tiles = t // tm + N_GROUPS
    rows = n_tiles * tm
    rank = route[:, 0].astype(jnp.int32)
    gid = route[:, 1].astype(jnp.int32)
    counts = cnt[0, :N_GROUPS].astype(jnp.int32)
    padded = (counts + tm - 1) // tm * tm
    ends = jnp.cumsum(padded)
    pos = (ends - padded)[gid] + rank
    tok = jnp.arange(t, dtype=jnp.int32)
    row = jnp.arange(rows, dtype=jnp.int32)
    src = jnp.zeros((rows,), jnp.int32).at[pos].set(tok)
    dst = (t + row % tm).at[pos].set(tok)
    dst = jnp.concatenate([t + row[:tm], dst])
    tile_start = jnp.arange(n_tiles, dtype=jnp.int32) * tm
    tile_group = jnp.minimum(jnp.sum(tile_start[:, None] >= ends[None, :], axis=1), N_GROUPS - 1)
    return src, dst, tile_group.astype(jnp.int32), n_tiles


def _expert_kernel(src_ref, dst_ref, tg_ref, ext_hbm, wg_ref, wu_ref, wd_ref, g_ref, b_ref, out_hbm,
                   xa, xb, oa, ob, gsem, ssem, *, tm, n_tiles):
    i = pl.program_id(0)
    bufs = ((xa, oa, 0), (xb, ob, 1))

    def gather_copy(tile, buf, r):
        return pltpu.make_async_copy(ext_hbm.at[pl.ds(src_ref[tile * tm + r], 1)],
                                     buf[0].at[pl.ds(r, 1)], gsem.at[buf[2]])

    def scatter_copy(tile, buf, r):
        return pltpu.make_async_copy(buf[1].at[pl.ds(r, 1)],
                                     out_hbm.at[pl.ds(dst_ref[(tile + 1) * tm + r], 1)],
                                     ssem.at[buf[2]])

    def wait_gather(buf):
        pltpu.make_async_copy(ext_hbm.at[pl.ds(0, tm)], buf[0], gsem.at[buf[2]]).wait()

    def wait_scatter(buf):
        pltpu.make_async_copy(buf[1], out_hbm.at[pl.ds(0, tm)], ssem.at[buf[2]]).wait()

    @pl.when(i == 0)
    def _():
        ob[...] = jnp.zeros(ob.shape, F32)
        for r in range(tm):
            gather_copy(0, bufs[0], r).start()

    def tile_body(cur, oth):
        wait_gather(cur)

        @pl.when(i >= 1)
        def _():
            wait_scatter(cur)

        x = cur[0][:, :D_MODEL]
        info = cur[0][:, D_MODEL:]
        xb_ = x.astype(BF16)
        acc = jnp.zeros((tm, D_MODEL), F32)
        per = tm // EXPERTS_PER_GROUP
        nxt = jnp.minimum(i + 1, n_tiles - 1)
        for e in range(EXPERTS_PER_GROUP):
            act = (_silu(jnp.dot(xb_, wg_ref[0, e], preferred_element_type=F32))
                   * jnp.dot(xb_, wu_ref[0, e], preferred_element_type=F32) * info[:, e:e + 1])
            acc = acc + _dot(act, wd_ref[0, e])
            for r in range(e * per, (e + 1) * per):
                gather_copy(nxt, oth, r).start()
                scatter_copy(i - 1, oth, r).start()
        cur[1][...] = _layer_norm(DEEPNORM_ALPHA * x + acc, g_ref[...], b_ref[...])

        @pl.when(i == n_tiles - 1)
        def _():
            wait_scatter(oth)
            for r in range(tm):
                scatter_copy(i, cur, r).start()
            wait_scatter(cur)
            wait_gather(oth)

    @pl.when(i % 2 == 0)
    def _():
        tile_body(bufs[0], bufs[1])

    @pl.when(i % 2 == 1)
    def _():
        tile_body(bufs[1], bufs[0])


def _expert_call(ext, plan, w, tm):
    src, dst, tile_group, n_tiles = plan
    t = ext.shape[0]
    d = D_MODEL
    chunks = d // LANES
    wspec = lambda shape: pl.BlockSpec((1,) + shape, lambda i, s, d_, tg: (tg[i], 0, 0, 0))
    vec = pl.BlockSpec((1, d), lambda i, s, d_, tg: (0, 0))
    grid_spec = pltpu.PrefetchScalarGridSpec(
        num_scalar_prefetch=3,
        grid=(n_tiles,),
        in_specs=[pl.BlockSpec(memory_space=pl.ANY), wspec((EXPERTS_PER_GROUP, d, EXPERT_FF)),
                  wspec((EXPERTS_PER_GROUP, d, EXPERT_FF)), wspec((EXPERTS_PER_GROUP, EXPERT_FF, d)),
                  vec, vec],
        out_specs=pl.BlockSpec(memory_space=pl.ANY),
        scratch_shapes=[pltpu.VMEM((tm, EXT_COLS), F32), pltpu.VMEM((tm, EXT_COLS), F32),
                        pltpu.VMEM((tm, d), F32), pltpu.VMEM((tm, d), F32),
                        pltpu.SemaphoreType.DMA((2,)), pltpu.SemaphoreType.DMA((2,))],
    )
    out = pl.pallas_call(
        functools.partial(_expert_kernel, tm=tm, n_tiles=n_tiles),
        out_shape=jax.ShapeDtypeStruct((t + tm, d), F32),
        grid_spec=grid_spec,
        compiler_params=_params(("arbitrary",)),
        name="moe_experts",
    )(src, dst, tile_group, ext, w["e_gate"], w["e_up"], w["e_down"], w["ln2_g"], w["ln2_b"])
    return out[:t]


def _moe_call(ext, info, w, tm=MOE_TILE):
    route, cnt = _rank_call(info)
    return _expert_call(ext, _dispatch_plan(route, cnt, ext.shape[0], tm), w, tm)


def _pad_cols(w, lo, total):
    return jnp.zeros((w.shape[0], total), w.dtype).at[:, lo:lo + w.shape[1]].set(w)


def _layer_weights(l, p):
    d = D_MODEL
    w_in = p["w_in"][l]
    sizes = (MLA_Q_RANK, MLA_KV_RANK, MLA_ROPE, DN_HEADS * (2 * DN_DK + DN_DV), DN_HEADS, DN_HEADS,
             DN_HEADS * DN_DV, HG_HEADS * HG_DK, HG_HEADS * HG_DK, HG_HEADS * HG_DV,
             HG_HEADS * HG_DV, 3 * d)
    offs = np.concatenate([[0], np.cumsum(sizes)])
    col = lambda i: w_in[:, offs[i]:offs[i + 1]]
    qk_w = MLA_NOPE + MLA_ROPE
    uq = p["mla_w_uq"][l].reshape(MLA_Q_RANK, MLA_HEADS, qk_w)
    uq = jnp.pad(uq, ((0, 0), (0, 0), (0, HEAD_PAD - qk_w))).reshape(MLA_Q_RANK, MLA_HEADS * HEAD_PAD)
    ukv = p["mla_w_ukv"][l].reshape(MLA_KV_RANK, MLA_HEADS, MLA_NOPE + MLA_V)
    uk = jnp.pad(ukv[:, :, :MLA_NOPE], ((0, 0), (0, 0), (0, HEAD_PAD - MLA_NOPE)))
    uk = uk.reshape(MLA_KV_RANK, MLA_HEADS * HEAD_PAD)
    uv = ukv[:, :, MLA_NOPE:].reshape(MLA_KV_RANK, MLA_HEADS * MLA_V)
    ba = jnp.concatenate([col(4), col(5)], axis=1)
    w_dn = jnp.concatenate([col(3), col(6), _pad_cols(ba, 0, LANES)], axis=1)
    w_hg = jnp.concatenate([col(7), col(8), col(9), col(10)], axis=1)
    router = jnp.concatenate([p["router_group_w"][l], p["router_expert_w"][l]], axis=1)
    router = _pad_cols(router, 0, LANES)
    r_hi = router.astype(BF16)
    r_b = jnp.concatenate([p["router_group_b"][l], p["router_expert_b"][l]])
    bf = lambda a: a.astype(BF16)
    return dict(
        wcq=bf(col(0)), wckv=bf(col(1)), wkr=bf(_pad_cols(col(2), ROPE_LO, LANES)),
        qn=p["mla_q_norm"][l].reshape(1, -1), kvn=p["mla_kv_norm"][l].reshape(1, -1),
        wuq=bf(uq), wuk=bf(uk), wuv=bf(uv),
        w_dn=bf(w_dn), w_hg=bf(w_hg), wgate=bf(col(11)),
        dn_conv=p["dn_conv"][l],
        alog128=_pad_cols(p["dn_a_log"][l].reshape(1, -1), DN_HEADS, LANES),
        dtb128=_pad_cols(p["dn_dt_bias"][l].reshape(1, -1), DN_HEADS, LANES),
        dn_onorm=p["dn_o_norm"][l].reshape(1, -1), hg_onorm=p["hg_o_norm"][l].reshape(1, -1),
        wbr_a=bf(p["w_br_a"][l]), wbr_b=bf(p["w_br_b"][l]), wbr_c=bf(p["w_br_c"][l]),
        wout=bf(p["w_out"][l]),
        ln1_g=p["ln1_g"][l].reshape(1, d), ln1_b=p["ln1_b"][l].reshape(1, d),
        r_hi=r_hi, r_lo=bf(router - r_hi.astype(F32)), r_b=_pad_cols(r_b.reshape(1, -1), 0, LANES),
        e_gate=bf(p["exp_w_gate"][l]).reshape(N_GROUPS, EXPERTS_PER_GROUP, d, EXPERT_FF),
        e_up=bf(p["exp_w_up"][l]).reshape(N_GROUPS, EXPERTS_PER_GROUP, d, EXPERT_FF),
        e_down=bf(p["exp_w_down"][l]).reshape(N_GROUPS, EXPERTS_PER_GROUP, EXPERT_FF, d),
        ln2_g=p["ln2_g"][l].reshape(1, d), ln2_b=p["ln2_b"][l].reshape(1, d),
    )


def _forward(p):
    x = p["x"]
    batch, s, d = x.shape
    t = batch * s
    half = MLA_ROPE // 2
    inv_freq = ROPE_BASE ** (-jnp.arange(half, dtype=F32) / half)
    inv128 = _pad_cols(jnp.concatenate([inv_freq, inv_freq]).reshape(1, -1), ROPE_LO, LANES)
    tables = _rope_call(p["positions"].reshape(t, 1), inv128)
    h = _ln_call(x.reshape(t, d), p["ln_in_g"], p["ln_in_b"])
    for l in range(DEPTH):
        w = _layer_weights(l, p)
        q, k, v = _mla_proj_call(h, w, tables)
        oa = _flash_call(q, k, v, batch)
        x_dn = _proj_call(h, w["w_dn"], "proj_deltanet")
        ob = _dn_call(x_dn.reshape(batch, s, -1), w["dn_conv"], w["alog128"], w["dtb128"],
                      w["dn_onorm"])
        x_hg = _proj_call(h, w["w_hg"], "proj_hgrn2")
        oc = _hg_call(x_hg.reshape(batch, s, -1), p["hg_lower_bounds"], w["hg_onorm"], l)
        ext, info = _merge_call(h, oa, ob.reshape(t, -1), oc.reshape(t, -1), w)
        h = _moe_call(ext, info, w)
    return h.reshape(batch, s, d)


def kernel(x, positions, ln_in_g, ln_in_b, hg_lower_bounds, w_in, mla_q_norm, mla_w_uq, mla_kv_norm, mla_w_ukv, dn_conv, dn_a_log, dn_dt_bias, dn_o_norm, hg_o_norm, w_br_a, w_br_b, w_br_c, w_out, ln1_g, ln1_b, router_group_w, router_group_b, router_expert_w, router_expert_b, exp_w_gate, exp_w_up, exp_w_down, ln2_g, ln2_b):
    return _forward(dict(
        x=x, positions=positions, ln_in_g=ln_in_g, ln_in_b=ln_in_b,
        hg_lower_bounds=hg_lower_bounds, w_in=w_in, mla_q_norm=mla_q_norm, mla_w_uq=mla_w_uq,
        mla_kv_norm=mla_kv_norm, mla_w_ukv=mla_w_ukv, dn_conv=dn_conv, dn_a_log=dn_a_log,
        dn_dt_bias=dn_dt_bias, dn_o_norm=dn_o_norm, hg_o_norm=hg_o_norm, w_br_a=w_br_a,
        w_br_b=w_br_b, w_br_c=w_br_c, w_out=w_out, ln1_g=ln1_g, ln1_b=ln1_b,
        router_group_w=router_group_w, router_group_b=router_group_b,
        router_expert_w=router_expert_w, router_expert_b=router_expert_b,
        exp_w_gate=exp_w_gate, exp_w_up=exp_w_up, exp_w_down=exp_w_down, ln2_g=ln2_g, ln2_b=ln2_b))
```

```python
import functools
import math

import jax
import jax.numpy as jnp
import numpy as np
from jax import lax
from jax.experimental import pallas as pl
from jax.experimental.pallas import tpu as pltpu

D_MODEL = 1024
DEPTH = 2
MLA_HEADS = 8
MLA_Q_RANK = 256
MLA_KV_RANK = 128
MLA_NOPE = 64
MLA_ROPE = 32
MLA_V = 64
ROPE_BASE = 10000.0
MASK_VALUE = -1e30
DN_HEADS = 4
DN_DK = 128
DN_DV = 128
DN_CONV = 4
HG_HEADS = 4
HG_DK = 128
HG_DV = 128
CHUNK = 64
MIN_FORGET = 1e-30
N_GROUPS = 4
EXPERTS_PER_GROUP = 8
N_EXPERTS = N_GROUPS * EXPERTS_PER_GROUP
EXPERT_FF = 256
DEEPNORM_ALPHA = (2 * DEPTH) ** 0.25
NORM_EPS = 1e-6

LANES = 128
HEAD_PAD = 128
ROPE_LO = MLA_NOPE
ROPE_MID = MLA_NOPE + MLA_ROPE // 2
ROPE_HI = MLA_NOPE + MLA_ROPE
SCAN_ROWS = 256
ATTN_TILE = 512
EXT_COLS = D_MODEL + LANES
MOE_TILE = 256
VMEM_LIMIT = 56 * 1024 * 1024

BF16 = jnp.bfloat16
F32 = jnp.float32


def _dot(a, b):
    return jnp.dot(a.astype(BF16), b.astype(BF16), preferred_element_type=F32)


def _dot_nt(a, b):
    return lax.dot_general(a.astype(BF16), b.astype(BF16), (((1,), (1,)), ((), ())),
                           preferred_element_type=F32)


def _dot_tn(a, b):
    return lax.dot_general(a.astype(BF16), b.astype(BF16), (((0,), (0,)), ((), ())),
                           preferred_element_type=F32)


def _sigmoid(x):
    return 1.0 / (1.0 + jnp.exp(-x))


def _silu(x):
    return x * _sigmoid(x)


def _layer_norm(x, g, b):
    mu = jnp.mean(x, axis=-1, keepdims=True)
    xc = x - mu
    var = jnp.mean(xc * xc, axis=-1, keepdims=True)
    return xc * lax.rsqrt(var + NORM_EPS) * g + b


def _rms_norm(x, g):
    return x * lax.rsqrt(jnp.mean(x * x, axis=-1, keepdims=True) + NORM_EPS) * g


def _params(sem):
    return pltpu.CompilerParams(dimension_semantics=sem, vmem_limit_bytes=VMEM_LIMIT)


def _full(shape):
    n = len(shape)
    return pl.BlockSpec(shape, lambda *_: (0,) * n)


def _ln_kernel(x_ref, g_ref, b_ref, o_ref):
    o_ref[...] = _layer_norm(x_ref[...], g_ref[...], b_ref[...])


def _ln_call(x, g, b, tm=1024):
    t, d = x.shape
    return pl.pallas_call(
        _ln_kernel,
        out_shape=jax.ShapeDtypeStruct((t, d), F32),
        grid=(t // tm,),
        in_specs=[pl.BlockSpec((tm, d), lambda i: (i, 0)), _full((1, d)), _full((1, d))],
        out_specs=pl.BlockSpec((tm, d), lambda i: (i, 0)),
        compiler_params=_params(("parallel",)),
        name="ln_in",
    )(x, g.reshape(1, d), b.reshape(1, d))


def _rope_kernel(pos_ref, inv_ref, c_ref, sa_ref, sb_ref):
    ang = pos_ref[...].astype(F32) * inv_ref[...]
    lane = lax.broadcasted_iota(jnp.int32, ang.shape, 1)
    sin = jnp.sin(ang)
    c_ref[...] = jnp.cos(ang)
    sa_ref[...] = jnp.where((lane >= ROPE_MID) & (lane < ROPE_HI), sin, 0.0)
    sb_ref[...] = jnp.where((lane >= ROPE_LO) & (lane < ROPE_MID), -sin, 0.0)


def _rope_call(pos_col, inv128, tm=1024):
    t = pos_col.shape[0]
    spec = pl.BlockSpec((tm, LANES), lambda i: (i, 0))
    shp = jax.ShapeDtypeStruct((t, LANES), F32)
    return pl.pallas_call(
        _rope_kernel,
        out_shape=(shp, shp, shp),
        grid=(t // tm,),
        in_specs=[pl.BlockSpec((tm, 1), lambda i: (i, 0)), _full((1, LANES))],
        out_specs=(spec, spec, spec),
        compiler_params=_params(("parallel",)),
        name="rope_tables",
    )(pos_col, inv128)


def _rope_apply(x, c, sa, sb):
    return x * c + pltpu.roll(x, 16, 1) * sa + pltpu.roll(x, LANES - 16, 1) * sb


def _mla_proj_kernel(h_ref, wcq_ref, wckv_ref, wkr_ref, qn_ref, kvn_ref, wuq_ref, wuk_ref,
                     wuv_ref, c_ref, sa_ref, sb_ref, qt_ref, k_ref, vt_ref):
    hb = h_ref[...].astype(BF16)
    cq = _rms_norm(_dot(hb, wcq_ref[...]), qn_ref[...])
    ckv = _rms_norm(_dot(hb, wckv_ref[...]), kvn_ref[...])
    kr = _dot(hb, wkr_ref[...])
    q = _dot(cq, wuq_ref[...])
    k = _dot(ckv, wuk_ref[...])
    v = _dot(ckv, wuv_ref[...])
    for pr in range(MLA_HEADS // 2):
        sl = slice(pr * 2 * MLA_V, (pr + 1) * 2 * MLA_V)
        vt_ref[0, sl, :] = v[:, sl].T.astype(BF16)
    c, sa, sb = c_ref[...], sa_ref[...], sb_ref[...]
    scale = (MLA_NOPE + MLA_ROPE) ** -0.5 * math.log2(math.e)
    kr = _rope_apply(kr, c, sa, sb)
    for hd in range(MLA_HEADS):
        sl = slice(hd * HEAD_PAD, (hd + 1) * HEAD_PAD)
        qt_ref[0, sl, :] = (_rope_apply(q[:, sl], c, sa, sb) * scale).T.astype(BF16)
        k_ref[:, sl] = (k[:, sl] + kr).astype(BF16)


def _mla_proj_call(h, t, w, tables, tm=ATTN_TILE):
    d = h.shape[1]
    hq = MLA_HEADS * HEAD_PAD
    hv = MLA_HEADS * MLA_V
    row = lambda n: pl.BlockSpec((tm, n), lambda i: (i, 0))
    tile = lambda n: pl.BlockSpec((1, n, tm), lambda i: (i, 0, 0))
    return pl.pallas_call(
        _mla_proj_kernel,
        out_shape=(jax.ShapeDtypeStruct((t // tm, hq, tm), BF16),
                   jax.ShapeDtypeStruct((t, hq), BF16),
                   jax.ShapeDtypeStruct((t // tm, hv, tm), BF16)),
        grid=(t // tm,),
        in_specs=[row(d), _full((d, MLA_Q_RANK)), _full((d, MLA_KV_RANK)), _full((d, LANES)),
                  _full((1, MLA_Q_RANK)), _full((1, MLA_KV_RANK)), _full((MLA_Q_RANK, hq)),
                  _full((MLA_KV_RANK, hq)), _full((MLA_KV_RANK, hv)),
                  row(LANES), row(LANES), row(LANES)],
        out_specs=(tile(hq), row(hq), tile(hv)),
        compiler_params=_params(("parallel",)),
        name="mla_proj",
    )(h, w["wcq"], w["wckv"], w["wkr"], w["qn"], w["kvn"], w["wuq"], w["wuk"], w["wuv"], *tables)


def _flash_kernel(qt_ref, k_ref, vt_ref, o_ref, m0, l0, a0, m1, l1, a1, sa_ref, sb_ref,
                  *, tq):
    qi = pl.program_id(2)
    stats = ((m0, l0, a0), (m1, l1, a1))
    for m_ref, l_ref, acc_ref in stats:
        m_ref[...] = jnp.full(m_ref.shape, MASK_VALUE, F32)
        l_ref[...] = jnp.zeros(l_ref.shape, F32)
        acc_ref[...] = jnp.zeros(acc_ref.shape, F32)

    def score(j, s_ref):
        rows = pl.ds(pl.multiple_of(j * tq, tq), tq)
        for hh in range(2):
            sl = slice(hh * HEAD_PAD, (hh + 1) * HEAD_PAD)
            s_ref[hh] = jnp.dot(k_ref[0, rows, sl], qt_ref[0, 0, sl, :],
                                preferred_element_type=F32)

    def consume(j, s_ref, masked):
        vt = vt_ref[0, j, 0]
        for hh, (m_ref, l_ref, acc_ref) in enumerate(stats):
            st = s_ref[hh]
            if masked:
                key = lax.broadcasted_iota(jnp.int32, st.shape, 0)
                qry = lax.broadcasted_iota(jnp.int32, st.shape, 1)
                st = jnp.where(key <= qry, st, MASK_VALUE)
            m_old = m_ref[...]
            m_new = jnp.maximum(m_old, jnp.max(st, axis=0, keepdims=True))
            alpha = jnp.exp2(m_old - m_new)
            p = jnp.exp2(st - m_new)
            l_ref[...] = alpha * l_ref[...] + jnp.sum(p, axis=0, keepdims=True)
            acc_ref[...] = alpha * acc_ref[...] + jnp.dot(vt, p.astype(BF16),
                                                          preferred_element_type=F32)
            m_ref[...] = m_new

    score(0, sa_ref)

    def body(i, carry):
        j = 2 * i
        score(j + 1, sb_ref)
        consume(j, sa_ref, False)
        score(j + 2, sa_ref)
        consume(j + 1, sb_ref, False)
        return carry

    lax.fori_loop(0, qi // 2, body, 0)

    @pl.when(qi % 2 == 0)
    def _():
        consume(qi, sa_ref, True)

    @pl.when(qi % 2 == 1)
    def _():
        score(qi, sb_ref)
        consume(qi - 1, sa_ref, False)
        consume(qi, sb_ref, True)
    row = lax.broadcasted_iota(jnp.int32, a0.shape, 0)
    ot = jnp.where(row < MLA_V, a0[...] * (1.0 / l0[...]), a1[...] * (1.0 / l1[...]))
    o_ref[0] = ot.T.astype(BF16)


def _flash_call(qt, k, vt, batch, tq=ATTN_TILE):
    nt, hq, _ = qt.shape
    t = nt * tq
    s = t // batch
    ns = s // tq
    pairs = MLA_HEADS // 2
    hv = MLA_HEADS * MLA_V
    qt5 = qt.reshape(batch, ns, hq, tq)
    k3 = k.reshape(batch, s, hq)
    vt5 = vt.reshape(batch, ns, pairs, 2 * MLA_V, tq)
    stat = lambda: pltpu.VMEM((1, tq), F32)
    acc = lambda: pltpu.VMEM((2 * MLA_V, tq), F32)
    out = pl.pallas_call(
        functools.partial(_flash_kernel, tq=tq),
        out_shape=jax.ShapeDtypeStruct((batch, s, hv), BF16),
        grid=(batch, pairs, ns),
        in_specs=[pl.BlockSpec((1, 1, 2 * HEAD_PAD, tq), lambda b, p, i: (b, i, p, 0)),
                  pl.BlockSpec((1, s, 2 * HEAD_PAD), lambda b, p, i: (b, 0, p)),
                  pl.BlockSpec((1, ns, 1, 2 * MLA_V, tq), lambda b, p, i: (b, 0, p, 0, 0))],
        out_specs=pl.BlockSpec((1, tq, 2 * MLA_V), lambda b, p, i: (b, i, p)),
        scratch_shapes=[stat(), stat(), acc(), stat(), stat(), acc(),
                        pltpu.VMEM((2, tq, tq), F32), pltpu.VMEM((2, tq, tq), F32)],
        compiler_params=_params(("parallel", "parallel", "arbitrary")),
        name="flash_attn",
    )(qt5, k3, vt5)
    return out.reshape(t, hv)


def _proj_kernel(h_ref, w_ref, o_ref):
    o_ref[...] = _dot(h_ref[...], w_ref[...])


def _proj_call(h, t, w, name, tm=512):
    d = h.shape[1]
    n = w.shape[1]
    return pl.pallas_call(
        _proj_kernel,
        out_shape=jax.ShapeDtypeStruct((t, n), F32),
        grid=(t // tm,),
        in_specs=[pl.BlockSpec((tm, d), lambda i: (i, 0)), _full((d, n))],
        out_specs=pl.BlockSpec((tm, n), lambda i: (i, 0)),
        compiler_params=_params(("parallel",)),
        name=name,
    )(h, w)


def _chunk_cumsum(x):
    rows = lax.broadcasted_iota(jnp.int32, x.shape, 0) % CHUNK
    shift = 1
    while shift < CHUNK:
        x = x + jnp.where(rows >= shift, pltpu.roll(x, shift, 0), 0.0)
        shift *= 2
    return x


def _chunk_last(x):
    r, n = x.shape
    x3 = x.reshape(r // CHUNK, CHUNK, n)
    return jnp.broadcast_to(x3[:, CHUNK - 1:CHUNK, :], x3.shape).reshape(r, n)


def _chunk_masks(r):
    i = lax.broadcasted_iota(jnp.int32, (r, r), 0)
    j = lax.broadcasted_iota(jnp.int32, (r, r), 1)
    same = (i // CHUNK) == (j // CHUNK)
    return same & (j <= i), same & (j < i), i == j


def _dn_solve(heads, eye):
    invs = [eye - p["m"] for p in heads]
    pws = [p["m"].astype(BF16) for p in heads]
    for _ in range(5):
        pws = [jnp.dot(pw, pw, preferred_element_type=F32).astype(BF16) for pw in pws]
        invs = [inv + jnp.dot(inv.astype(BF16), pw, preferred_element_type=F32)
                for inv, pw in zip(invs, pws)]
    for p, inv in zip(heads, invs):
        uw = jnp.dot(inv.astype(BF16), p["rhs"], preferred_element_type=F32).astype(BF16)
        quw = jnp.dot(p["qk"].astype(BF16), uw, preferred_element_type=F32)
        p["o_own"] = quw[:, :DN_DV]
        q_eff = (p["qd"] - quw[:, DN_DV:]).astype(BF16)
        kd = p["kd"].astype(BF16)
        p["lhs"], p["add"] = [], []
        for c in range(uw.shape[0] // CHUNK):
            rc = slice(c * CHUNK, (c + 1) * CHUNK)
            kuw = lax.dot_general(kd[rc], uw[rc], (((0,), (0,)), ((), ())),
                                  preferred_element_type=F32)
            p["add"].append(kuw[:, :DN_DV])
            p["lhs"].append(jnp.concatenate([kuw[:, DN_DV:].astype(BF16), q_eff[rc]], axis=0))


def _dn_kernel(x_ref, conv_ref, alog_ref, dtb_ref, onorm_ref, o_ref, halo_ref, state_ref,
               *, batch):
    r = SCAN_ROWS
    nqkv = DN_HEADS * (2 * DN_DK + DN_DV)
    gate_off = nqkv
    ba_off = nqkv + DN_HEADS * DN_DV

    @pl.when(pl.program_id(0) == 0)
    def _():
        halo_ref[...] = jnp.zeros(halo_ref.shape, F32)
        state_ref[...] = jnp.zeros(state_ref.shape, F32)

    incl, strict, diag = _chunk_masks(r)
    eye = jnp.where(diag, 1.0, 0.0)
    cw = conv_ref[...]
    pre = []
    for b in range(batch):
        xq = x_ref[b, :, :nqkv]
        xe = jnp.concatenate([halo_ref[b], xq], axis=0)
        halo_ref[b] = xq[r - 8:, :]
        y = xe[8:, :] * cw[DN_CONV - 1:DN_CONV, :]
        for i in range(DN_CONV - 1):
            y = y + pltpu.roll(xe, DN_CONV - 1 - i, 0)[8:, :] * cw[i:i + 1, :]
        y = _silu(y)
        ba = x_ref[b, :, ba_off:ba_off + LANES]
        beta_all = _sigmoid(ba)
        av = ba + dtb_ref[...]
        softplus = jnp.maximum(av, 0.0) + jnp.log(1.0 + jnp.exp(-jnp.abs(av)))
        g_all = _chunk_cumsum(-jnp.exp(alog_ref[...]) * softplus)
        g_all_t = g_all.T
        g_last_all = _chunk_last(g_all)
        for hd in range(DN_HEADS):
            q = y[:, hd * DN_DK:(hd + 1) * DN_DK]
            k = y[:, DN_HEADS * DN_DK + hd * DN_DK:DN_HEADS * DN_DK + (hd + 1) * DN_DK]
            v = y[:, 2 * DN_HEADS * DN_DK + hd * DN_DV:2 * DN_HEADS * DN_DK + (hd + 1) * DN_DV]
            q = q * lax.rsqrt(jnp.sum(q * q, axis=-1, keepdims=True) + NORM_EPS) * DN_DK ** -0.5
            k = k * lax.rsqrt(jnp.sum(k * k, axis=-1, keepdims=True) + NORM_EPS)
            beta = beta_all[:, hd:hd + 1]
            g_col = g_all[:, DN_HEADS + hd:DN_HEADS + hd + 1]
            g_row = g_all_t[DN_HEADS + hd:DN_HEADS + hd + 1, :]
            g_last = g_last_all[:, DN_HEADS + hd:DN_HEADS + hd + 1]
            decay = jnp.where(incl, jnp.exp(jnp.where(incl, g_col - g_row, 0.0)), 0.0)
            kb = k * beta
            kk = _dot_nt(kb, k)
            m = jnp.where(strict, kk * decay, 0.0)
            qk = jnp.where(incl, _dot_nt(q, k) * decay, 0.0)
            eg = jnp.exp(g_col)
            pre.append(dict(m=m, rhs=jnp.concatenate([v * beta, kb * eg], axis=1).astype(BF16),
                            qd=q * eg, kd=k * jnp.exp(g_last - g_col), qk=qk,
                            gl=jnp.exp(g_last)))
        _dn_solve(pre[b * DN_HEADS:], eye)
    outs = [[] for _ in pre]
    for c in range(r // CHUNK):
        rc = slice(c * CHUNK, (c + 1) * CHUNK)
        for idx, p in enumerate(pre):
            st = state_ref[idx]
            prod = jnp.dot(p["lhs"][c], st.astype(BF16), preferred_element_type=F32)
            outs[idx].append(prod[DN_DK:] + p["o_own"][rc])
            state_ref[idx] = (st * p["gl"][c * CHUNK:c * CHUNK + 1, :] - prod[:DN_DK]
                              + p["add"][c])
    for idx in range(len(pre)):
        b, hd = divmod(idx, DN_HEADS)
        o = jnp.concatenate(outs[idx], axis=0)
        gate = x_ref[b, :, gate_off + hd * DN_DV:gate_off + (hd + 1) * DN_DV]
        o_ref[b, :, hd * DN_DV:(hd + 1) * DN_DV] = (
            _rms_norm(o, onorm_ref[...]) * _silu(gate)).astype(BF16)


def _dn_call(x3, conv_w, alog128, dtb128, onorm):
    batch, s, n = x3.shape
    r = SCAN_ROWS
    nqkv = DN_HEADS * (2 * DN_DK + DN_DV)
    width = DN_HEADS * DN_DV
    return pl.pallas_call(
        functools.partial(_dn_kernel, batch=batch),
        out_shape=jax.ShapeDtypeStruct((batch, s, width), BF16),
        grid=(s // r,),
        in_specs=[pl.BlockSpec((batch, r, n), lambda i: (0, i, 0)), _full((DN_CONV, nqkv)),
                  _full((1, LANES)), _full((1, LANES)), _full((1, DN_DV))],
        out_specs=pl.BlockSpec((batch, r, width), lambda i: (0, i, 0)),
        scratch_shapes=[pltpu.VMEM((batch, 8, nqkv), F32),
                        pltpu.VMEM((batch * DN_HEADS, DN_DK, DN_DV), F32)],
        compiler_params=_params(("arbitrary",)),
        name="deltanet",
    )(x3, conv_w, alog128, dtb128, onorm)


HG_LEVELS = (32, 16, 8, 4, 2, 1)


def _hg_kernel(x_ref, lbp_ref, onorm_ref, sel_ref, o_ref, state_ref, *, batch, layer):
    r = SCAN_ROWS
    width = HG_HEADS * HG_DK

    @pl.when(pl.program_id(0) == 0)
    def _():
        state_ref[...] = jnp.zeros(state_ref.shape, F32)

    lbp = lbp_ref[...]
    e = jnp.exp(lbp - jnp.max(lbp, axis=0, keepdims=True))
    soft = e / jnp.sum(e, axis=0, keepdims=True)
    lb_all = jnp.zeros((1, width), F32)
    for i in range(1, layer + 1):
        lb_all = lb_all + soft[i:i + 1, :]

    i_idx = lax.broadcasted_iota(jnp.int32, (r, r), 0)
    j_idx = lax.broadcasted_iota(jnp.int32, (r, r), 1)
    level_masks = [((i_idx // (2 * s)) == (j_idx // (2 * s))) & ((i_idx % (2 * s)) >= s)
                   & ((j_idx % (2 * s)) < s) for s in HG_LEVELS]
    diag = i_idx == j_idx

    for b in range(batch):
        for hd in range(HG_HEADS):
            idx = b * HG_HEADS + hd
            hs = slice(hd * HG_DK, (hd + 1) * HG_DK)
            lb = lb_all[:, hs]
            q = _silu(x_ref[b, :, hd * HG_DK:(hd + 1) * HG_DK])
            z = x_ref[b, :, width + hd * HG_DK:width + (hd + 1) * HG_DK]
            v = x_ref[b, :, 2 * width + hd * HG_DV:2 * width + (hd + 1) * HG_DV]
            gate = x_ref[b, :, 3 * width + hd * HG_DV:3 * width + (hd + 1) * HG_DV]
            f = lb + (1.0 - lb) * _sigmoid(z)
            k = (1.0 - lb) * _sigmoid(-z)
            cum = _chunk_cumsum(jnp.log(jnp.maximum(f, MIN_FORGET)))
            cum_last = _chunk_last(cum)
            qd = q * jnp.exp(cum)
            kd = k * jnp.exp(cum_last - cum)
            dl = jnp.exp(cum_last)
            att = jnp.where(diag, _dot_nt(q, k), 0.0)
            cum_b = cum.astype(BF16)
            for lv, mask in enumerate(level_masks):
                ref = jnp.dot(sel_ref[lv], cum_b, preferred_element_type=F32)
                qs = q * jnp.exp(jnp.minimum(cum - ref, 60.0))
                ks = k * jnp.exp(jnp.minimum(ref - cum, 60.0))
                att = att + jnp.where(mask, _dot_nt(qs, ks), 0.0)
            o_intra = _dot(att, v)
            outs = []
            st = state_ref[idx]
            for c in range(r // CHUNK):
                rc = slice(c * CHUNK, (c + 1) * CHUNK)
                outs.append(_dot_nt(qd[rc], st) + o_intra[rc])
                st = st * dl[c * CHUNK:c * CHUNK + 1, :] + _dot_tn(v[rc], kd[rc])
            state_ref[idx] = st
            o = jnp.concatenate(outs, axis=0)
            o_ref[b, :, hs] = (_rms_norm(o, onorm_ref[...]) * _silu(gate)).astype(BF16)


def _hg_sel():
    r = SCAN_ROWS
    i = np.arange(r)
    sel = np.zeros((len(HG_LEVELS), r, r), np.float32)
    for lv, s in enumerate(HG_LEVELS):
        sel[lv, i, (i // (2 * s)) * (2 * s) + s] = 1.0
    return jnp.asarray(sel, BF16)


def _hg_call(x3, lbp, onorm, layer):
    batch, s, n = x3.shape
    r = SCAN_ROWS
    width = HG_HEADS * HG_DV
    return pl.pallas_call(
        functools.partial(_hg_kernel, batch=batch, layer=layer),
        out_shape=jax.ShapeDtypeStruct((batch, s, width), BF16),
        grid=(s // r,),
        in_specs=[pl.BlockSpec((batch, r, n), lambda i: (0, i, 0)), _full(lbp.shape),
                  _full((1, HG_DV)), _full((len(HG_LEVELS), r, r))],
        out_specs=pl.BlockSpec((batch, r, width), lambda i: (0, i, 0)),
        scratch_shapes=[pltpu.VMEM((batch * HG_HEADS, HG_DV, HG_DK), F32)],
        compiler_params=_params(("arbitrary",)),
        name="hgrn2",
    )(x3, lbp, onorm, _hg_sel())


def _merge_kernel(h_ref, oa_ref, ob_ref, oc_ref, wgate_ref, wa_ref, wb_ref, wc_ref, wout_ref,
                  g_ref, b_ref, rhi_ref, rlo_ref, rb_ref, ext_ref, info_ref):
    h = h_ref[...]
    gates = _sigmoid(_dot(h, wgate_ref[...]))
    d = D_MODEL
    mixed = (gates[:, :d] * _dot(oa_ref[...], wa_ref[...])
             + gates[:, d:2 * d] * _dot(ob_ref[...], wb_ref[...])
             + gates[:, 2 * d:] * _dot(oc_ref[...], wc_ref[...]))
    h1 = _layer_norm(DEEPNORM_ALPHA * h + _dot(mixed, wout_ref[...]), g_ref[...], b_ref[...])
    hi = h1.astype(BF16)
    lo = (h1 - hi.astype(F32)).astype(BF16)
    logits = (jnp.dot(hi, rhi_ref[...], preferred_element_type=F32)
              + jnp.dot(hi, rlo_ref[...], preferred_element_type=F32)
              + jnp.dot(lo, rhi_ref[...], preferred_element_type=F32)) + rb_ref[...]
    lane = lax.broadcasted_iota(jnp.int32, logits.shape, 1)
    neg = jnp.float32(-jnp.inf)
    big = jnp.int32(1 << 20)
    is_g = lane < N_GROUPS
    gl = jnp.where(is_g, logits, neg)
    gmax = jnp.max(gl, axis=-1, keepdims=True)
    gidx = jnp.min(jnp.where(gl == gmax, lane, big), axis=-1, keepdims=True)
    p_group = 1.0 / jnp.sum(jnp.where(is_g, jnp.exp(gl - gmax), 0.0), axis=-1, keepdims=True)
    lo_lane = N_GROUPS + gidx * EXPERTS_PER_GROUP
    in_g = (lane >= lo_lane) & (lane < lo_lane + EXPERTS_PER_GROUP)
    el = jnp.where(in_g, logits, neg)
    v1 = jnp.max(el, axis=-1, keepdims=True)
    i1 = jnp.min(jnp.where(el == v1, lane, big), axis=-1, keepdims=True)
    el2 = jnp.where(lane == i1, neg, el)
    v2 = jnp.max(el2, axis=-1, keepdims=True)
    i2 = jnp.min(jnp.where(el2 == v2, lane, big), axis=-1, keepdims=True)
    e21 = jnp.exp(v2 - v1)
    w1 = p_group / (1.0 + e21)
    w2 = p_group * e21 / (1.0 + e21)
    info = (jnp.where(lane == i1 - lo_lane, w1, 0.0) + jnp.where(lane == i2 - lo_lane, w2, 0.0)
            + jnp.where(lane == EXPERTS_PER_GROUP, gidx.astype(F32), 0.0))
    info_ref[...] = info
    ext_ref[:, :D_MODEL] = h1
    ext_ref[:, D_MODEL:] = info


def _merge_call(h, oa, ob, oc, w, tm=512):
    t, d = oa.shape[0], h.shape[1]
    row = lambda n: pl.BlockSpec((tm, n), lambda i: (i, 0))
    wa, wb, wc = w["wbr_a"], w["wbr_b"], w["wbr_c"]
    return pl.pallas_call(
        _merge_kernel,
        out_shape=(jax.ShapeDtypeStruct((t, EXT_COLS), F32),
                   jax.ShapeDtypeStruct((t, LANES), F32)),
        grid=(t // tm,),
        in_specs=[row(d), row(oa.shape[1]), row(ob.shape[1]), row(oc.shape[1]),
                  _full((d, 3 * d)), _full(wa.shape), _full(wb.shape), _full(wc.shape),
                  _full((d, d)), _full((1, d)), _full((1, d)),
                  _full((d, LANES)), _full((d, LANES)), _full((1, LANES))],
        out_specs=(row(EXT_COLS), row(LANES)),
        compiler_params=_params(("parallel",)),
        name="merge_ln_router",
    )(h, oa, ob, oc, w["wgate"], wa, wb, wc, w["wout"], w["ln1_g"], w["ln1_b"],
      w["r_hi"], w["r_lo"], w["r_b"])


def _rank_kernel(info_ref, route_ref, cnt_ref, base_ref):
    @pl.when(pl.program_id(0) == 0)
    def _():
        base_ref[...] = jnp.zeros(base_ref.shape, F32)

    info = info_ref[...]
    tb = info.shape[0]
    lane = lax.broadcasted_iota(jnp.int32, info.shape, 1)
    gid = info[:, EXPERTS_PER_GROUP:EXPERTS_PER_GROUP + 1]
    onehot = jnp.where((lane.astype(F32) == gid) & (lane < N_GROUPS), 1.0, 0.0)
    r = lax.broadcasted_iota(jnp.int32, (tb, tb), 0)
    c = lax.broadcasted_iota(jnp.int32, (tb, tb), 1)
    earlier = jnp.where(c < r, 1.0, 0.0)
    prefix = _dot(earlier, onehot)
    base = base_ref[...]
    rank = jnp.sum(onehot * (prefix + base), axis=1, keepdims=True)
    route_ref[...] = jnp.where(lane == 0, rank, jnp.where(lane == 1, gid, 0.0))
    base = base + jnp.sum(onehot, axis=0, keepdims=True)
    base_ref[...] = base
    cnt_ref[...] = base


def _rank_call(info, tb=512):
    t = info.shape[0]
    return pl.pallas_call(
        _rank_kernel,
        out_shape=(jax.ShapeDtypeStruct((t, LANES), F32), jax.ShapeDtypeStruct((1, LANES), F32)),
        grid=(t // tb,),
        in_specs=[pl.BlockSpec((tb, LANES), lambda i: (i, 0))],
        out_specs=(pl.BlockSpec((tb, LANES), lambda i: (i, 0)),
                   pl.BlockSpec((1, LANES), lambda i: (0, 0))),
        scratch_shapes=[pltpu.VMEM((1, LANES), F32)],
        compiler_params=_params(("arbitrary",)),
        name="moe_rank",
    )(info)


def _dispatch_plan(route, cnt, t, tm):
    n_tiles = t // tm + N_GROUPS
    rows = n_tiles * tm
    rank = route[:, 0].astype(jnp.int32)
    gid = route[:, 1].astype(jnp.int32)
    counts = cnt[0, :N_GROUPS].astype(jnp.int32)
    padded = (counts + tm - 1) // tm * tm
    ends = jnp.cumsum(padded)
    pos = (ends - padded)[gid] + rank
    tok = jnp.arange(t, dtype=jnp.int32)
    row = jnp.arange(rows, dtype=jnp.int32)
    owner = jnp.full((rows,), -1, jnp.int32).at[pos].set(tok, unique_indices=True,
                                                          mode="promise_in_bounds")
    src = jnp.maximum(owner, 0)
    dst = jnp.where(owner < 0, t + row % tm, owner)
    dst = jnp.concatenate([t + row[:tm], dst])
    tile_start = jnp.arange(n_tiles, dtype=jnp.int32) * tm
    tile_group = jnp.minimum(jnp.sum(tile_start[:, None] >= ends[None, :], axis=1), N_GROUPS - 1)
    return src, dst, tile_group.astype(jnp.int32), n_tiles


def _expert_kernel(src_ref, dst_ref, tg_ref, ext_hbm, wg_ref, wu_ref, wd_ref, g_ref, b_ref, out_hbm,
                   xa, xb, oa, ob, gsem, ssem, *, tm, n_tiles):
    i = pl.program_id(0)
    bufs = ((xa, oa, 0), (xb, ob, 1))

    def gather_copy(tile, buf, r):
        return pltpu.make_async_copy(ext_hbm.at[pl.ds(src_ref[tile * tm + r], 1)],
                                     buf[0].at[pl.ds(r, 1)], gsem.at[buf[2]])

    def scatter_copy(tile, buf, r):
        return pltpu.make_async_copy(buf[1].at[pl.ds(r, 1)],
                                     out_hbm.at[pl.ds(dst_ref[(tile + 1) * tm + r], 1)],
                                     ssem.at[buf[2]])

    def wait_gather(buf):
        pltpu.make_async_copy(ext_hbm.at[pl.ds(0, tm)], buf[0], gsem.at[buf[2]]).wait()

    def wait_scatter(buf):
        pltpu.make_async_copy(buf[1], out_hbm.at[pl.ds(0, tm)], ssem.at[buf[2]]).wait()

    @pl.when(i == 0)
    def _():
        ob[...] = jnp.zeros(ob.shape, F32)
        for r in range(tm):
            gather_copy(0, bufs[0], r).start()

    def tile_body(cur, oth):
        wait_gather(cur)

        @pl.when(i >= 1)
        def _():
            wait_scatter(cur)

        x = cur[0][:, :D_MODEL]
        info = cur[0][:, D_MODEL:]
        xb_ = x.astype(BF16)
        acc = jnp.zeros((tm, D_MODEL), F32)
        per = tm // EXPERTS_PER_GROUP
        nxt = jnp.minimum(i + 1, n_tiles - 1)
        for e in range(EXPERTS_PER_GROUP):
            act = (_silu(jnp.dot(xb_, wg_ref[0, e], preferred_element_type=F32))
                   * jnp.dot(xb_, wu_ref[0, e], preferred_element_type=F32) * info[:, e:e + 1])
            acc = acc + _dot(act, wd_ref[0, e])
            for r in range(e * per, (e + 1) * per):
                gather_copy(nxt, oth, r).start()
                scatter_copy(i - 1, oth, r).start()
        cur[1][...] = _layer_norm(DEEPNORM_ALPHA * x + acc, g_ref[...], b_ref[...])

        @pl.when(i == n_tiles - 1)
        def _():
            wait_scatter(oth)
            for r in range(tm):
                scatter_copy(i, cur, r).start()
            wait_scatter(cur)
            wait_gather(oth)

    @pl.when(i % 2 == 0)
    def _():
        tile_body(bufs[0], bufs[1])

    @pl.when(i % 2 == 1)
    def _():
        tile_body(bufs[1], bufs[0])


def _expert_call(ext, plan, w, tm):
    src, dst, tile_group, n_tiles = plan
    t = ext.shape[0]
    d = D_MODEL
    chunks = d // LANES
    wspec = lambda shape: pl.BlockSpec((1,) + shape, lambda i, s, d_, tg: (tg[i], 0, 0, 0))
    vec = pl.BlockSpec((1, d), lambda i, s, d_, tg: (0, 0))
    grid_spec = pltpu.PrefetchScalarGridSpec(
        num_scalar_prefetch=3,
        grid=(n_tiles,),
        in_specs=[pl.BlockSpec(memory_space=pl.ANY), wspec((EXPERTS_PER_GROUP, d, EXPERT_FF)),
                  wspec((EXPERTS_PER_GROUP, d, EXPERT_FF)), wspec((EXPERTS_PER_GROUP, EXPERT_FF, d)),
                  vec, vec],
        out_specs=pl.BlockSpec(memory_space=pl.ANY),
        scratch_shapes=[pltpu.VMEM((tm, EXT_COLS), F32), pltpu.VMEM((tm, EXT_COLS), F32),
                        pltpu.VMEM((tm, d), F32), pltpu.VMEM((tm, d), F32),
                        pltpu.SemaphoreType.DMA((2,)), pltpu.SemaphoreType.DMA((2,))],
    )
    out = pl.pallas_call(
        functools.partial(_expert_kernel, tm=tm, n_tiles=n_tiles),
        out_shape=jax.ShapeDtypeStruct((t + tm, d), F32),
        grid_spec=grid_spec,
        compiler_params=_params(("arbitrary",)),
        name="moe_experts",
    )(src, dst, tile_group, ext, w["e_gate"], w["e_up"], w["e_down"], w["ln2_g"], w["ln2_b"])
    return out


def _moe_call(ext, info, w, tm=MOE_TILE):
    route, cnt = _rank_call(info)
    return _expert_call(ext, _dispatch_plan(route, cnt, ext.shape[0], tm), w, tm)


def _pad_cols(w, lo, total):
    return jnp.zeros((w.shape[0], total), w.dtype).at[:, lo:lo + w.shape[1]].set(w)


def _layer_weights(l, p):
    d = D_MODEL
    w_in = p["w_in"][l]
    sizes = (MLA_Q_RANK, MLA_KV_RANK, MLA_ROPE, DN_HEADS * (2 * DN_DK + DN_DV), DN_HEADS, DN_HEADS,
             DN_HEADS * DN_DV, HG_HEADS * HG_DK, HG_HEADS * HG_DK, HG_HEADS * HG_DV,
             HG_HEADS * HG_DV, 3 * d)
    offs = np.concatenate([[0], np.cumsum(sizes)])
    col = lambda i: w_in[:, offs[i]:offs[i + 1]]
    qk_w = MLA_NOPE + MLA_ROPE
    uq = p["mla_w_uq"][l].reshape(MLA_Q_RANK, MLA_HEADS, qk_w)
    uq = jnp.pad(uq, ((0, 0), (0, 0), (0, HEAD_PAD - qk_w))).reshape(MLA_Q_RANK, MLA_HEADS * HEAD_PAD)
    ukv = p["mla_w_ukv"][l].reshape(MLA_KV_RANK, MLA_HEADS, MLA_NOPE + MLA_V)
    uk = jnp.pad(ukv[:, :, :MLA_NOPE], ((0, 0), (0, 0), (0, HEAD_PAD - MLA_NOPE)))
    uk = uk.reshape(MLA_KV_RANK, MLA_HEADS * HEAD_PAD)
    uv = ukv[:, :, MLA_NOPE:].reshape(MLA_KV_RANK, MLA_HEADS * MLA_V)
    ba = jnp.concatenate([col(4), col(5)], axis=1)
    w_dn = jnp.concatenate([col(3), col(6), _pad_cols(ba, 0, LANES)], axis=1)
    w_hg = jnp.concatenate([col(7), col(8), col(9), col(10)], axis=1)
    router = jnp.concatenate([p["router_group_w"][l], p["router_expert_w"][l]], axis=1)
    router = _pad_cols(router, 0, LANES)
    r_hi = router.astype(BF16)
    r_b = jnp.concatenate([p["router_group_b"][l], p["router_expert_b"][l]])
    bf = lambda a: a.astype(BF16)
    return dict(
        wcq=bf(col(0)), wckv=bf(col(1)), wkr=bf(_pad_cols(col(2), ROPE_LO, LANES)),
        qn=p["mla_q_norm"][l].reshape(1, -1), kvn=p["mla_kv_norm"][l].reshape(1, -1),
        wuq=bf(uq), wuk=bf(uk), wuv=bf(uv),
        w_dn=bf(w_dn), w_hg=bf(w_hg), wgate=bf(col(11)),
        dn_conv=p["dn_conv"][l],
        alog128=_pad_cols(p["dn_a_log"][l].reshape(1, -1), DN_HEADS, LANES),
        dtb128=_pad_cols(p["dn_dt_bias"][l].reshape(1, -1), DN_HEADS, LANES),
        dn_onorm=p["dn_o_norm"][l].reshape(1, -1), hg_onorm=p["hg_o_norm"][l].reshape(1, -1),
        wbr_a=bf(p["w_br_a"][l]), wbr_b=bf(p["w_br_b"][l]), wbr_c=bf(p["w_br_c"][l]),
        wout=bf(p["w_out"][l]),
        ln1_g=p["ln1_g"][l].reshape(1, d), ln1_b=p["ln1_b"][l].reshape(1, d),
        r_hi=r_hi, r_lo=bf(router - r_hi.astype(F32)), r_b=_pad_cols(r_b.reshape(1, -1), 0, LANES),
        e_gate=bf(p["exp_w_gate"][l]).reshape(N_GROUPS, EXPERTS_PER_GROUP, d, EXPERT_FF),
        e_up=bf(p["exp_w_up"][l]).reshape(N_GROUPS, EXPERTS_PER_GROUP, d, EXPERT_FF),
        e_down=bf(p["exp_w_down"][l]).reshape(N_GROUPS, EXPERTS_PER_GROUP, EXPERT_FF, d),
        ln2_g=p["ln2_g"][l].reshape(1, d), ln2_b=p["ln2_b"][l].reshape(1, d),
    )


def _forward(p):
    x = p["x"]
    batch, s, d = x.shape
    t = batch * s
    half = MLA_ROPE // 2
    inv_freq = ROPE_BASE ** (-jnp.arange(half, dtype=F32) / half)
    inv128 = _pad_cols(jnp.concatenate([inv_freq, inv_freq]).reshape(1, -1), ROPE_LO, LANES)
    tables = _rope_call(p["positions"].reshape(t, 1), inv128)
    h = _ln_call(x.reshape(t, d), p["ln_in_g"], p["ln_in_b"])
    for l in range(DEPTH):
        w = _layer_weights(l, p)
        q, k, v = _mla_proj_call(h, t, w, tables)
        oa = _flash_call(q, k, v, batch)
        x_dn = _proj_call(h, t, w["w_dn"], "proj_deltanet")
        ob = _dn_call(x_dn.reshape(batch, s, -1), w["dn_conv"], w["alog128"], w["dtb128"],
                      w["dn_onorm"])
        x_hg = _proj_call(h, t, w["w_hg"], "proj_hgrn2")
        oc = _hg_call(x_hg.reshape(batch, s, -1), p["hg_lower_bounds"], w["hg_onorm"], l)
        ext, info = _merge_call(h, oa, ob.reshape(t, -1), oc.reshape(t, -1), w)
        h = _moe_call(ext, info, w)
    return h[:t].reshape(batch, s, d)


def kernel(x, positions, ln_in_g, ln_in_b, hg_lower_bounds, w_in, mla_q_norm, mla_w_uq, mla_kv_norm, mla_w_ukv, dn_conv, dn_a_log, dn_dt_bias, dn_o_norm, hg_o_norm, w_br_a, w_br_b, w_br_c, w_out, ln1_g, ln1_b, router_group_w, router_group_b, router_expert_w, router_expert_b, exp_w_gate, exp_w_up, exp_w_down, ln2_g, ln2_b):
    return _forward(dict(
        x=x, positions=positions, ln_in_g=ln_in_g, ln_in_b=ln_in_b,
        hg_lower_bounds=hg_lower_bounds, w_in=w_in, mla_q_norm=mla_q_norm, mla_w_uq=mla_w_uq,
        mla_kv_norm=mla_kv_norm, mla_w_ukv=mla_w_ukv, dn_conv=dn_conv, dn_a_log=dn_a_log,
        dn_dt_bias=dn_dt_bias, dn_o_norm=dn_o_norm, hg_o_norm=hg_o_norm, w_br_a=w_br_a,
        w_br_b=w_br_b, w_br_c=w_br_c, w_out=w_out, ln1_g=ln1_g, ln1_b=ln1_b,
        router_group_w=router_group_w, router_group_b=router_group_b,
        router_expert_w=router_expert_w, router_expert_b=router_expert_b,
        exp_w_gate=exp_w_gate, exp_w_up=exp_w_up, exp_w_down=exp_w_down, ln2_g=ln2_g, ln2_b=ln2_b))
```

```python
import functools
import math

import jax
import jax.numpy as jnp
import numpy as np
from jax import lax
from jax.experimental import pallas as pl
from jax.experimental.pallas import tpu as pltpu

D_MODEL = 1024
DEPTH = 2
MLA_HEADS = 8
MLA_Q_RANK = 256
MLA_KV_RANK = 128
MLA_NOPE = 64
MLA_ROPE = 32
MLA_V = 64
ROPE_BASE = 10000.0
MASK_VALUE = -1e30
DN_HEADS = 4
DN_DK = 128
DN_DV = 128
DN_CONV = 4
HG_HEADS = 4
HG_DK = 128
HG_DV = 128
CHUNK = 64
MIN_FORGET = 1e-30
N_GROUPS = 4
EXPERTS_PER_GROUP = 8
N_EXPERTS = N_GROUPS * EXPERTS_PER_GROUP
EXPERT_FF = 256
DEEPNORM_ALPHA = (2 * DEPTH) ** 0.25
NORM_EPS = 1e-6

LANES = 128
HEAD_PAD = 128
ROPE_LO = MLA_NOPE
ROPE_MID = MLA_NOPE + MLA_ROPE // 2
ROPE_HI = MLA_NOPE + MLA_ROPE
SCAN_ROWS = 256
ATTN_TILE = 512
EXT_COLS = D_MODEL + LANES
MOE_TILE = 256
VMEM_LIMIT = 56 * 1024 * 1024

BF16 = jnp.bfloat16
F32 = jnp.float32


def _dot(a, b):
    return jnp.dot(a.astype(BF16), b.astype(BF16), preferred_element_type=F32)


def _dot_nt(a, b):
    return lax.dot_general(a.astype(BF16), b.astype(BF16), (((1,), (1,)), ((), ())),
                           preferred_element_type=F32)


def _dot_tn(a, b):
    return lax.dot_general(a.astype(BF16), b.astype(BF16), (((0,), (0,)), ((), ())),
                           preferred_element_type=F32)


def _sigmoid(x):
    return 1.0 / (1.0 + jnp.exp(-x))


def _silu(x):
    return x * _sigmoid(x)


def _layer_norm(x, g, b):
    mu = jnp.mean(x, axis=-1, keepdims=True)
    xc = x - mu
    var = jnp.mean(xc * xc, axis=-1, keepdims=True)
    return xc * lax.rsqrt(var + NORM_EPS) * g + b


def _rms_norm(x, g):
    return x * lax.rsqrt(jnp.mean(x * x, axis=-1, keepdims=True) + NORM_EPS) * g


def _params(sem):
    return pltpu.CompilerParams(dimension_semantics=sem, vmem_limit_bytes=VMEM_LIMIT)


def _full(shape):
    n = len(shape)
    return pl.BlockSpec(shape, lambda *_: (0,) * n)


def _ln_kernel(x_ref, g_ref, b_ref, o_ref):
    o_ref[...] = _layer_norm(x_ref[...], g_ref[...], b_ref[...])


def _ln_call(x, g, b, tm=1024):
    t, d = x.shape
    return pl.pallas_call(
        _ln_kernel,
        out_shape=jax.ShapeDtypeStruct((t, d), F32),
        grid=(t // tm,),
        in_specs=[pl.BlockSpec((tm, d), lambda i: (i, 0)), _full((1, d)), _full((1, d))],
        out_specs=pl.BlockSpec((tm, d), lambda i: (i, 0)),
        compiler_params=_params(("parallel",)),
        name="ln_in",
    )(x, g.reshape(1, d), b.reshape(1, d))


def _rope_kernel(pos_ref, inv_ref, c_ref, sa_ref, sb_ref):
    ang = pos_ref[...].astype(F32) * inv_ref[...]
    lane = lax.broadcasted_iota(jnp.int32, ang.shape, 1)
    sin = jnp.sin(ang)
    c_ref[...] = jnp.cos(ang)
    sa_ref[...] = jnp.where((lane >= ROPE_MID) & (lane < ROPE_HI), sin, 0.0)
    sb_ref[...] = jnp.where((lane >= ROPE_LO) & (lane < ROPE_MID), -sin, 0.0)


def _rope_call(pos_col, inv128, tm=1024):
    t = pos_col.shape[0]
    spec = pl.BlockSpec((tm, LANES), lambda i: (i, 0))
    shp = jax.ShapeDtypeStruct((t, LANES), F32)
    return pl.pallas_call(
        _rope_kernel,
        out_shape=(shp, shp, shp),
        grid=(t // tm,),
        in_specs=[pl.BlockSpec((tm, 1), lambda i: (i, 0)), _full((1, LANES))],
        out_specs=(spec, spec, spec),
        compiler_params=_params(("parallel",)),
        name="rope_tables",
    )(pos_col, inv128)


def _rope_apply(x, c, sa, sb):
    return x * c + pltpu.roll(x, 16, 1) * sa + pltpu.roll(x, LANES - 16, 1) * sb


def _mla_proj_kernel(h_ref, wcq_ref, wckv_ref, wkr_ref, qn_ref, kvn_ref, wuq_ref, wuk_ref,
                     wuv_ref, c_ref, sa_ref, sb_ref, qt_ref, k_ref, vt_ref):
    hb = h_ref[...].astype(BF16)
    cq = _rms_norm(_dot(hb, wcq_ref[...]), qn_ref[...])
    ckv = _rms_norm(_dot(hb, wckv_ref[...]), kvn_ref[...])
    kr = _dot(hb, wkr_ref[...])
    q = _dot(cq, wuq_ref[...])
    k = _dot(ckv, wuk_ref[...])
    v = _dot(ckv, wuv_ref[...])
    for pr in range(MLA_HEADS // 2):
        sl = slice(pr * 2 * MLA_V, (pr + 1) * 2 * MLA_V)
        vt_ref[0, sl, :] = v[:, sl].T.astype(BF16)
    c, sa, sb = c_ref[...], sa_ref[...], sb_ref[...]
    scale = (MLA_NOPE + MLA_ROPE) ** -0.5 * math.log2(math.e)
    kr = _rope_apply(kr, c, sa, sb)
    for hd in range(MLA_HEADS):
        sl = slice(hd * HEAD_PAD, (hd + 1) * HEAD_PAD)
        qt_ref[0, sl, :] = (_rope_apply(q[:, sl], c, sa, sb) * scale).T.astype(BF16)
        k_ref[:, sl] = (k[:, sl] + kr).astype(BF16)


def _mla_proj_call(h, t, w, tables, tm=ATTN_TILE):
    d = h.shape[1]
    hq = MLA_HEADS * HEAD_PAD
    hv = MLA_HEADS * MLA_V
    row = lambda n: pl.BlockSpec((tm, n), lambda i: (i, 0))
    tile = lambda n: pl.BlockSpec((1, n, tm), lambda i: (i, 0, 0))
    return pl.pallas_call(
        _mla_proj_kernel,
        out_shape=(jax.ShapeDtypeStruct((t // tm, hq, tm), BF16),
                   jax.ShapeDtypeStruct((t, hq), BF16),
                   jax.ShapeDtypeStruct((t // tm, hv, tm), BF16)),
        grid=(t // tm,),
        in_specs=[row(d), _full((d, MLA_Q_RANK)), _full((d, MLA_KV_RANK)), _full((d, LANES)),
                  _full((1, MLA_Q_RANK)), _full((1, MLA_KV_RANK)), _full((MLA_Q_RANK, hq)),
                  _full((MLA_KV_RANK, hq)), _full((MLA_KV_RANK, hv)),
                  row(LANES), row(LANES), row(LANES)],
        out_specs=(tile(hq), row(hq), tile(hv)),
        compiler_params=_params(("parallel",)),
        name="mla_proj",
    )(h, w["wcq"], w["wckv"], w["wkr"], w["qn"], w["kvn"], w["wuq"], w["wuk"], w["wuv"], *tables)


def _flash_kernel(qt_ref, k_ref, vt_ref, o_ref, m0, l0, a0, m1, l1, a1, sa_ref, sb_ref,
                  *, tk):
    qi = pl.program_id(2)
    stats = ((m0, l0, a0), (m1, l1, a1))
    for m_ref, l_ref, acc_ref in stats:
        m_ref[...] = jnp.full(m_ref.shape, MASK_VALUE, F32)
        l_ref[...] = jnp.zeros(l_ref.shape, F32)
        acc_ref[...] = jnp.zeros(acc_ref.shape, F32)

    def score(j, s_ref, lo):
        rows = pl.ds(pl.multiple_of(j * tk, tk), tk)
        for hh in range(2):
            sl = slice(hh * HEAD_PAD, (hh + 1) * HEAD_PAD)
            qs = [qt_ref[0, u, sl, :] for u in range(lo // tk, 2)]
            qh = qs[0] if len(qs) == 1 else jnp.concatenate(qs, axis=1)
            s_ref[hh, :, lo:] = jnp.dot(k_ref[0, rows, sl], qh, preferred_element_type=F32)

    def consume(j, s_ref, lo, masked):
        vt = vt_ref[0, j, 0]
        for hh, (m_ref, l_ref, acc_ref) in enumerate(stats):
            st = s_ref[hh, :, lo:]
            if masked:
                key = lax.broadcasted_iota(jnp.int32, st.shape, 0)
                qry = lax.broadcasted_iota(jnp.int32, st.shape, 1)
                st = jnp.where(key <= qry, st, MASK_VALUE)
            m_old = m_ref[:, lo:]
            m_new = jnp.maximum(m_old, jnp.max(st, axis=0, keepdims=True))
            alpha = jnp.exp2(m_old - m_new)
            p = jnp.exp2(st - m_new)
            l_ref[:, lo:] = alpha * l_ref[:, lo:] + jnp.sum(p, axis=0, keepdims=True)
            acc_ref[:, lo:] = alpha * acc_ref[:, lo:] + jnp.dot(vt, p.astype(BF16),
                                                                preferred_element_type=F32)
            m_ref[:, lo:] = m_new

    score(0, sa_ref, 0)

    def body(i, carry):
        j = 2 * i
        score(j + 1, sb_ref, 0)
        consume(j, sa_ref, 0, False)
        score(j + 2, sa_ref, 0)
        consume(j + 1, sb_ref, 0, False)
        return carry

    lax.fori_loop(0, qi, body, 0)
    score(2 * qi + 1, sb_ref, tk)
    consume(2 * qi, sa_ref, 0, True)
    consume(2 * qi + 1, sb_ref, tk, True)
    row = lax.broadcasted_iota(jnp.int32, a0.shape, 0)
    ot = jnp.where(row < MLA_V, a0[...] * (1.0 / l0[...]), a1[...] * (1.0 / l1[...]))
    o_ref[0] = ot.T.astype(BF16)


def _flash_call(qt, k, vt, batch, tk=ATTN_TILE):
    nt, hq, _ = qt.shape
    t = nt * tk
    s = t // batch
    ns = s // tk
    tq = 2 * tk
    pairs = MLA_HEADS // 2
    hv = MLA_HEADS * MLA_V
    qt5 = qt.reshape(batch, ns, hq, tk)
    k3 = k.reshape(batch, s, hq)
    vt5 = vt.reshape(batch, ns, pairs, 2 * MLA_V, tk)
    stat = lambda: pltpu.VMEM((1, tq), F32)
    acc = lambda: pltpu.VMEM((2 * MLA_V, tq), F32)
    out = pl.pallas_call(
        functools.partial(_flash_kernel, tk=tk),
        out_shape=jax.ShapeDtypeStruct((batch, s, hv), BF16),
        grid=(batch, pairs, s // tq),
        in_specs=[pl.BlockSpec((1, 2, 2 * HEAD_PAD, tk), lambda b, p, i: (b, i, p, 0)),
                  pl.BlockSpec((1, s, 2 * HEAD_PAD), lambda b, p, i: (b, 0, p)),
                  pl.BlockSpec((1, ns, 1, 2 * MLA_V, tk), lambda b, p, i: (b, 0, p, 0, 0))],
        out_specs=pl.BlockSpec((1, tq, 2 * MLA_V), lambda b, p, i: (b, i, p)),
        scratch_shapes=[stat(), stat(), acc(), stat(), stat(), acc(),
                        pltpu.VMEM((2, tk, tq), F32), pltpu.VMEM((2, tk, tq), F32)],
        compiler_params=_params(("parallel", "parallel", "arbitrary")),
        name="flash_attn",
    )(qt5, k3, vt5)
    return out.reshape(t, hv)


def _proj_kernel(h_ref, w_ref, o_ref):
    o_ref[...] = _dot(h_ref[...], w_ref[...])


def _proj_call(h, t, w, name, tm=512):
    d = h.shape[1]
    n = w.shape[1]
    return pl.pallas_call(
        _proj_kernel,
        out_shape=jax.ShapeDtypeStruct((t, n), F32),
        grid=(t // tm,),
        in_specs=[pl.BlockSpec((tm, d), lambda i: (i, 0)), _full((d, n))],
        out_specs=pl.BlockSpec((tm, n), lambda i: (i, 0)),
        compiler_params=_params(("parallel",)),
        name=name,
    )(h, w)


def _chunk_cumsum(x):
    rows = lax.broadcasted_iota(jnp.int32, x.shape, 0) % CHUNK
    shift = 1
    while shift < CHUNK:
        x = x + jnp.where(rows >= shift, pltpu.roll(x, shift, 0), 0.0)
        shift *= 2
    return x


def _chunk_last(x):
    r, n = x.shape
    x3 = x.reshape(r // CHUNK, CHUNK, n)
    return jnp.broadcast_to(x3[:, CHUNK - 1:CHUNK, :], x3.shape).reshape(r, n)


def _chunk_masks(r):
    i = lax.broadcasted_iota(jnp.int32, (r, r), 0)
    j = lax.broadcasted_iota(jnp.int32, (r, r), 1)
    same = (i // CHUNK) == (j // CHUNK)
    return same & (j <= i), same & (j < i), i == j


def _dn_solve(heads, eye):
    invs = [eye - p["m"] for p in heads]
    pws = [p["m"].astype(BF16) for p in heads]
    for _ in range(5):
        pws = [jnp.dot(pw, pw, preferred_element_type=F32).astype(BF16) for pw in pws]
        invs = [inv + jnp.dot(inv.astype(BF16), pw, preferred_element_type=F32)
                for inv, pw in zip(invs, pws)]
    for p, inv in zip(heads, invs):
        uw = jnp.dot(inv.astype(BF16), p["rhs"], preferred_element_type=F32).astype(BF16)
        quw = jnp.dot(p["qk"].astype(BF16), uw, preferred_element_type=F32)
        p["o_own"] = quw[:, :DN_DV]
        q_eff = (p["qd"] - quw[:, DN_DV:]).astype(BF16)
        kd = p["kd"].astype(BF16)
        p["lhs"], p["add"] = [], []
        for c in range(uw.shape[0] // CHUNK):
            rc = slice(c * CHUNK, (c + 1) * CHUNK)
            kuw = lax.dot_general(kd[rc], uw[rc], (((0,), (0,)), ((), ())),
                                  preferred_element_type=F32)
            p["add"].append(kuw[:, :DN_DV])
            p["lhs"].append(jnp.concatenate([kuw[:, DN_DV:].astype(BF16), q_eff[rc]], axis=0))


def _dn_kernel(x_ref, conv_ref, alog_ref, dtb_ref, onorm_ref, o_ref, halo_ref, state_ref,
               *, batch):
    r = SCAN_ROWS
    nqkv = DN_HEADS * (2 * DN_DK + DN_DV)
    gate_off = nqkv
    ba_off = nqkv + DN_HEADS * DN_DV

    @pl.when(pl.program_id(0) == 0)
    def _():
        halo_ref[...] = jnp.zeros(halo_ref.shape, F32)
        state_ref[...] = jnp.zeros(state_ref.shape, F32)

    incl, strict, diag = _chunk_masks(r)
    eye = jnp.where(diag, 1.0, 0.0)
    cw = conv_ref[...]
    pre = []
    for b in range(batch):
        xq = x_ref[b, :, :nqkv]
        xe = jnp.concatenate([halo_ref[b], xq], axis=0)
        halo_ref[b] = xq[r - 8:, :]
        y = xe[8:, :] * cw[DN_CONV - 1:DN_CONV, :]
        for i in range(DN_CONV - 1):
            y = y + pltpu.roll(xe, DN_CONV - 1 - i, 0)[8:, :] * cw[i:i + 1, :]
        y = _silu(y)
        ba = x_ref[b, :, ba_off:ba_off + LANES]
        beta_all = _sigmoid(ba)
        av = ba + dtb_ref[...]
        softplus = jnp.maximum(av, 0.0) + jnp.log(1.0 + jnp.exp(-jnp.abs(av)))
        g_all = _chunk_cumsum(-jnp.exp(alog_ref[...]) * softplus)
        g_all_t = g_all.T
        g_last_all = _chunk_last(g_all)
        for hd in range(DN_HEADS):
            q = y[:, hd * DN_DK:(hd + 1) * DN_DK]
            k = y[:, DN_HEADS * DN_DK + hd * DN_DK:DN_HEADS * DN_DK + (hd + 1) * DN_DK]
            v = y[:, 2 * DN_HEADS * DN_DK + hd * DN_DV:2 * DN_HEADS * DN_DK + (hd + 1) * DN_DV]
            q = q * lax.rsqrt(jnp.sum(q * q, axis=-1, keepdims=True) + NORM_EPS) * DN_DK ** -0.5
            k = k * lax.rsqrt(jnp.sum(k * k, axis=-1, keepdims=True) + NORM_EPS)
            beta = beta_all[:, hd:hd + 1]
            g_col = g_all[:, DN_HEADS + hd:DN_HEADS + hd + 1]
            g_row = g_all_t[DN_HEADS + hd:DN_HEADS + hd + 1, :]
            g_last = g_last_all[:, DN_HEADS + hd:DN_HEADS + hd + 1]
            decay = jnp.where(incl, jnp.exp(jnp.where(incl, g_col - g_row, 0.0)), 0.0)
            kb = k * beta
            kk = _dot_nt(kb, k)
            m = jnp.where(strict, kk * decay, 0.0)
            qk = jnp.where(incl, _dot_nt(q, k) * decay, 0.0)
            eg = jnp.exp(g_col)
            pre.append(dict(m=m, rhs=jnp.concatenate([v * beta, kb * eg], axis=1).astype(BF16),
                            qd=q * eg, kd=k * jnp.exp(g_last - g_col), qk=qk,
                            gl=jnp.exp(g_last)))
        _dn_solve(pre[b * DN_HEADS:], eye)
    outs = [[] for _ in pre]
    for c in range(r // CHUNK):
        rc = slice(c * CHUNK, (c + 1) * CHUNK)
        for idx, p in enumerate(pre):
            st = state_ref[idx]
            prod = jnp.dot(p["lhs"][c], st.astype(BF16), preferred_element_type=F32)
            outs[idx].append(prod[DN_DK:] + p["o_own"][rc])
            state_ref[idx] = (st * p["gl"][c * CHUNK:c * CHUNK + 1, :] - prod[:DN_DK]
                              + p["add"][c])
    for idx in range(len(pre)):
        b, hd = divmod(idx, DN_HEADS)
        o = jnp.concatenate(outs[idx], axis=0)
        gate = x_ref[b, :, gate_off + hd * DN_DV:gate_off + (hd + 1) * DN_DV]
        o_ref[b, :, hd * DN_DV:(hd + 1) * DN_DV] = (
            _rms_norm(o, onorm_ref[...]) * _silu(gate)).astype(BF16)


def _dn_call(x3, conv_w, alog128, dtb128, onorm):
    batch, s, n = x3.shape
    r = SCAN_ROWS
    nqkv = DN_HEADS * (2 * DN_DK + DN_DV)
    width = DN_HEADS * DN_DV
    return pl.pallas_call(
        functools.partial(_dn_kernel, batch=batch),
        out_shape=jax.ShapeDtypeStruct((batch, s, width), BF16),
        grid=(s // r,),
        in_specs=[pl.BlockSpec((batch, r, n), lambda i: (0, i, 0)), _full((DN_CONV, nqkv)),
                  _full((1, LANES)), _full((1, LANES)), _full((1, DN_DV))],
        out_specs=pl.BlockSpec((batch, r, width), lambda i: (0, i, 0)),
        scratch_shapes=[pltpu.VMEM((batch, 8, nqkv), F32),
                        pltpu.VMEM((batch * DN_HEADS, DN_DK, DN_DV), F32)],
        compiler_params=_params(("arbitrary",)),
        name="deltanet",
    )(x3, conv_w, alog128, dtb128, onorm)


HG_LEVELS = (32, 16, 8, 4, 2, 1)


def _hg_kernel(x_ref, lbp_ref, onorm_ref, sel_ref, o_ref, state_ref, *, batch, layer):
    r = SCAN_ROWS
    width = HG_HEADS * HG_DK

    @pl.when(pl.program_id(0) == 0)
    def _():
        state_ref[...] = jnp.zeros(state_ref.shape, F32)

    lbp = lbp_ref[...]
    e = jnp.exp(lbp - jnp.max(lbp, axis=0, keepdims=True))
    soft = e / jnp.sum(e, axis=0, keepdims=True)
    lb_all = jnp.zeros((1, width), F32)
    for i in range(1, layer + 1):
        lb_all = lb_all + soft[i:i + 1, :]

    i_idx = lax.broadcasted_iota(jnp.int32, (r, r), 0)
    j_idx = lax.broadcasted_iota(jnp.int32, (r, r), 1)
    level_masks = [((i_idx // (2 * s)) == (j_idx // (2 * s))) & ((i_idx % (2 * s)) >= s)
                   & ((j_idx % (2 * s)) < s) for s in HG_LEVELS]
    diag = i_idx == j_idx

    for b in range(batch):
        for hd in range(HG_HEADS):
            idx = b * HG_HEADS + hd
            hs = slice(hd * HG_DK, (hd + 1) * HG_DK)
            lb = lb_all[:, hs]
            q = _silu(x_ref[b, :, hd * HG_DK:(hd + 1) * HG_DK])
            z = x_ref[b, :, width + hd * HG_DK:width + (hd + 1) * HG_DK]
            v = x_ref[b, :, 2 * width + hd * HG_DV:2 * width + (hd + 1) * HG_DV]
            gate = x_ref[b, :, 3 * width + hd * HG_DV:3 * width + (hd + 1) * HG_DV]
            f = lb + (1.0 - lb) * _sigmoid(z)
            k = (1.0 - lb) * _sigmoid(-z)
            cum = _chunk_cumsum(jnp.log(jnp.maximum(f, MIN_FORGET)))
            cum_last = _chunk_last(cum)
            qd = q * jnp.exp(cum)
            kd = k * jnp.exp(cum_last - cum)
            dl = jnp.exp(cum_last)
            att = jnp.where(diag, _dot_nt(q, k), 0.0)
            cum_b = cum.astype(BF16)
            for lv, mask in enumerate(level_masks):
                ref = jnp.dot(sel_ref[lv], cum_b, preferred_element_type=F32)
                qs = q * jnp.exp(jnp.minimum(cum - ref, 60.0))
                ks = k * jnp.exp(jnp.minimum(ref - cum, 60.0))
                att = att + jnp.where(mask, _dot_nt(qs, ks), 0.0)
            o_intra = _dot(att, v)
            outs = []
            st = state_ref[idx]
            for c in range(r // CHUNK):
                rc = slice(c * CHUNK, (c + 1) * CHUNK)
                outs.append(_dot_nt(qd[rc], st) + o_intra[rc])
                st = st * dl[c * CHUNK:c * CHUNK + 1, :] + _dot_tn(v[rc], kd[rc])
            state_ref[idx] = st
            o = jnp.concatenate(outs, axis=0)
            o_ref[b, :, hs] = (_rms_norm(o, onorm_ref[...]) * _silu(gate)).astype(BF16)


def _hg_sel():
    r = SCAN_ROWS
    i = np.arange(r)
    sel = np.zeros((len(HG_LEVELS), r, r), np.float32)
    for lv, s in enumerate(HG_LEVELS):
        sel[lv, i, (i // (2 * s)) * (2 * s) + s] = 1.0
    return jnp.asarray(sel, BF16)


def _hg_call(x3, lbp, onorm, layer):
    batch, s, n = x3.shape
    r = SCAN_ROWS
    width = HG_HEADS * HG_DV
    return pl.pallas_call(
        functools.partial(_hg_kernel, batch=batch, layer=layer),
        out_shape=jax.ShapeDtypeStruct((batch, s, width), BF16),
        grid=(s // r,),
        in_specs=[pl.BlockSpec((batch, r, n), lambda i: (0, i, 0)), _full(lbp.shape),
                  _full((1, HG_DV)), _full((len(HG_LEVELS), r, r))],
        out_specs=pl.BlockSpec((batch, r, width), lambda i: (0, i, 0)),
        scratch_shapes=[pltpu.VMEM((batch * HG_HEADS, HG_DV, HG_DK), F32)],
        compiler_params=_params(("arbitrary",)),
        name="hgrn2",
    )(x3, lbp, onorm, _hg_sel())


def _merge_kernel(h_ref, oa_ref, ob_ref, oc_ref, wgate_ref, wa_ref, wb_ref, wc_ref, wout_ref,
                  g_ref, b_ref, rhi_ref, rlo_ref, rb_ref, ext_ref, info_ref):
    h = h_ref[...]
    gates = _sigmoid(_dot(h, wgate_ref[...]))
    d = D_MODEL
    mixed = (gates[:, :d] * _dot(oa_ref[...], wa_ref[...])
             + gates[:, d:2 * d] * _dot(ob_ref[...], wb_ref[...])
             + gates[:, 2 * d:] * _dot(oc_ref[...], wc_ref[...]))
    h1 = _layer_norm(DEEPNORM_ALPHA * h + _dot(mixed, wout_ref[...]), g_ref[...], b_ref[...])
    hi = h1.astype(BF16)
    lo = (h1 - hi.astype(F32)).astype(BF16)
    logits = (jnp.dot(hi, rhi_ref[...], preferred_element_type=F32)
              + jnp.dot(hi, rlo_ref[...], preferred_element_type=F32)
              + jnp.dot(lo, rhi_ref[...], preferred_element_type=F32)) + rb_ref[...]
    lane = lax.broadcasted_iota(jnp.int32, logits.shape, 1)
    neg = jnp.float32(-jnp.inf)
    big = jnp.int32(1 << 20)
    is_g = lane < N_GROUPS
    gl = jnp.where(is_g, logits, neg)
    gmax = jnp.max(gl, axis=-1, keepdims=True)
    gidx = jnp.min(jnp.where(gl == gmax, lane, big), axis=-1, keepdims=True)
    p_group = 1.0 / jnp.sum(jnp.where(is_g, jnp.exp(gl - gmax), 0.0), axis=-1, keepdims=True)
    lo_lane = N_GROUPS + gidx * EXPERTS_PER_GROUP
    in_g = (lane >= lo_lane) & (lane < lo_lane + EXPERTS_PER_GROUP)
    el = jnp.where(in_g, logits, neg)
    v1 = jnp.max(el, axis=-1, keepdims=True)
    i1 = jnp.min(jnp.where(el == v1, lane, big), axis=-1, keepdims=True)
    el2 = jnp.where(lane == i1, neg, el)
    v2 = jnp.max(el2, axis=-1, keepdims=True)
    i2 = jnp.min(jnp.where(el2 == v2, lane, big), axis=-1, keepdims=True)
    e21 = jnp.exp(v2 - v1)
    w1 = p_group / (1.0 + e21)
    w2 = p_group * e21 / (1.0 + e21)
    info = (jnp.where(lane == i1 - lo_lane, w1, 0.0) + jnp.where(lane == i2 - lo_lane, w2, 0.0)
            + jnp.where(lane == EXPERTS_PER_GROUP, gidx.astype(F32), 0.0))
    info_ref[...] = info
    ext_ref[:, :D_MODEL] = h1
    ext_ref[:, D_MODEL:] = info


def _merge_call(h, oa, ob, oc, w, tm=512):
    t, d = oa.shape[0], h.shape[1]
    row = lambda n: pl.BlockSpec((tm, n), lambda i: (i, 0))
    wa, wb, wc = w["wbr_a"], w["wbr_b"], w["wbr_c"]
    return pl.pallas_call(
        _merge_kernel,
        out_shape=(jax.ShapeDtypeStruct((t, EXT_COLS), F32),
                   jax.ShapeDtypeStruct((t, LANES), F32)),
        grid=(t // tm,),
        in_specs=[row(d), row(oa.shape[1]), row(ob.shape[1]), row(oc.shape[1]),
                  _full((d, 3 * d)), _full(wa.shape), _full(wb.shape), _full(wc.shape),
                  _full((d, d)), _full((1, d)), _full((1, d)),
                  _full((d, LANES)), _full((d, LANES)), _full((1, LANES))],
        out_specs=(row(EXT_COLS), row(LANES)),
        compiler_params=_params(("parallel",)),
        name="merge_ln_router",
    )(h, oa, ob, oc, w["wgate"], wa, wb, wc, w["wout"], w["ln1_g"], w["ln1_b"],
      w["r_hi"], w["r_lo"], w["r_b"])


def _rank_kernel(info_ref, route_ref, cnt_ref, base_ref):
    @pl.when(pl.program_id(0) == 0)
    def _():
        base_ref[...] = jnp.zeros(base_ref.shape, F32)

    info = info_ref[...]
    tb = info.shape[0]
    lane = lax.broadcasted_iota(jnp.int32, info.shape, 1)
    gid = info[:, EXPERTS_PER_GROUP:EXPERTS_PER_GROUP + 1]
    onehot = jnp.where((lane.astype(F32) == gid) & (lane < N_GROUPS), 1.0, 0.0)
    r = lax.broadcasted_iota(jnp.int32, (tb, tb), 0)
    c = lax.broadcasted_iota(jnp.int32, (tb, tb), 1)
    earlier = jnp.where(c < r, 1.0, 0.0)
    prefix = _dot(earlier, onehot)
    base = base_ref[...]
    rank = jnp.sum(onehot * (prefix + base), axis=1, keepdims=True)
    route_ref[...] = jnp.where(lane == 0, rank, jnp.where(lane == 1, gid, 0.0))
    base = base + jnp.sum(onehot, axis=0, keepdims=True)
    base_ref[...] = base
    cnt_ref[...] = base


def _rank_call(info, tb=512):
    t = info.shape[0]
    return pl.pallas_call(
        _rank_kernel,
        out_shape=(jax.ShapeDtypeStruct((t, LANES), F32), jax.ShapeDtypeStruct((1, LANES), F32)),
        grid=(t // tb,),
        in_specs=[pl.BlockSpec((tb, LANES), lambda i: (i, 0))],
        out_specs=(pl.BlockSpec((tb, LANES), lambda i: (i, 0)),
                   pl.BlockSpec((1, LANES), lambda i: (0, 0))),
        scratch_shapes=[pltpu.VMEM((1, LANES), F32)],
        compiler_params=_params(("arbitrary",)),
        name="moe_rank",
    )(info)


def _dispatch_plan(route, cnt, t, tm):
    n_tiles = t // tm + N_GROUPS
    rows = n_tiles * tm
    rank = route[:, 0].astype(jnp.int32)
    gid = route[:, 1].astype(jnp.int32)
    counts = cnt[0, :N_GROUPS].astype(jnp.int32)
    padded = (counts + tm - 1) // tm * tm
    ends = jnp.cumsum(padded)
    pos = (ends - padded)[gid] + rank
    tok = jnp.arange(t, dtype=jnp.int32)
    row = jnp.arange(rows, dtype=jnp.int32)
    owner = jnp.full((rows,), -1, jnp.int32).at[pos].set(tok, unique_indices=True,
                                                          mode="promise_in_bounds")
    src = jnp.maximum(owner, 0)
    dst = jnp.where(owner < 0, t + row % tm, owner)
    dst = jnp.concatenate([t + row[:tm], dst])
    tile_start = jnp.arange(n_tiles, dtype=jnp.int32) * tm
    tile_group = jnp.minimum(jnp.sum(tile_start[:, None] >= ends[None, :], axis=1), N_GROUPS - 1)
    return src, dst, tile_group.astype(jnp.int32), n_tiles


def _expert_kernel(src_ref, dst_ref, tg_ref, ext_hbm, wg_ref, wu_ref, wd_ref, g_ref, b_ref, out_hbm,
                   xa, xb, oa, ob, gsem, ssem, *, tm, n_tiles):
    i = pl.program_id(0)
    bufs = ((xa, oa, 0), (xb, ob, 1))

    def gather_copy(tile, buf, r):
        return pltpu.make_async_copy(ext_hbm.at[pl.ds(src_ref[tile * tm + r], 1)],
                                     buf[0].at[pl.ds(r, 1)], gsem.at[buf[2]])

    def scatter_copy(tile, buf, r):
        return pltpu.make_async_copy(buf[1].at[pl.ds(r, 1)],
                                     out_hbm.at[pl.ds(dst_ref[(tile + 1) * tm + r], 1)],
                                     ssem.at[buf[2]])

    def wait_gather(buf):
        pltpu.make_async_copy(ext_hbm.at[pl.ds(0, tm)], buf[0], gsem.at[buf[2]]).wait()

    def wait_scatter(buf):
        pltpu.make_async_copy(buf[1], out_hbm.at[pl.ds(0, tm)], ssem.at[buf[2]]).wait()

    @pl.when(i == 0)
    def _():
        ob[...] = jnp.zeros(ob.shape, F32)
        for r in range(tm):
            gather_copy(0, bufs[0], r).start()

    def tile_body(cur, oth):
        wait_gather(cur)

        @pl.when(i >= 1)
        def _():
            wait_scatter(cur)

        x = cur[0][:, :D_MODEL]
        info = cur[0][:, D_MODEL:]
        xb_ = x.astype(BF16)
        acc = jnp.zeros((tm, D_MODEL), F32)
        per = tm // EXPERTS_PER_GROUP
        nxt = jnp.minimum(i + 1, n_tiles - 1)
        for e in range(EXPERTS_PER_GROUP):
            act = (_silu(jnp.dot(xb_, wg_ref[0, e], preferred_element_type=F32))
                   * jnp.dot(xb_, wu_ref[0, e], preferred_element_type=F32) * info[:, e:e + 1])
            acc = acc + _dot(act, wd_ref[0, e])
            half = EXPERTS_PER_GROUP // 2
            for r in range((e % half) * 2 * per, (e % half + 1) * 2 * per):
                if e < half:
                    gather_copy(nxt, oth, r).start()
                else:
                    scatter_copy(i - 1, oth, r).start()
        cur[1][...] = _layer_norm(DEEPNORM_ALPHA * x + acc, g_ref[...], b_ref[...])

        @pl.when(i == n_tiles - 1)
        def _():
            wait_scatter(oth)
            for r in range(tm):
                scatter_copy(i, cur, r).start()
            wait_scatter(cur)
            wait_gather(oth)

    @pl.when(i % 2 == 0)
    def _():
        tile_body(bufs[0], bufs[1])

    @pl.when(i % 2 == 1)
    def _():
        tile_body(bufs[1], bufs[0])


def _expert_call(ext, plan, w, tm):
    src, dst, tile_group, n_tiles = plan
    t = ext.shape[0]
    d = D_MODEL
    chunks = d // LANES
    wspec = lambda shape: pl.BlockSpec((1,) + shape, lambda i, s, d_, tg: (tg[i], 0, 0, 0))
    vec = pl.BlockSpec((1, d), lambda i, s, d_, tg: (0, 0))
    grid_spec = pltpu.PrefetchScalarGridSpec(
        num_scalar_prefetch=3,
        grid=(n_tiles,),
        in_specs=[pl.BlockSpec(memory_space=pl.ANY), wspec((EXPERTS_PER_GROUP, d, EXPERT_FF)),
                  wspec((EXPERTS_PER_GROUP, d, EXPERT_FF)), wspec((EXPERTS_PER_GROUP, EXPERT_FF, d)),
                  vec, vec],
        out_specs=pl.BlockSpec(memory_space=pl.ANY),
        scratch_shapes=[pltpu.VMEM((tm, EXT_COLS), F32), pltpu.VMEM((tm, EXT_COLS), F32),
                        pltpu.VMEM((tm, d), F32), pltpu.VMEM((tm, d), F32),
                        pltpu.SemaphoreType.DMA((2,)), pltpu.SemaphoreType.DMA((2,))],
    )
    out = pl.pallas_call(
        functools.partial(_expert_kernel, tm=tm, n_tiles=n_tiles),
        out_shape=jax.ShapeDtypeStruct((t + tm, d), F32),
        grid_spec=grid_spec,
        compiler_params=_params(("arbitrary",)),
        name="moe_experts",
    )(src, dst, tile_group, ext, w["e_gate"], w["e_up"], w["e_down"], w["ln2_g"], w["ln2_b"])
    return out


def _moe_call(ext, info, w, tm=MOE_TILE):
    route, cnt = _rank_call(info)
    return _expert_call(ext, _dispatch_plan(route, cnt, ext.shape[0], tm), w, tm)


def _pad_cols(w, lo, total):
    return jnp.zeros((w.shape[0], total), w.dtype).at[:, lo:lo + w.shape[1]].set(w)


def _layer_weights(l, p):
    d = D_MODEL
    w_in = p["w_in"][l]
    sizes = (MLA_Q_RANK, MLA_KV_RANK, MLA_ROPE, DN_HEADS * (2 * DN_DK + DN_DV), DN_HEADS, DN_HEADS,
             DN_HEADS * DN_DV, HG_HEADS * HG_DK, HG_HEADS * HG_DK, HG_HEADS * HG_DV,
             HG_HEADS * HG_DV, 3 * d)
    offs = np.concatenate([[0], np.cumsum(sizes)])
    col = lambda i: w_in[:, offs[i]:offs[i + 1]]
    qk_w = MLA_NOPE + MLA_ROPE
    uq = p["mla_w_uq"][l].reshape(MLA_Q_RANK, MLA_HEADS, qk_w)
    uq = jnp.pad(uq, ((0, 0), (0, 0), (0, HEAD_PAD - qk_w))).reshape(MLA_Q_RANK, MLA_HEADS * HEAD_PAD)
    ukv = p["mla_w_ukv"][l].reshape(MLA_KV_RANK, MLA_HEADS, MLA_NOPE + MLA_V)
    uk = jnp.pad(ukv[:, :, :MLA_NOPE], ((0, 0), (0, 0), (0, HEAD_PAD - MLA_NOPE)))
    uk = uk.reshape(MLA_KV_RANK, MLA_HEADS * HEAD_PAD)
    uv = ukv[:, :, MLA_NOPE:].reshape(MLA_KV_RANK, MLA_HEADS * MLA_V)
    ba = jnp.concatenate([col(4), col(5)], axis=1)
    w_dn = jnp.concatenate([col(3), col(6), _pad_cols(ba, 0, LANES)], axis=1)
    w_hg = jnp.concatenate([col(7), col(8), col(9), col(10)], axis=1)
    router = jnp.concatenate([p["router_group_w"][l], p["router_expert_w"][l]], axis=1)
    router = _pad_cols(router, 0, LANES)
    r_hi = router.astype(BF16)
    r_b = jnp.concatenate([p["router_group_b"][l], p["router_expert_b"][l]])
    bf = lambda a: a.astype(BF16)
    return dict(
        wcq=bf(col(0)), wckv=bf(col(1)), wkr=bf(_pad_cols(col(2), ROPE_LO, LANES)),
        qn=p["mla_q_norm"][l].reshape(1, -1), kvn=p["mla_kv_norm"][l].reshape(1, -1),
        wuq=bf(uq), wuk=bf(uk), wuv=bf(uv),
        w_dn=bf(w_dn), w_hg=bf(w_hg), wgate=bf(col(11)),
        dn_conv=p["dn_conv"][l],
        alog128=_pad_cols(p["dn_a_log"][l].reshape(1, -1), DN_HEADS, LANES),
        dtb128=_pad_cols(p["dn_dt_bias"][l].reshape(1, -1), DN_HEADS, LANES),
        dn_onorm=p["dn_o_norm"][l].reshape(1, -1), hg_onorm=p["hg_o_norm"][l].reshape(1, -1),
        wbr_a=bf(p["w_br_a"][l]), wbr_b=bf(p["w_br_b"][l]), wbr_c=bf(p["w_br_c"][l]),
        wout=bf(p["w_out"][l]),
        ln1_g=p["ln1_g"][l].reshape(1, d), ln1_b=p["ln1_b"][l].reshape(1, d),
        r_hi=r_hi, r_lo=bf(router - r_hi.astype(F32)), r_b=_pad_cols(r_b.reshape(1, -1), 0, LANES),
        e_gate=bf(p["exp_w_gate"][l]).reshape(N_GROUPS, EXPERTS_PER_GROUP, d, EXPERT_FF),
        e_up=bf(p["exp_w_up"][l]).reshape(N_GROUPS, EXPERTS_PER_GROUP, d, EXPERT_FF),
        e_down=bf(p["exp_w_down"][l]).reshape(N_GROUPS, EXPERTS_PER_GROUP, EXPERT_FF, d),
        ln2_g=p["ln2_g"][l].reshape(1, d), ln2_b=p["ln2_b"][l].reshape(1, d),
    )


def _forward(p):
    x = p["x"]
    batch, s, d = x.shape
    t = batch * s
    half = MLA_ROPE // 2
    inv_freq = ROPE_BASE ** (-jnp.arange(half, dtype=F32) / half)
    inv128 = _pad_cols(jnp.concatenate([inv_freq, inv_freq]).reshape(1, -1), ROPE_LO, LANES)
    tables = _rope_call(p["positions"].reshape(t, 1), inv128)
    h = _ln_call(x.reshape(t, d), p["ln_in_g"], p["ln_in_b"])
    for l in range(DEPTH):
        w = _layer_weights(l, p)
        q, k, v = _mla_proj_call(h, t, w, tables)
        oa = _flash_call(q, k, v, batch)
        x_dn = _proj_call(h, t, w["w_dn"], "proj_deltanet")
        ob = _dn_call(x_dn.reshape(batch, s, -1), w["dn_conv"], w["alog128"], w["dtb128"],
                      w["dn_onorm"])
        x_hg = _proj_call(h, t, w["w_hg"], "proj_hgrn2")
        oc = _hg_call(x_hg.reshape(batch, s, -1), p["hg_lower_bounds"], w["hg_onorm"], l)
        ext, info = _merge_call(h, oa, ob.reshape(t, -1), oc.reshape(t, -1), w)
        h = _moe_call(ext, info, w)
    return h[:t].reshape(batch, s, d)


def kernel(x, positions, ln_in_g, ln_in_b, hg_lower_bounds, w_in, mla_q_norm, mla_w_uq, mla_kv_norm, mla_w_ukv, dn_conv, dn_a_log, dn_dt_bias, dn_o_norm, hg_o_norm, w_br_a, w_br_b, w_br_c, w_out, ln1_g, ln1_b, router_group_w, router_group_b, router_expert_w, router_expert_b, exp_w_gate, exp_w_up, exp_w_down, ln2_g, ln2_b):
    return _forward(dict(
        x=x, positions=positions, ln_in_g=ln_in_g, ln_in_b=ln_in_b,
        hg_lower_bounds=hg_lower_bounds, w_in=w_in, mla_q_norm=mla_q_norm, mla_w_uq=mla_w_uq,
        mla_kv_norm=mla_kv_norm, mla_w_ukv=mla_w_ukv, dn_conv=dn_conv, dn_a_log=dn_a_log,
        dn_dt_bias=dn_dt_bias, dn_o_norm=dn_o_norm, hg_o_norm=hg_o_norm, w_br_a=w_br_a,
        w_br_b=w_br_b, w_br_c=w_br_c, w_out=w_out, ln1_g=ln1_g, ln1_b=ln1_b,
        router_group_w=router_group_w, router_group_b=router_group_b,
        router_expert_w=router_expert_w, router_expert_b=router_expert_b,
        exp_w_gate=exp_w_gate, exp_w_up=exp_w_up, exp_w_down=exp_w_down, ln2_g=ln2_g, ln2_b=ln2_b))
```

```python
import functools
import math

import jax
import jax.numpy as jnp
import numpy as np
from jax import lax
from jax.experimental import pallas as pl
from jax.experimental.pallas import tpu as pltpu

D_MODEL = 1024
DEPTH = 2
MLA_HEADS = 8
MLA_Q_RANK = 256
MLA_KV_RANK = 128
MLA_NOPE = 64
MLA_ROPE = 32
MLA_V = 64
ROPE_BASE = 10000.0
MASK_VALUE = -1e30
DN_HEADS = 4
DN_DK = 128
DN_DV = 128
DN_CONV = 4
HG_HEADS = 4
HG_DK = 128
HG_DV = 128
CHUNK = 64
MIN_FORGET = 1e-30
N_GROUPS = 4
EXPERTS_PER_GROUP = 8
N_EXPERTS = N_GROUPS * EXPERTS_PER_GROUP
EXPERT_FF = 256
DEEPNORM_ALPHA = (2 * DEPTH) ** 0.25
NORM_EPS = 1e-6

LANES = 128
HEAD_PAD = 128
ROPE_LO = MLA_NOPE
ROPE_MID = MLA_NOPE + MLA_ROPE // 2
ROPE_HI = MLA_NOPE + MLA_ROPE
SCAN_ROWS = 256
ATTN_TILE = 512
EXT_COLS = D_MODEL + LANES
MOE_TILE = 256
VMEM_LIMIT = 56 * 1024 * 1024

BF16 = jnp.bfloat16
F32 = jnp.float32


def _dot(a, b):
    return jnp.dot(a.astype(BF16), b.astype(BF16), preferred_element_type=F32)


def _dot_nt(a, b):
    return lax.dot_general(a.astype(BF16), b.astype(BF16), (((1,), (1,)), ((), ())),
                           preferred_element_type=F32)


def _dot_tn(a, b):
    return lax.dot_general(a.astype(BF16), b.astype(BF16), (((0,), (0,)), ((), ())),
                           preferred_element_type=F32)


def _sigmoid(x):
    return 1.0 / (1.0 + jnp.exp(-x))


def _silu(x):
    return x * _sigmoid(x)


def _layer_norm(x, g, b):
    mu = jnp.mean(x, axis=-1, keepdims=True)
    xc = x - mu
    var = jnp.mean(xc * xc, axis=-1, keepdims=True)
    return xc * lax.rsqrt(var + NORM_EPS) * g + b


def _rms_norm(x, g):
    return x * lax.rsqrt(jnp.mean(x * x, axis=-1, keepdims=True) + NORM_EPS) * g


def _params(sem):
    return pltpu.CompilerParams(dimension_semantics=sem, vmem_limit_bytes=VMEM_LIMIT)


def _full(shape):
    n = len(shape)
    return pl.BlockSpec(shape, lambda *_: (0,) * n)


def _ln_kernel(x_ref, g_ref, b_ref, o_ref):
    o_ref[...] = _layer_norm(x_ref[...], g_ref[...], b_ref[...])


def _ln_call(x, g, b, tm=1024):
    t, d = x.shape
    return pl.pallas_call(
        _ln_kernel,
        out_shape=jax.ShapeDtypeStruct((t, d), F32),
        grid=(t // tm,),
        in_specs=[pl.BlockSpec((tm, d), lambda i: (i, 0)), _full((1, d)), _full((1, d))],
        out_specs=pl.BlockSpec((tm, d), lambda i: (i, 0)),
        compiler_params=_params(("parallel",)),
        name="ln_in",
    )(x, g.reshape(1, d), b.reshape(1, d))


def _rope_kernel(pos_ref, inv_ref, c_ref, sa_ref, sb_ref):
    ang = pos_ref[...].astype(F32) * inv_ref[...]
    lane = lax.broadcasted_iota(jnp.int32, ang.shape, 1)
    sin = jnp.sin(ang)
    c_ref[...] = jnp.cos(ang)
    sa_ref[...] = jnp.where((lane >= ROPE_MID) & (lane < ROPE_HI), sin, 0.0)
    sb_ref[...] = jnp.where((lane >= ROPE_LO) & (lane < ROPE_MID), -sin, 0.0)


def _rope_call(pos_col, inv128, tm=1024):
    t = pos_col.shape[0]
    spec = pl.BlockSpec((tm, LANES), lambda i: (i, 0))
    shp = jax.ShapeDtypeStruct((t, LANES), F32)
    return pl.pallas_call(
        _rope_kernel,
        out_shape=(shp, shp, shp),
        grid=(t // tm,),
        in_specs=[pl.BlockSpec((tm, 1), lambda i: (i, 0)), _full((1, LANES))],
        out_specs=(spec, spec, spec),
        compiler_params=_params(("parallel",)),
        name="rope_tables",
    )(pos_col, inv128)


def _rope_apply(x, c, sa, sb):
    return x * c + pltpu.roll(x, 16, 1) * sa + pltpu.roll(x, LANES - 16, 1) * sb


def _mla_proj_kernel(h_ref, wcq_ref, wckv_ref, wkr_ref, qn_ref, kvn_ref, wuq_ref, wuk_ref,
                     wuv_ref, c_ref, sa_ref, sb_ref, qt_ref, k_ref, vt_ref):
    hb = h_ref[...].astype(BF16)
    cq = _rms_norm(_dot(hb, wcq_ref[...]), qn_ref[...])
    ckv = _rms_norm(_dot(hb, wckv_ref[...]), kvn_ref[...])
    kr = _dot(hb, wkr_ref[...])
    q = _dot(cq, wuq_ref[...])
    k = _dot(ckv, wuk_ref[...])
    v = _dot(ckv, wuv_ref[...])
    for pr in range(MLA_HEADS // 2):
        sl = slice(pr * 2 * MLA_V, (pr + 1) * 2 * MLA_V)
        vt_ref[0, sl, :] = v[:, sl].T.astype(BF16)
    c, sa, sb = c_ref[...], sa_ref[...], sb_ref[...]
    scale = (MLA_NOPE + MLA_ROPE) ** -0.5 * math.log2(math.e)
    kr = _rope_apply(kr, c, sa, sb)
    for hd in range(MLA_HEADS):
        sl = slice(hd * HEAD_PAD, (hd + 1) * HEAD_PAD)
        qt_ref[0, sl, :] = (_rope_apply(q[:, sl], c, sa, sb) * scale).T.astype(BF16)
        k_ref[:, sl] = (k[:, sl] + kr).astype(BF16)


def _mla_proj_call(h, t, w, tables, tm=ATTN_TILE):
    d = h.shape[1]
    hq = MLA_HEADS * HEAD_PAD
    hv = MLA_HEADS * MLA_V
    row = lambda n: pl.BlockSpec((tm, n), lambda i: (i, 0))
    tile = lambda n: pl.BlockSpec((1, n, tm), lambda i: (i, 0, 0))
    return pl.pallas_call(
        _mla_proj_kernel,
        out_shape=(jax.ShapeDtypeStruct((t // tm, hq, tm), BF16),
                   jax.ShapeDtypeStruct((t, hq), BF16),
                   jax.ShapeDtypeStruct((t // tm, hv, tm), BF16)),
        grid=(t // tm,),
        in_specs=[row(d), _full((d, MLA_Q_RANK)), _full((d, MLA_KV_RANK)), _full((d, LANES)),
                  _full((1, MLA_Q_RANK)), _full((1, MLA_KV_RANK)), _full((MLA_Q_RANK, hq)),
                  _full((MLA_KV_RANK, hq)), _full((MLA_KV_RANK, hv)),
                  row(LANES), row(LANES), row(LANES)],
        out_specs=(tile(hq), row(hq), tile(hv)),
        compiler_params=_params(("parallel",)),
        name="mla_proj",
    )(h, w["wcq"], w["wckv"], w["wkr"], w["qn"], w["kvn"], w["wuq"], w["wuk"], w["wuv"], *tables)


def _flash_kernel(qt_ref, k_ref, vt_ref, o_ref, m0, l0, a0, m1, l1, a1, sa_ref, sb_ref,
                  *, tk):
    qi = pl.program_id(2)
    stats = ((m0, l0, a0), (m1, l1, a1))
    for m_ref, l_ref, acc_ref in stats:
        m_ref[...] = jnp.full(m_ref.shape, MASK_VALUE, F32)
        l_ref[...] = jnp.zeros(l_ref.shape, F32)
        acc_ref[...] = jnp.zeros(acc_ref.shape, F32)

    def score(j, s_ref, lo):
        rows = pl.ds(pl.multiple_of(j * tk, tk), tk)
        for hh in range(2):
            sl = slice(hh * HEAD_PAD, (hh + 1) * HEAD_PAD)
            qs = [qt_ref[0, u, sl, :] for u in range(lo // tk, 2)]
            qh = qs[0] if len(qs) == 1 else jnp.concatenate(qs, axis=1)
            s_ref[hh, :, lo:] = jnp.dot(k_ref[0, rows, sl], qh, preferred_element_type=F32)

    def consume(j, s_ref, lo, masked):
        vt = vt_ref[0, j, 0]
        for hh, (m_ref, l_ref, acc_ref) in enumerate(stats):
            st = s_ref[hh, :, lo:]
            if masked:
                key = lax.broadcasted_iota(jnp.int32, st.shape, 0)
                qry = lax.broadcasted_iota(jnp.int32, st.shape, 1)
                st = jnp.where(key <= qry, st, MASK_VALUE)
            m_old = m_ref[:, lo:]
            m_new = jnp.maximum(m_old, jnp.max(st, axis=0, keepdims=True))
            alpha = jnp.exp2(m_old - m_new)
            p = jnp.exp2(st - m_new)
            l_ref[:, lo:] = alpha * l_ref[:, lo:] + jnp.sum(p, axis=0, keepdims=True)
            acc_ref[:, lo:] = alpha * acc_ref[:, lo:] + jnp.dot(vt, p.astype(BF16),
                                                                preferred_element_type=F32)
            m_ref[:, lo:] = m_new

    score(0, sa_ref, 0)

    def body(i, carry):
        j = 2 * i
        score(j + 1, sb_ref, 0)
        consume(j, sa_ref, 0, False)
        score(j + 2, sa_ref, 0)
        consume(j + 1, sb_ref, 0, False)
        return carry

    lax.fori_loop(0, qi, body, 0)
    score(2 * qi + 1, sb_ref, tk)
    consume(2 * qi, sa_ref, 0, True)
    consume(2 * qi + 1, sb_ref, tk, True)
    row = lax.broadcasted_iota(jnp.int32, a0.shape, 0)
    ot = jnp.where(row < MLA_V, a0[...] * (1.0 / l0[...]), a1[...] * (1.0 / l1[...]))
    o_ref[0] = ot.T.astype(BF16)


def _flash_call(qt, k, vt, batch, tk=ATTN_TILE):
    nt, hq, _ = qt.shape
    t = nt * tk
    s = t // batch
    ns = s // tk
    tq = 2 * tk
    pairs = MLA_HEADS // 2
    hv = MLA_HEADS * MLA_V
    qt5 = qt.reshape(batch, ns, hq, tk)
    k3 = k.reshape(batch, s, hq)
    vt5 = vt.reshape(batch, ns, pairs, 2 * MLA_V, tk)
    stat = lambda: pltpu.VMEM((1, tq), F32)
    acc = lambda: pltpu.VMEM((2 * MLA_V, tq), F32)
    out = pl.pallas_call(
        functools.partial(_flash_kernel, tk=tk),
        out_shape=jax.ShapeDtypeStruct((batch, s, hv), BF16),
        grid=(batch, pairs, s // tq),
        in_specs=[pl.BlockSpec((1, 2, 2 * HEAD_PAD, tk), lambda b, p, i: (b, i, p, 0)),
                  pl.BlockSpec((1, s, 2 * HEAD_PAD), lambda b, p, i: (b, 0, p)),
                  pl.BlockSpec((1, ns, 1, 2 * MLA_V, tk), lambda b, p, i: (b, 0, p, 0, 0))],
        out_specs=pl.BlockSpec((1, tq, 2 * MLA_V), lambda b, p, i: (b, i, p)),
        scratch_shapes=[stat(), stat(), acc(), stat(), stat(), acc(),
                        pltpu.VMEM((2, tk, tq), F32), pltpu.VMEM((2, tk, tq), F32)],
        compiler_params=_params(("parallel", "parallel", "arbitrary")),
        name="flash_attn",
    )(qt5, k3, vt5)
    return out.reshape(t, hv)


def _proj_kernel(h_ref, w_ref, o_ref):
    o_ref[...] = _dot(h_ref[...], w_ref[...])


def _proj_call(h, t, w, name, tm=512):
    d = h.shape[1]
    n = w.shape[1]
    return pl.pallas_call(
        _proj_kernel,
        out_shape=jax.ShapeDtypeStruct((t, n), F32),
        grid=(t // tm,),
        in_specs=[pl.BlockSpec((tm, d), lambda i: (i, 0)), _full((d, n))],
        out_specs=pl.BlockSpec((tm, n), lambda i: (i, 0)),
        compiler_params=_params(("parallel",)),
        name=name,
    )(h, w)


def _chunk_cumsum(x):
    rows = lax.broadcasted_iota(jnp.int32, x.shape, 0) % CHUNK
    shift = 1
    while shift < CHUNK:
        x = x + jnp.where(rows >= shift, pltpu.roll(x, shift, 0), 0.0)
        shift *= 2
    return x


def _chunk_last(x):
    r, n = x.shape
    x3 = x.reshape(r // CHUNK, CHUNK, n)
    return jnp.broadcast_to(x3[:, CHUNK - 1:CHUNK, :], x3.shape).reshape(r, n)


def _chunk_masks(r):
    i = lax.broadcasted_iota(jnp.int32, (r, r), 0)
    j = lax.broadcasted_iota(jnp.int32, (r, r), 1)
    same = (i // CHUNK) == (j // CHUNK)
    return same & (j <= i), same & (j < i), i == j


def _dn_solve(heads, eye):
    invs = [eye - p["m"] for p in heads]
    pws = [p["m"].astype(BF16) for p in heads]
    for _ in range(5):
        pws = [jnp.dot(pw, pw, preferred_element_type=F32).astype(BF16) for pw in pws]
        invs = [inv + jnp.dot(inv.astype(BF16), pw, preferred_element_type=F32)
                for inv, pw in zip(invs, pws)]
    for p, inv in zip(heads, invs):
        uw = jnp.dot(inv.astype(BF16), p["rhs"], preferred_element_type=F32).astype(BF16)
        quw = jnp.dot(p["qk"].astype(BF16), uw, preferred_element_type=F32)
        p["o_own"] = quw[:, :DN_DV]
        q_eff = (p["qd"] - quw[:, DN_DV:]).astype(BF16)
        kd = p["kd"].astype(BF16)
        p["lhs"], p["add"] = [], []
        for c in range(uw.shape[0] // CHUNK):
            rc = slice(c * CHUNK, (c + 1) * CHUNK)
            kuw = lax.dot_general(kd[rc], uw[rc], (((0,), (0,)), ((), ())),
                                  preferred_element_type=F32)
            p["add"].append(kuw[:, :DN_DV])
            p["lhs"].append(jnp.concatenate([kuw[:, DN_DV:].astype(BF16), q_eff[rc]], axis=0))


def _dn_kernel(x_ref, conv_ref, alog_ref, dtb_ref, onorm_ref, o_ref, halo_ref, state_ref,
               *, batch):
    r = SCAN_ROWS
    nqkv = DN_HEADS * (2 * DN_DK + DN_DV)
    gate_off = nqkv
    ba_off = nqkv + DN_HEADS * DN_DV

    @pl.when(pl.program_id(0) == 0)
    def _():
        halo_ref[...] = jnp.zeros(halo_ref.shape, F32)
        state_ref[...] = jnp.zeros(state_ref.shape, F32)

    incl, strict, diag = _chunk_masks(r)
    eye = jnp.where(diag, 1.0, 0.0)
    cw = conv_ref[...]
    pre = []
    for b in range(batch):
        xq = x_ref[b, :, :nqkv]
        xe = jnp.concatenate([halo_ref[b], xq], axis=0)
        halo_ref[b] = xq[r - 8:, :]
        y = xe[8:, :] * cw[DN_CONV - 1:DN_CONV, :]
        for i in range(DN_CONV - 1):
            y = y + pltpu.roll(xe, DN_CONV - 1 - i, 0)[8:, :] * cw[i:i + 1, :]
        y = _silu(y)
        ba = x_ref[b, :, ba_off:ba_off + LANES]
        beta_all = _sigmoid(ba)
        av = ba + dtb_ref[...]
        softplus = jnp.maximum(av, 0.0) + jnp.log(1.0 + jnp.exp(-jnp.abs(av)))
        g_all = _chunk_cumsum(-jnp.exp(alog_ref[...]) * softplus)
        g_all_t = g_all.T
        g_last_all = _chunk_last(g_all)
        for hd in range(DN_HEADS):
            q = y[:, hd * DN_DK:(hd + 1) * DN_DK]
            k = y[:, DN_HEADS * DN_DK + hd * DN_DK:DN_HEADS * DN_DK + (hd + 1) * DN_DK]
            v = y[:, 2 * DN_HEADS * DN_DK + hd * DN_DV:2 * DN_HEADS * DN_DK + (hd + 1) * DN_DV]
            q = q * lax.rsqrt(jnp.sum(q * q, axis=-1, keepdims=True) + NORM_EPS) * DN_DK ** -0.5
            k = k * lax.rsqrt(jnp.sum(k * k, axis=-1, keepdims=True) + NORM_EPS)
            beta = beta_all[:, hd:hd + 1]
            g_col = g_all[:, DN_HEADS + hd:DN_HEADS + hd + 1]
            g_row = g_all_t[DN_HEADS + hd:DN_HEADS + hd + 1, :]
            g_last = g_last_all[:, DN_HEADS + hd:DN_HEADS + hd + 1]
            decay = jnp.where(incl, jnp.exp(jnp.where(incl, g_col - g_row, 0.0)), 0.0)
            kb = k * beta
            kk = _dot_nt(kb, k)
            m = jnp.where(strict, kk * decay, 0.0)
            qk = jnp.where(incl, _dot_nt(q, k) * decay, 0.0)
            eg = jnp.exp(g_col)
            pre.append(dict(m=m, rhs=jnp.concatenate([v * beta, kb * eg], axis=1).astype(BF16),
                            qd=q * eg, kd=k * jnp.exp(g_last - g_col), qk=qk,
                            gl=jnp.exp(g_last)))
        _dn_solve(pre[b * DN_HEADS:], eye)
    outs = [[] for _ in pre]
    for c in range(r // CHUNK):
        rc = slice(c * CHUNK, (c + 1) * CHUNK)
        for idx, p in enumerate(pre):
            st = state_ref[idx]
            prod = jnp.dot(p["lhs"][c], st.astype(BF16), preferred_element_type=F32)
            outs[idx].append(prod[DN_DK:] + p["o_own"][rc])
            state_ref[idx] = (st * p["gl"][c * CHUNK:c * CHUNK + 1, :] - prod[:DN_DK]
                              + p["add"][c])
    for idx in range(len(pre)):
        b, hd = divmod(idx, DN_HEADS)
        o = jnp.concatenate(outs[idx], axis=0)
        gate = x_ref[b, :, gate_off + hd * DN_DV:gate_off + (hd + 1) * DN_DV]
        o_ref[b, :, hd * DN_DV:(hd + 1) * DN_DV] = (
            _rms_norm(o, onorm_ref[...]) * _silu(gate)).astype(BF16)


def _dn_call(x3, conv_w, alog128, dtb128, onorm):
    batch, s, n = x3.shape
    r = SCAN_ROWS
    nqkv = DN_HEADS * (2 * DN_DK + DN_DV)
    width = DN_HEADS * DN_DV
    return pl.pallas_call(
        functools.partial(_dn_kernel, batch=batch),
        out_shape=jax.ShapeDtypeStruct((batch, s, width), BF16),
        grid=(s // r,),
        in_specs=[pl.BlockSpec((batch, r, n), lambda i: (0, i, 0)), _full((DN_CONV, nqkv)),
                  _full((1, LANES)), _full((1, LANES)), _full((1, DN_DV))],
        out_specs=pl.BlockSpec((batch, r, width), lambda i: (0, i, 0)),
        scratch_shapes=[pltpu.VMEM((batch, 8, nqkv), F32),
                        pltpu.VMEM((batch * DN_HEADS, DN_DK, DN_DV), F32)],
        compiler_params=_params(("arbitrary",)),
        name="deltanet",
    )(x3, conv_w, alog128, dtb128, onorm)


HG_LEVELS = (32, 16, 8, 4, 2, 1)


def _hg_kernel(x_ref, lbp_ref, onorm_ref, sel_ref, o_ref, state_ref, *, batch, layer):
    r = SCAN_ROWS
    width = HG_HEADS * HG_DK

    @pl.when(pl.program_id(0) == 0)
    def _():
        state_ref[...] = jnp.zeros(state_ref.shape, F32)

    lbp = lbp_ref[...]
    e = jnp.exp(lbp - jnp.max(lbp, axis=0, keepdims=True))
    soft = e / jnp.sum(e, axis=0, keepdims=True)
    lb_all = jnp.zeros((1, width), F32)
    for i in range(1, layer + 1):
        lb_all = lb_all + soft[i:i + 1, :]

    i_idx = lax.broadcasted_iota(jnp.int32, (r, r), 0)
    j_idx = lax.broadcasted_iota(jnp.int32, (r, r), 1)
    level_masks = [((i_idx // (2 * s)) == (j_idx // (2 * s))) & ((i_idx % (2 * s)) >= s)
                   & ((j_idx % (2 * s)) < s) for s in HG_LEVELS]
    diag = i_idx == j_idx

    for b in range(batch):
        for hd in range(HG_HEADS):
            idx = b * HG_HEADS + hd
            hs = slice(hd * HG_DK, (hd + 1) * HG_DK)
            lb = lb_all[:, hs]
            q = _silu(x_ref[b, :, hd * HG_DK:(hd + 1) * HG_DK])
            z = x_ref[b, :, width + hd * HG_DK:width + (hd + 1) * HG_DK]
            v = x_ref[b, :, 2 * width + hd * HG_DV:2 * width + (hd + 1) * HG_DV]
            gate = x_ref[b, :, 3 * width + hd * HG_DV:3 * width + (hd + 1) * HG_DV]
            f = lb + (1.0 - lb) * _sigmoid(z)
            k = (1.0 - lb) * _sigmoid(-z)
            cum = _chunk_cumsum(jnp.log(jnp.maximum(f, MIN_FORGET))) * math.log2(math.e)
            cum_last = _chunk_last(cum)
            qd = q * jnp.exp2(cum)
            kd = k * jnp.exp2(cum_last - cum)
            dl = jnp.exp2(cum_last)
            att = jnp.where(diag, _dot_nt(q, k), 0.0)
            cum_b = cum.astype(BF16)
            for lv, (s, mask) in enumerate(zip(HG_LEVELS, level_masks)):
                if 2 * s >= 8:
                    grp = cum.reshape(r // (2 * s), 2 * s, HG_DK)
                    ref = jnp.broadcast_to(grp[:, s:s + 1, :], grp.shape).reshape(r, HG_DK)
                else:
                    ref = jnp.dot(sel_ref[lv], cum_b, preferred_element_type=F32)
                fac = jnp.exp2(cum - ref)
                att = att + jnp.where(mask, _dot_nt(q * fac, k * (1.0 / fac)), 0.0)
            o_intra = _dot(att, v)
            outs = []
            st = state_ref[idx]
            for c in range(r // CHUNK):
                rc = slice(c * CHUNK, (c + 1) * CHUNK)
                outs.append(_dot_nt(qd[rc], st) + o_intra[rc])
                st = st * dl[c * CHUNK:c * CHUNK + 1, :] + _dot_tn(v[rc], kd[rc])
            state_ref[idx] = st
            o = jnp.concatenate(outs, axis=0)
            o_ref[b, :, hs] = (_rms_norm(o, onorm_ref[...]) * _silu(gate)).astype(BF16)


def _hg_sel():
    r = SCAN_ROWS
    i = np.arange(r)
    sel = np.zeros((len(HG_LEVELS), r, r), np.float32)
    for lv, s in enumerate(HG_LEVELS):
        sel[lv, i, (i // (2 * s)) * (2 * s) + s] = 1.0
    return jnp.asarray(sel, BF16)


def _hg_call(x3, lbp, onorm, layer):
    batch, s, n = x3.shape
    r = SCAN_ROWS
    width = HG_HEADS * HG_DV
    return pl.pallas_call(
        functools.partial(_hg_kernel, batch=batch, layer=layer),
        out_shape=jax.ShapeDtypeStruct((batch, s, width), BF16),
        grid=(s // r,),
        in_specs=[pl.BlockSpec((batch, r, n), lambda i: (0, i, 0)), _full(lbp.shape),
                  _full((1, HG_DV)), _full((len(HG_LEVELS), r, r))],
        out_specs=pl.BlockSpec((batch, r, width), lambda i: (0, i, 0)),
        scratch_shapes=[pltpu.VMEM((batch * HG_HEADS, HG_DV, HG_DK), F32)],
        compiler_params=_params(("arbitrary",)),
        name="hgrn2",
    )(x3, lbp, onorm, _hg_sel())


def _merge_kernel(h_ref, oa_ref, ob_ref, oc_ref, wgate_ref, wa_ref, wb_ref, wc_ref, wout_ref,
                  g_ref, b_ref, rhi_ref, rlo_ref, rb_ref, ext_ref, info_ref):
    h = h_ref[...]
    gates = _sigmoid(_dot(h, wgate_ref[...]))
    d = D_MODEL
    mixed = (gates[:, :d] * _dot(oa_ref[...], wa_ref[...])
             + gates[:, d:2 * d] * _dot(ob_ref[...], wb_ref[...])
             + gates[:, 2 * d:] * _dot(oc_ref[...], wc_ref[...]))
    h1 = _layer_norm(DEEPNORM_ALPHA * h + _dot(mixed, wout_ref[...]), g_ref[...], b_ref[...])
    hi = h1.astype(BF16)
    lo = (h1 - hi.astype(F32)).astype(BF16)
    logits = (jnp.dot(hi, rhi_ref[...], preferred_element_type=F32)
              + jnp.dot(hi, rlo_ref[...], preferred_element_type=F32)
              + jnp.dot(lo, rhi_ref[...], preferred_element_type=F32)) + rb_ref[...]
    lane = lax.broadcasted_iota(jnp.int32, logits.shape, 1)
    neg = jnp.float32(-jnp.inf)
    big = jnp.int32(1 << 20)
    is_g = lane < N_GROUPS
    gl = jnp.where(is_g, logits, neg)
    gmax = jnp.max(gl, axis=-1, keepdims=True)
    gidx = jnp.min(jnp.where(gl == gmax, lane, big), axis=-1, keepdims=True)
    p_group = 1.0 / jnp.sum(jnp.where(is_g, jnp.exp(gl - gmax), 0.0), axis=-1, keepdims=True)
    lo_lane = N_GROUPS + gidx * EXPERTS_PER_GROUP
    in_g = (lane >= lo_lane) & (lane < lo_lane + EXPERTS_PER_GROUP)
    el = jnp.where(in_g, logits, neg)
    v1 = jnp.max(el, axis=-1, keepdims=True)
    i1 = jnp.min(jnp.where(el == v1, lane, big), axis=-1, keepdims=True)
    el2 = jnp.where(lane == i1, neg, el)
    v2 = jnp.max(el2, axis=-1, keepdims=True)
    i2 = jnp.min(jnp.where(el2 == v2, lane, big), axis=-1, keepdims=True)
    e21 = jnp.exp(v2 - v1)
    w1 = p_group / (1.0 + e21)
    w2 = p_group * e21 / (1.0 + e21)
    info = (jnp.where(lane == i1 - lo_lane, w1, 0.0) + jnp.where(lane == i2 - lo_lane, w2, 0.0)
            + jnp.where(lane == EXPERTS_PER_GROUP, gidx.astype(F32), 0.0))
    info_ref[...] = info
    ext_ref[:, :D_MODEL] = h1
    ext_ref[:, D_MODEL:] = info


def _merge_call(h, oa, ob, oc, w, tm=512):
    t, d = oa.shape[0], h.shape[1]
    row = lambda n: pl.BlockSpec((tm, n), lambda i: (i, 0))
    wa, wb, wc = w["wbr_a"], w["wbr_b"], w["wbr_c"]
    return pl.pallas_call(
        _merge_kernel,
        out_shape=(jax.ShapeDtypeStruct((t, EXT_COLS), F32),
                   jax.ShapeDtypeStruct((t, LANES), F32)),
        grid=(t // tm,),
        in_specs=[row(d), row(oa.shape[1]), row(ob.shape[1]), row(oc.shape[1]),
                  _full((d, 3 * d)), _full(wa.shape), _full(wb.shape), _full(wc.shape),
                  _full((d, d)), _full((1, d)), _full((1, d)),
                  _full((d, LANES)), _full((d, LANES)), _full((1, LANES))],
        out_specs=(row(EXT_COLS), row(LANES)),
        compiler_params=_params(("parallel",)),
        name="merge_ln_router",
    )(h, oa, ob, oc, w["wgate"], wa, wb, wc, w["wout"], w["ln1_g"], w["ln1_b"],
      w["r_hi"], w["r_lo"], w["r_b"])


def _rank_kernel(info_ref, route_ref, cnt_ref, base_ref):
    @pl.when(pl.program_id(0) == 0)
    def _():
        base_ref[...] = jnp.zeros(base_ref.shape, F32)

    info = info_ref[...]
    tb = info.shape[0]
    lane = lax.broadcasted_iota(jnp.int32, info.shape, 1)
    gid = info[:, EXPERTS_PER_GROUP:EXPERTS_PER_GROUP + 1]
    onehot = jnp.where((lane.astype(F32) == gid) & (lane < N_GROUPS), 1.0, 0.0)
    r = lax.broadcasted_iota(jnp.int32, (tb, tb), 0)
    c = lax.broadcasted_iota(jnp.int32, (tb, tb), 1)
    earlier = jnp.where(c < r, 1.0, 0.0)
    prefix = _dot(earlier, onehot)
    base = base_ref[...]
    rank = jnp.sum(onehot * (prefix + base), axis=1, keepdims=True)
    route_ref[...] = jnp.where(lane == 0, rank, jnp.where(lane == 1, gid, 0.0))
    base = base + jnp.sum(onehot, axis=0, keepdims=True)
    base_ref[...] = base
    cnt_ref[...] = base


def _rank_call(info, tb=512):
    t = info.shape[0]
    return pl.pallas_call(
        _rank_kernel,
        out_shape=(jax.ShapeDtypeStruct((t, LANES), F32), jax.ShapeDtypeStruct((1, LANES), F32)),
        grid=(t // tb,),
        in_specs=[pl.BlockSpec((tb, LANES), lambda i: (i, 0))],
        out_specs=(pl.BlockSpec((tb, LANES), lambda i: (i, 0)),
                   pl.BlockSpec((1, LANES), lambda i: (0, 0))),
        scratch_shapes=[pltpu.VMEM((1, LANES), F32)],
        compiler_params=_params(("arbitrary",)),
        name="moe_rank",
    )(info)


def _dispatch_plan(route, cnt, t, tm):
    n_tiles = t // tm + N_GROUPS
    rows = n_tiles * tm
    rank = route[:, 0].astype(jnp.int32)
    gid = route[:, 1].astype(jnp.int32)
    counts = cnt[0, :N_GROUPS].astype(jnp.int32)
    padded = (counts + tm - 1) // tm * tm
    ends = jnp.cumsum(padded)
    pos = (ends - padded)[gid] + rank
    tok = jnp.arange(t, dtype=jnp.int32)
    row = jnp.arange(rows, dtype=jnp.int32)
    owner = jnp.full((rows,), -1, jnp.int32).at[pos].set(tok, unique_indices=True,
                                                          mode="promise_in_bounds")
    src = jnp.maximum(owner, 0)
    dst = jnp.where(owner < 0, t + row % tm, owner)
    dst = jnp.concatenate([t + row[:tm], dst])
    tile_start = jnp.arange(n_tiles, dtype=jnp.int32) * tm
    tile_group = jnp.minimum(jnp.sum(tile_start[:, None] >= ends[None, :], axis=1), N_GROUPS - 1)
    return src, dst, tile_group.astype(jnp.int32), n_tiles


def _expert_kernel(src_ref, dst_ref, tg_ref, ext_hbm, wg_ref, wu_ref, wd_ref, g_ref, b_ref, out_hbm,
                   xa, xb, oa, ob, gsem, ssem, *, tm, n_tiles):
    i = pl.program_id(0)
    bufs = ((xa, oa, 0), (xb, ob, 1))

    def gather_copy(tile, buf, r):
        return pltpu.make_async_copy(ext_hbm.at[pl.ds(src_ref[tile * tm + r], 1)],
                                     buf[0].at[pl.ds(r, 1)], gsem.at[buf[2]])

    def scatter_copy(tile, buf, r):
        return pltpu.make_async_copy(buf[1].at[pl.ds(r, 1)],
                                     out_hbm.at[pl.ds(dst_ref[(tile + 1) * tm + r], 1)],
                                     ssem.at[buf[2]])

    def wait_gather(buf):
        pltpu.make_async_copy(ext_hbm.at[pl.ds(0, tm)], buf[0], gsem.at[buf[2]]).wait()

    def wait_scatter(buf):
        pltpu.make_async_copy(buf[1], out_hbm.at[pl.ds(0, tm)], ssem.at[buf[2]]).wait()

    @pl.when(i == 0)
    def _():
        ob[...] = jnp.zeros(ob.shape, F32)
        for r in range(tm):
            gather_copy(0, bufs[0], r).start()

    def tile_body(cur, oth):
        wait_gather(cur)

        @pl.when(i >= 1)
        def _():
            wait_scatter(cur)

        x = cur[0][:, :D_MODEL]
        info = cur[0][:, D_MODEL:]
        xb_ = x.astype(BF16)
        acc = jnp.zeros((tm, D_MODEL), F32)
        per = tm // EXPERTS_PER_GROUP
        nxt = jnp.minimum(i + 1, n_tiles - 1)
        for e in range(EXPERTS_PER_GROUP):
            act = (_silu(jnp.dot(xb_, wg_ref[0, e], preferred_element_type=F32))
                   * jnp.dot(xb_, wu_ref[0, e], preferred_element_type=F32) * info[:, e:e + 1])
            acc = acc + _dot(act, wd_ref[0, e])
            half = EXPERTS_PER_GROUP // 2
            for r in range((e % half) * 2 * per, (e % half + 1) * 2 * per):
                if e < half:
                    gather_copy(nxt, oth, r).start(priority=r % 2)
                else:
                    scatter_copy(i - 1, oth, r).start(priority=r % 2)
        cur[1][...] = _layer_norm(DEEPNORM_ALPHA * x + acc, g_ref[...], b_ref[...])

        @pl.when(i == n_tiles - 1)
        def _():
            wait_scatter(oth)
            for r in range(tm):
                scatter_copy(i, cur, r).start()
            wait_scatter(cur)
            wait_gather(oth)

    @pl.when(i % 2 == 0)
    def _():
        tile_body(bufs[0], bufs[1])

    @pl.when(i % 2 == 1)
    def _():
        tile_body(bufs[1], bufs[0])


def _expert_call(ext, plan, w, tm):
    src, dst, tile_group, n_tiles = plan
    t = ext.shape[0]
    d = D_MODEL
    chunks = d // LANES
    wspec = lambda shape: pl.BlockSpec((1,) + shape, lambda i, s, d_, tg: (tg[i], 0, 0, 0))
    vec = pl.BlockSpec((1, d), lambda i, s, d_, tg: (0, 0))
    grid_spec = pltpu.PrefetchScalarGridSpec(
        num_scalar_prefetch=3,
        grid=(n_tiles,),
        in_specs=[pl.BlockSpec(memory_space=pl.ANY), wspec((EXPERTS_PER_GROUP, d, EXPERT_FF)),
                  wspec((EXPERTS_PER_GROUP, d, EXPERT_FF)), wspec((EXPERTS_PER_GROUP, EXPERT_FF, d)),
                  vec, vec],
        out_specs=pl.BlockSpec(memory_space=pl.ANY),
        scratch_shapes=[pltpu.VMEM((tm, EXT_COLS), F32), pltpu.VMEM((tm, EXT_COLS), F32),
                        pltpu.VMEM((tm, d), F32), pltpu.VMEM((tm, d), F32),
                        pltpu.SemaphoreType.DMA((2,)), pltpu.SemaphoreType.DMA((2,))],
    )
    out = pl.pallas_call(
        functools.partial(_expert_kernel, tm=tm, n_tiles=n_tiles),
        out_shape=jax.ShapeDtypeStruct((t + tm, d), F32),
        grid_spec=grid_spec,
        compiler_params=_params(("arbitrary",)),
        name="moe_experts",
    )(src, dst, tile_group, ext, w["e_gate"], w["e_up"], w["e_down"], w["ln2_g"], w["ln2_b"])
    return out


def _moe_call(ext, info, w, tm=MOE_TILE):
    route, cnt = _rank_call(info)
    return _expert_call(ext, _dispatch_plan(route, cnt, ext.shape[0], tm), w, tm)


def _pad_cols(w, lo, total):
    return jnp.zeros((w.shape[0], total), w.dtype).at[:, lo:lo + w.shape[1]].set(w)


def _layer_weights(l, p):
    d = D_MODEL
    w_in = p["w_in"][l]
    sizes = (MLA_Q_RANK, MLA_KV_RANK, MLA_ROPE, DN_HEADS * (2 * DN_DK + DN_DV), DN_HEADS, DN_HEADS,
             DN_HEADS * DN_DV, HG_HEADS * HG_DK, HG_HEADS * HG_DK, HG_HEADS * HG_DV,
             HG_HEADS * HG_DV, 3 * d)
    offs = np.concatenate([[0], np.cumsum(sizes)])
    col = lambda i: w_in[:, offs[i]:offs[i + 1]]
    qk_w = MLA_NOPE + MLA_ROPE
    uq = p["mla_w_uq"][l].reshape(MLA_Q_RANK, MLA_HEADS, qk_w)
    uq = jnp.pad(uq, ((0, 0), (0, 0), (0, HEAD_PAD - qk_w))).reshape(MLA_Q_RANK, MLA_HEADS * HEAD_PAD)
    ukv = p["mla_w_ukv"][l].reshape(MLA_KV_RANK, MLA_HEADS, MLA_NOPE + MLA_V)
    uk = jnp.pad(ukv[:, :, :MLA_NOPE], ((0, 0), (0, 0), (0, HEAD_PAD - MLA_NOPE)))
    uk = uk.reshape(MLA_KV_RANK, MLA_HEADS * HEAD_PAD)
    uv = ukv[:, :, MLA_NOPE:].reshape(MLA_KV_RANK, MLA_HEADS * MLA_V)
    ba = jnp.concatenate([col(4), col(5)], axis=1)
    w_dn = jnp.concatenate([col(3), col(6), _pad_cols(ba, 0, LANES)], axis=1)
    w_hg = jnp.concatenate([col(7), col(8), col(9), col(10)], axis=1)
    router = jnp.concatenate([p["router_group_w"][l], p["router_expert_w"][l]], axis=1)
    router = _pad_cols(router, 0, LANES)
    r_hi = router.astype(BF16)
    r_b = jnp.concatenate([p["router_group_b"][l], p["router_expert_b"][l]])
    bf = lambda a: a.astype(BF16)
    return dict(
        wcq=bf(col(0)), wckv=bf(col(1)), wkr=bf(_pad_cols(col(2), ROPE_LO, LANES)),
        qn=p["mla_q_norm"][l].reshape(1, -1), kvn=p["mla_kv_norm"][l].reshape(1, -1),
        wuq=bf(uq), wuk=bf(uk), wuv=bf(uv),
        w_dn=bf(w_dn), w_hg=bf(w_hg), wgate=bf(col(11)),
        dn_conv=p["dn_conv"][l],
        alog128=_pad_cols(p["dn_a_log"][l].reshape(1, -1), DN_HEADS, LANES),
        dtb128=_pad_cols(p["dn_dt_bias"][l].reshape(1, -1), DN_HEADS, LANES),
        dn_onorm=p["dn_o_norm"][l].reshape(1, -1), hg_onorm=p["hg_o_norm"][l].reshape(1, -1),
        wbr_a=bf(p["w_br_a"][l]), wbr_b=bf(p["w_br_b"][l]), wbr_c=bf(p["w_br_c"][l]),
        wout=bf(p["w_out"][l]),
        ln1_g=p["ln1_g"][l].reshape(1, d), ln1_b=p["ln1_b"][l].reshape(1, d),
        r_hi=r_hi, r_lo=bf(router - r_hi.astype(F32)), r_b=_pad_cols(r_b.reshape(1, -1), 0, LANES),
        e_gate=bf(p["exp_w_gate"][l]).reshape(N_GROUPS, EXPERTS_PER_GROUP, d, EXPERT_FF),
        e_up=bf(p["exp_w_up"][l]).reshape(N_GROUPS, EXPERTS_PER_GROUP, d, EXPERT_FF),
        e_down=bf(p["exp_w_down"][l]).reshape(N_GROUPS, EXPERTS_PER_GROUP, EXPERT_FF, d),
        ln2_g=p["ln2_g"][l].reshape(1, d), ln2_b=p["ln2_b"][l].reshape(1, d),
    )


def _forward(p):
    x = p["x"]
    batch, s, d = x.shape
    t = batch * s
    half = MLA_ROPE // 2
    inv_freq = ROPE_BASE ** (-jnp.arange(half, dtype=F32) / half)
    inv128 = _pad_cols(jnp.concatenate([inv_freq, inv_freq]).reshape(1, -1), ROPE_LO, LANES)
    tables = _rope_call(p["positions"].reshape(t, 1), inv128)
    h = _ln_call(x.reshape(t, d), p["ln_in_g"], p["ln_in_b"])
    for l in range(DEPTH):
        w = _layer_weights(l, p)
        q, k, v = _mla_proj_call(h, t, w, tables)
        oa = _flash_call(q, k, v, batch)
        x_dn = _proj_call(h, t, w["w_dn"], "proj_deltanet")
        ob = _dn_call(x_dn.reshape(batch, s, -1), w["dn_conv"], w["alog128"], w["dtb128"],
                      w["dn_onorm"])
        x_hg = _proj_call(h, t, w["w_hg"], "proj_hgrn2")
        oc = _hg_call(x_hg.reshape(batch, s, -1), p["hg_lower_bounds"], w["hg_onorm"], l)
        ext, info = _merge_call(h, oa, ob.reshape(t, -1), oc.reshape(t, -1), w)
        h = _moe_call(ext, info, w)
    return h[:t].reshape(batch, s, d)


def kernel(x, positions, ln_in_g, ln_in_b, hg_lower_bounds, w_in, mla_q_norm, mla_w_uq, mla_kv_norm, mla_w_ukv, dn_conv, dn_a_log, dn_dt_bias, dn_o_norm, hg_o_norm, w_br_a, w_br_b, w_br_c, w_out, ln1_g, ln1_b, router_group_w, router_group_b, router_expert_w, router_expert_b, exp_w_gate, exp_w_up, exp_w_down, ln2_g, ln2_b):
    return _forward(dict(
        x=x, positions=positions, ln_in_g=ln_in_g, ln_in_b=ln_in_b,
        hg_lower_bounds=hg_lower_bounds, w_in=w_in, mla_q_norm=mla_q_norm, mla_w_uq=mla_w_uq,
        mla_kv_norm=mla_kv_norm, mla_w_ukv=mla_w_ukv, dn_conv=dn_conv, dn_a_log=dn_a_log,
        dn_dt_bias=dn_dt_bias, dn_o_norm=dn_o_norm, hg_o_norm=hg_o_norm, w_br_a=w_br_a,
        w_br_b=w_br_b, w_br_c=w_br_c, w_out=w_out, ln1_g=ln1_g, ln1_b=ln1_b,
        router_group_w=router_group_w, router_group_b=router_group_b,
        router_expert_w=router_expert_w, router_expert_b=router_expert_b,
        exp_w_gate=exp_w_gate, exp_w_up=exp_w_up, exp_w_down=exp_w_down, ln2_g=ln2_g, ln2_b=ln2_b))
```

```python
import functools
import math

import jax
import jax.numpy as jnp
import numpy as np
from jax import lax
from jax.experimental import pallas as pl
from jax.experimental.pallas import tpu as pltpu

D_MODEL = 1024
DEPTH = 2
MLA_HEADS = 8
MLA_Q_RANK = 256
MLA_KV_RANK = 128
MLA_NOPE = 64
MLA_ROPE = 32
MLA_V = 64
ROPE_BASE = 10000.0
MASK_VALUE = -1e30
DN_HEADS = 4
DN_DK = 128
DN_DV = 128
DN_CONV = 4
HG_HEADS = 4
HG_DK = 128
HG_DV = 128
CHUNK = 64
MIN_FORGET = 1e-30
N_GROUPS = 4
EXPERTS_PER_GROUP = 8
N_EXPERTS = N_GROUPS * EXPERTS_PER_GROUP
EXPERT_FF = 256
DEEPNORM_ALPHA = (2 * DEPTH) ** 0.25
NORM_EPS = 1e-6

LANES = 128
HEAD_PAD = 128
ROPE_LO = MLA_NOPE
ROPE_MID = MLA_NOPE + MLA_ROPE // 2
ROPE_HI = MLA_NOPE + MLA_ROPE
SCAN_ROWS = 256
ATTN_TILE = 512
EXT_COLS = D_MODEL + LANES
MOE_TILE = 512
VMEM_LIMIT = 56 * 1024 * 1024

BF16 = jnp.bfloat16
F32 = jnp.float32


def _dot(a, b):
    return jnp.dot(a.astype(BF16), b.astype(BF16), preferred_element_type=F32)


def _dot_nt(a, b):
    return lax.dot_general(a.astype(BF16), b.astype(BF16), (((1,), (1,)), ((), ())),
                           preferred_element_type=F32)


def _dot_tn(a, b):
    return lax.dot_general(a.astype(BF16), b.astype(BF16), (((0,), (0,)), ((), ())),
                           preferred_element_type=F32)


def _sigmoid(x):
    return 1.0 / (1.0 + jnp.exp(-x))


def _silu(x):
    return x * _sigmoid(x)


def _layer_norm(x, g, b):
    mu = jnp.mean(x, axis=-1, keepdims=True)
    xc = x - mu
    var = jnp.mean(xc * xc, axis=-1, keepdims=True)
    return xc * lax.rsqrt(var + NORM_EPS) * g + b


def _rms_norm(x, g):
    return x * lax.rsqrt(jnp.mean(x * x, axis=-1, keepdims=True) + NORM_EPS) * g


def _params(sem):
    return pltpu.CompilerParams(dimension_semantics=sem, vmem_limit_bytes=VMEM_LIMIT)


def _full(shape):
    n = len(shape)
    return pl.BlockSpec(shape, lambda *_: (0,) * n)


def _ln_kernel(x_ref, g_ref, b_ref, o_ref):
    o_ref[...] = _layer_norm(x_ref[...], g_ref[...], b_ref[...])


def _ln_call(x, g, b, tm=1024):
    t, d = x.shape
    return pl.pallas_call(
        _ln_kernel,
        out_shape=jax.ShapeDtypeStruct((t, d), F32),
        grid=(t // tm,),
        in_specs=[pl.BlockSpec((tm, d), lambda i: (i, 0)), _full((1, d)), _full((1, d))],
        out_specs=pl.BlockSpec((tm, d), lambda i: (i, 0)),
        compiler_params=_params(("parallel",)),
        name="ln_in",
    )(x, g.reshape(1, d), b.reshape(1, d))


def _rope_kernel(pos_ref, inv_ref, c_ref, sa_ref, sb_ref):
    ang = pos_ref[...].astype(F32) * inv_ref[...]
    lane = lax.broadcasted_iota(jnp.int32, ang.shape, 1)
    sin = jnp.sin(ang)
    c_ref[...] = jnp.cos(ang)
    sa_ref[...] = jnp.where((lane >= ROPE_MID) & (lane < ROPE_HI), sin, 0.0)
    sb_ref[...] = jnp.where((lane >= ROPE_LO) & (lane < ROPE_MID), -sin, 0.0)


def _rope_call(pos_col, inv128, tm=1024):
    t = pos_col.shape[0]
    spec = pl.BlockSpec((tm, LANES), lambda i: (i, 0))
    shp = jax.ShapeDtypeStruct((t, LANES), F32)
    return pl.pallas_call(
        _rope_kernel,
        out_shape=(shp, shp, shp),
        grid=(t // tm,),
        in_specs=[pl.BlockSpec((tm, 1), lambda i: (i, 0)), _full((1, LANES))],
        out_specs=(spec, spec, spec),
        compiler_params=_params(("parallel",)),
        name="rope_tables",
    )(pos_col, inv128)


def _rope_apply(x, c, sa, sb):
    return x * c + pltpu.roll(x, 16, 1) * sa + pltpu.roll(x, LANES - 16, 1) * sb


def _mla_proj_kernel(h_ref, wcq_ref, wckv_ref, wkr_ref, qn_ref, kvn_ref, wuq_ref, wuk_ref,
                     wuv_ref, c_ref, sa_ref, sb_ref, qt_ref, k_ref, vt_ref):
    hb = h_ref[...].astype(BF16)
    cq = _rms_norm(_dot(hb, wcq_ref[...]), qn_ref[...])
    ckv = _rms_norm(_dot(hb, wckv_ref[...]), kvn_ref[...])
    kr = _dot(hb, wkr_ref[...])
    q = _dot(cq, wuq_ref[...])
    k = _dot(ckv, wuk_ref[...])
    v = _dot(ckv, wuv_ref[...])
    lane = lax.broadcasted_iota(jnp.int32, (1, HEAD_PAD), 1)
    one_lane = jnp.where(lane == MLA_V, 1.0, 0.0)
    for hd in range(MLA_HEADS):
        sl = slice(hd * HEAD_PAD, (hd + 1) * HEAD_PAD)
        vt_ref[0, hd] = (v[:, sl] + one_lane).T.astype(BF16)
    c, sa, sb = c_ref[...], sa_ref[...], sb_ref[...]
    scale = (MLA_NOPE + MLA_ROPE) ** -0.5 * math.log2(math.e)
    kr = _rope_apply(kr, c, sa, sb)
    for hd in range(MLA_HEADS):
        sl = slice(hd * HEAD_PAD, (hd + 1) * HEAD_PAD)
        qt_ref[0, sl, :] = (_rope_apply(q[:, sl], c, sa, sb) * scale).T.astype(BF16)
        k_ref[:, sl] = (k[:, sl] + kr).astype(BF16)


def _mla_proj_call(h, t, w, tables, tm=ATTN_TILE):
    d = h.shape[1]
    hq = MLA_HEADS * HEAD_PAD
    row = lambda n: pl.BlockSpec((tm, n), lambda i: (i, 0))
    tile = lambda n: pl.BlockSpec((1, n, tm), lambda i: (i, 0, 0))
    return pl.pallas_call(
        _mla_proj_kernel,
        out_shape=(jax.ShapeDtypeStruct((t // tm, hq, tm), BF16),
                   jax.ShapeDtypeStruct((t, hq), BF16),
                   jax.ShapeDtypeStruct((t // tm, MLA_HEADS, HEAD_PAD, tm), BF16)),
        grid=(t // tm,),
        in_specs=[row(d), _full((d, MLA_Q_RANK)), _full((d, MLA_KV_RANK)), _full((d, LANES)),
                  _full((1, MLA_Q_RANK)), _full((1, MLA_KV_RANK)), _full((MLA_Q_RANK, hq)),
                  _full((MLA_KV_RANK, hq)), _full((MLA_KV_RANK, hq)),
                  row(LANES), row(LANES), row(LANES)],
        out_specs=(tile(hq), row(hq),
                   pl.BlockSpec((1, MLA_HEADS, HEAD_PAD, tm), lambda i: (i, 0, 0, 0))),
        compiler_params=_params(("parallel",)),
        name="mla_proj",
    )(h, w["wcq"], w["wckv"], w["wkr"], w["qn"], w["kvn"], w["wuq"], w["wuk"], w["wuv"], *tables)


def _flash_kernel(qt_ref, k_ref, vt_ref, o_ref, m0, a0, m1, a1, sa_ref, sb_ref, *, tk):
    qi = pl.program_id(2)
    stats = ((m0, a0), (m1, a1))
    for m_ref, acc_ref in stats:
        m_ref[...] = jnp.full(m_ref.shape, MASK_VALUE, F32)
        acc_ref[...] = jnp.zeros(acc_ref.shape, F32)

    def score(j, s_ref, lo):
        rows = pl.ds(pl.multiple_of(j * tk, tk), tk)
        for hh in range(2):
            sl = slice(hh * HEAD_PAD, (hh + 1) * HEAD_PAD)
            qs = [qt_ref[0, u, sl, :] for u in range(lo // tk, 2)]
            qh = qs[0] if len(qs) == 1 else jnp.concatenate(qs, axis=1)
            s_ref[hh, :, lo:] = jnp.dot(k_ref[0, rows, sl], qh, preferred_element_type=F32)

    def consume(j, s_ref, lo, masked):
        for hh, (m_ref, acc_ref) in enumerate(stats):
            vt = vt_ref[0, j, 0, hh]
            st = s_ref[hh, :, lo:]
            if masked:
                key = lax.broadcasted_iota(jnp.int32, st.shape, 0)
                qry = lax.broadcasted_iota(jnp.int32, st.shape, 1)
                st = jnp.where(key <= qry, st, MASK_VALUE)
            m_old = m_ref[:, lo:]
            m_new = jnp.maximum(m_old, jnp.max(st, axis=0, keepdims=True))
            alpha = jnp.exp2(m_old - m_new)
            p = jnp.exp2(st - m_new)
            acc_ref[:, lo:] = alpha * acc_ref[:, lo:] + jnp.dot(vt, p.astype(BF16),
                                                                preferred_element_type=F32)
            m_ref[:, lo:] = m_new

    score(0, sa_ref, 0)

    def body(i, carry):
        j = 2 * i
        score(j + 1, sb_ref, 0)
        consume(j, sa_ref, 0, False)
        score(j + 2, sa_ref, 0)
        consume(j + 1, sb_ref, 0, False)
        return carry

    lax.fori_loop(0, qi, body, 0)
    score(2 * qi + 1, sb_ref, tk)
    consume(2 * qi, sa_ref, 0, True)
    consume(2 * qi + 1, sb_ref, tk, True)
    ot = jnp.concatenate([acc_ref[:MLA_V, :] * (1.0 / acc_ref[MLA_V:MLA_V + 1, :])
                          for _, acc_ref in stats], axis=0)
    o_ref[0] = ot.T.astype(BF16)


def _flash_call(qt, k, vt, batch, tk=ATTN_TILE):
    nt, hq, _ = qt.shape
    t = nt * tk
    s = t // batch
    ns = s // tk
    tq = 2 * tk
    pairs = MLA_HEADS // 2
    hv = MLA_HEADS * MLA_V
    qt5 = qt.reshape(batch, ns, hq, tk)
    k3 = k.reshape(batch, s, hq)
    vt6 = vt.reshape(batch, ns, pairs, 2, HEAD_PAD, tk)
    stat = lambda: pltpu.VMEM((1, tq), F32)
    acc = lambda: pltpu.VMEM((HEAD_PAD, tq), F32)
    out = pl.pallas_call(
        functools.partial(_flash_kernel, tk=tk),
        out_shape=jax.ShapeDtypeStruct((batch, s, hv), BF16),
        grid=(batch, pairs, s // tq),
        in_specs=[pl.BlockSpec((1, 2, 2 * HEAD_PAD, tk), lambda b, p, i: (b, i, p, 0)),
                  pl.BlockSpec((1, s, 2 * HEAD_PAD), lambda b, p, i: (b, 0, p)),
                  pl.BlockSpec((1, ns, 1, 2, HEAD_PAD, tk),
                               lambda b, p, i: (b, 0, p, 0, 0, 0))],
        out_specs=pl.BlockSpec((1, tq, 2 * MLA_V), lambda b, p, i: (b, i, p)),
        scratch_shapes=[stat(), acc(), stat(), acc(),
                        pltpu.VMEM((2, tk, tq), F32), pltpu.VMEM((2, tk, tq), F32)],
        compiler_params=_params(("parallel", "parallel", "arbitrary")),
        name="flash_attn",
    )(qt5, k3, vt6)
    return out.reshape(t, hv)


def _proj_kernel(h_ref, w_ref, o_ref):
    o_ref[...] = _dot(h_ref[...], w_ref[...])


def _proj_call(h, t, w, name, tm=512):
    d = h.shape[1]
    n = w.shape[1]
    return pl.pallas_call(
        _proj_kernel,
        out_shape=jax.ShapeDtypeStruct((t, n), F32),
        grid=(t // tm,),
        in_specs=[pl.BlockSpec((tm, d), lambda i: (i, 0)), _full((d, n))],
        out_specs=pl.BlockSpec((tm, n), lambda i: (i, 0)),
        compiler_params=_params(("parallel",)),
        name=name,
    )(h, w)


def _chunk_cumsum(x):
    rows = lax.broadcasted_iota(jnp.int32, x.shape, 0) % CHUNK
    shift = 1
    while shift < CHUNK:
        x = x + jnp.where(rows >= shift, pltpu.roll(x, shift, 0), 0.0)
        shift *= 2
    return x


def _chunk_last(x):
    r, n = x.shape
    x3 = x.reshape(r // CHUNK, CHUNK, n)
    return jnp.broadcast_to(x3[:, CHUNK - 1:CHUNK, :], x3.shape).reshape(r, n)


def _chunk_masks(r):
    i = lax.broadcasted_iota(jnp.int32, (r, r), 0)
    j = lax.broadcasted_iota(jnp.int32, (r, r), 1)
    same = (i // CHUNK) == (j // CHUNK)
    return same & (j <= i), same & (j < i), i == j


def _dn_solve(heads, eye):
    invs = [eye - p["m"] for p in heads]
    pws = [p["m"].astype(BF16) for p in heads]
    for _ in range(5):
        pws = [jnp.dot(pw, pw, preferred_element_type=F32).astype(BF16) for pw in pws]
        invs = [inv + jnp.dot(inv.astype(BF16), pw, preferred_element_type=F32)
                for inv, pw in zip(invs, pws)]
    for p, inv in zip(heads, invs):
        uw = jnp.dot(inv.astype(BF16), p["rhs"], preferred_element_type=F32).astype(BF16)
        quw = jnp.dot(p["qk"].astype(BF16), uw, preferred_element_type=F32)
        p["o_own"] = quw[:, :DN_DV]
        q_eff = (p["qd"] - quw[:, DN_DV:]).astype(BF16)
        kd = p["kd"].astype(BF16)
        p["lhs"], p["add"] = [], []
        for c in range(uw.shape[0] // CHUNK):
            rc = slice(c * CHUNK, (c + 1) * CHUNK)
            kuw = lax.dot_general(kd[rc], uw[rc], (((0,), (0,)), ((), ())),
                                  preferred_element_type=F32)
            p["add"].append(kuw[:, :DN_DV])
            p["lhs"].append(jnp.concatenate([kuw[:, DN_DV:].astype(BF16), q_eff[rc]], axis=0))


def _dn_kernel(x_ref, conv_ref, alog_ref, dtb_ref, onorm_ref, o_ref, halo_ref, state_ref,
               *, batch):
    r = SCAN_ROWS
    nqkv = DN_HEADS * (2 * DN_DK + DN_DV)
    gate_off = nqkv
    ba_off = nqkv + DN_HEADS * DN_DV

    @pl.when(pl.program_id(0) == 0)
    def _():
        halo_ref[...] = jnp.zeros(halo_ref.shape, F32)
        state_ref[...] = jnp.zeros(state_ref.shape, F32)

    incl, strict, diag = _chunk_masks(r)
    eye = jnp.where(diag, 1.0, 0.0)
    cw = conv_ref[...]
    pre = []
    for b in range(batch):
        xq = x_ref[b, :, :nqkv]
        xe = jnp.concatenate([halo_ref[b], xq], axis=0)
        halo_ref[b] = xq[r - 8:, :]
        y = xe[8:, :] * cw[DN_CONV - 1:DN_CONV, :]
        for i in range(DN_CONV - 1):
            y = y + pltpu.roll(xe, DN_CONV - 1 - i, 0)[8:, :] * cw[i:i + 1, :]
        y = _silu(y)
        ba = x_ref[b, :, ba_off:ba_off + LANES]
        beta_all = _sigmoid(ba)
        av = ba + dtb_ref[...]
        softplus = jnp.maximum(av, 0.0) + jnp.log(1.0 + jnp.exp(-jnp.abs(av)))
        g_all = _chunk_cumsum(-jnp.exp(alog_ref[...]) * softplus)
        g_all_t = g_all.T
        g_last_all = _chunk_last(g_all)
        for hd in range(DN_HEADS):
            q = y[:, hd * DN_DK:(hd + 1) * DN_DK]
            k = y[:, DN_HEADS * DN_DK + hd * DN_DK:DN_HEADS * DN_DK + (hd + 1) * DN_DK]
            v = y[:, 2 * DN_HEADS * DN_DK + hd * DN_DV:2 * DN_HEADS * DN_DK + (hd + 1) * DN_DV]
            q = q * lax.rsqrt(jnp.sum(q * q, axis=-1, keepdims=True) + NORM_EPS) * DN_DK ** -0.5
            k = k * lax.rsqrt(jnp.sum(k * k, axis=-1, keepdims=True) + NORM_EPS)
            beta = beta_all[:, hd:hd + 1]
            g_col = g_all[:, DN_HEADS + hd:DN_HEADS + hd + 1]
            g_row = g_all_t[DN_HEADS + hd:DN_HEADS + hd + 1, :]
            g_last = g_last_all[:, DN_HEADS + hd:DN_HEADS + hd + 1]
            decay = jnp.where(incl, jnp.exp(jnp.where(incl, g_col - g_row, 0.0)), 0.0)
            kb = k * beta
            kk = _dot_nt(kb, k)
            m = jnp.where(strict, kk * decay, 0.0)
            qk = jnp.where(incl, _dot_nt(q, k) * decay, 0.0)
            eg = jnp.exp(g_col)
            pre.append(dict(m=m, rhs=jnp.concatenate([v * beta, kb * eg], axis=1).astype(BF16),
                            qd=q * eg, kd=k * jnp.exp(g_last - g_col), qk=qk,
                            gl=jnp.exp(g_last)))
        _dn_solve(pre[b * DN_HEADS:], eye)
    outs = [[] for _ in pre]
    for c in range(r // CHUNK):
        rc = slice(c * CHUNK, (c + 1) * CHUNK)
        for idx, p in enumerate(pre):
            st = state_ref[idx]
            prod = jnp.dot(p["lhs"][c], st.astype(BF16), preferred_element_type=F32)
            outs[idx].append(prod[DN_DK:] + p["o_own"][rc])
            state_ref[idx] = (st * p["gl"][c * CHUNK:c * CHUNK + 1, :] - prod[:DN_DK]
                              + p["add"][c])
    for idx in range(len(pre)):
        b, hd = divmod(idx, DN_HEADS)
        o = jnp.concatenate(outs[idx], axis=0)
        gate = x_ref[b, :, gate_off + hd * DN_DV:gate_off + (hd + 1) * DN_DV]
        o_ref[b, :, hd * DN_DV:(hd + 1) * DN_DV] = (
            _rms_norm(o, onorm_ref[...]) * _silu(gate)).astype(BF16)


def _dn_call(x3, conv_w, alog128, dtb128, onorm):
    batch, s, n = x3.shape
    r = SCAN_ROWS
    nqkv = DN_HEADS * (2 * DN_DK + DN_DV)
    width = DN_HEADS * DN_DV
    return pl.pallas_call(
        functools.partial(_dn_kernel, batch=batch),
        out_shape=jax.ShapeDtypeStruct((batch, s, width), BF16),
        grid=(s // r,),
        in_specs=[pl.BlockSpec((batch, r, n), lambda i: (0, i, 0)), _full((DN_CONV, nqkv)),
                  _full((1, LANES)), _full((1, LANES)), _full((1, DN_DV))],
        out_specs=pl.BlockSpec((batch, r, width), lambda i: (0, i, 0)),
        scratch_shapes=[pltpu.VMEM((batch, 8, nqkv), F32),
                        pltpu.VMEM((batch * DN_HEADS, DN_DK, DN_DV), F32)],
        compiler_params=_params(("arbitrary",)),
        name="deltanet",
    )(x3, conv_w, alog128, dtb128, onorm)


HG_LEVELS = (32, 16, 8, 4, 2, 1)


def _hg_kernel(x_ref, lbp_ref, onorm_ref, sel_ref, o_ref, state_ref, *, batch, layer):
    r = SCAN_ROWS
    width = HG_HEADS * HG_DK

    @pl.when(pl.program_id(0) == 0)
    def _():
        state_ref[...] = jnp.zeros(state_ref.shape, F32)

    lbp = lbp_ref[...]
    e = jnp.exp(lbp - jnp.max(lbp, axis=0, keepdims=True))
    soft = e / jnp.sum(e, axis=0, keepdims=True)
    lb_all = jnp.zeros((1, width), F32)
    for i in range(1, layer + 1):
        lb_all = lb_all + soft[i:i + 1, :]

    i_idx = lax.broadcasted_iota(jnp.int32, (r, r), 0)
    j_idx = lax.broadcasted_iota(jnp.int32, (r, r), 1)
    level_masks = [((i_idx // (2 * s)) == (j_idx // (2 * s))) & ((i_idx % (2 * s)) >= s)
                   & ((j_idx % (2 * s)) < s) for s in HG_LEVELS]
    diag = i_idx == j_idx

    for b in range(batch):
        for hd in range(HG_HEADS):
            idx = b * HG_HEADS + hd
            hs = slice(hd * HG_DK, (hd + 1) * HG_DK)
            lb = lb_all[:, hs]
            q = _silu(x_ref[b, :, hd * HG_DK:(hd + 1) * HG_DK])
            z = x_ref[b, :, width + hd * HG_DK:width + (hd + 1) * HG_DK]
            v = x_ref[b, :, 2 * width + hd * HG_DV:2 * width + (hd + 1) * HG_DV]
            gate = x_ref[b, :, 3 * width + hd * HG_DV:3 * width + (hd + 1) * HG_DV]
            f = lb + (1.0 - lb) * _sigmoid(z)
            k = (1.0 - lb) * _sigmoid(-z)
            cum = _chunk_cumsum(jnp.log(jnp.maximum(f, MIN_FORGET))) * math.log2(math.e)
            cum_last = _chunk_last(cum)
            qd = q * jnp.exp2(cum)
            kd = k * jnp.exp2(cum_last - cum)
            dl = jnp.exp2(cum_last)
            att = jnp.where(diag, _dot_nt(q, k), 0.0)
            cum_b = cum.astype(BF16)
            for lv, (s, mask) in enumerate(zip(HG_LEVELS, level_masks)):
                if 2 * s >= 8:
                    grp = cum.reshape(r // (2 * s), 2 * s, HG_DK)
                    ref = jnp.broadcast_to(grp[:, s:s + 1, :], grp.shape).reshape(r, HG_DK)
                else:
                    ref = jnp.dot(sel_ref[lv], cum_b, preferred_element_type=F32)
                fac = jnp.exp2(cum - ref)
                att = att + jnp.where(mask, _dot_nt(q * fac, k * (1.0 / fac)), 0.0)
            o_intra = _dot(att, v)
            outs = []
            st = state_ref[idx]
            for c in range(r // CHUNK):
                rc = slice(c * CHUNK, (c + 1) * CHUNK)
                outs.append(_dot_nt(qd[rc], st) + o_intra[rc])
                st = st * dl[c * CHUNK:c * CHUNK + 1, :] + _dot_tn(v[rc], kd[rc])
            state_ref[idx] = st
            o = jnp.concatenate(outs, axis=0)
            o_ref[b, :, hs] = (_rms_norm(o, onorm_ref[...]) * _silu(gate)).astype(BF16)


def _hg_sel():
    r = SCAN_ROWS
    i = np.arange(r)
    sel = np.zeros((len(HG_LEVELS), r, r), np.float32)
    for lv, s in enumerate(HG_LEVELS):
        sel[lv, i, (i // (2 * s)) * (2 * s) + s] = 1.0
    return jnp.asarray(sel, BF16)


def _hg_call(x3, lbp, onorm, layer):
    batch, s, n = x3.shape
    r = SCAN_ROWS
    width = HG_HEADS * HG_DV
    return pl.pallas_call(
        functools.partial(_hg_kernel, batch=batch, layer=layer),
        out_shape=jax.ShapeDtypeStruct((batch, s, width), BF16),
        grid=(s // r,),
        in_specs=[pl.BlockSpec((batch, r, n), lambda i: (0, i, 0)), _full(lbp.shape),
                  _full((1, HG_DV)), _full((len(HG_LEVELS), r, r))],
        out_specs=pl.BlockSpec((batch, r, width), lambda i: (0, i, 0)),
        scratch_shapes=[pltpu.VMEM((batch * HG_HEADS, HG_DV, HG_DK), F32)],
        compiler_params=_params(("arbitrary",)),
        name="hgrn2",
    )(x3, lbp, onorm, _hg_sel())


def _merge_kernel(h_ref, oa_ref, ob_ref, oc_ref, wgate_ref, wa_ref, wb_ref, wc_ref, wout_ref,
                  g_ref, b_ref, rhi_ref, rlo_ref, rb_ref, ext_ref, info_ref):
    h = h_ref[...]
    gates = _sigmoid(_dot(h, wgate_ref[...]))
    d = D_MODEL
    mixed = (gates[:, :d] * _dot(oa_ref[...], wa_ref[...])
             + gates[:, d:2 * d] * _dot(ob_ref[...], wb_ref[...])
             + gates[:, 2 * d:] * _dot(oc_ref[...], wc_ref[...]))
    h1 = _layer_norm(DEEPNORM_ALPHA * h + _dot(mixed, wout_ref[...]), g_ref[...], b_ref[...])
    hi = h1.astype(BF16)
    lo = (h1 - hi.astype(F32)).astype(BF16)
    logits = (jnp.dot(hi, rhi_ref[...], preferred_element_type=F32)
              + jnp.dot(hi, rlo_ref[...], preferred_element_type=F32)
              + jnp.dot(lo, rhi_ref[...], preferred_element_type=F32)) + rb_ref[...]
    lane = lax.broadcasted_iota(jnp.int32, logits.shape, 1)
    neg = jnp.float32(-jnp.inf)
    big = jnp.int32(1 << 20)
    is_g = lane < N_GROUPS
    gl = jnp.where(is_g, logits, neg)
    gmax = jnp.max(gl, axis=-1, keepdims=True)
    gidx = jnp.min(jnp.where(gl == gmax, lane, big), axis=-1, keepdims=True)
    p_group = 1.0 / jnp.sum(jnp.where(is_g, jnp.exp(gl - gmax), 0.0), axis=-1, keepdims=True)
    lo_lane = N_GROUPS + gidx * EXPERTS_PER_GROUP
    in_g = (lane >= lo_lane) & (lane < lo_lane + EXPERTS_PER_GROUP)
    el = jnp.where(in_g, logits, neg)
    v1 = jnp.max(el, axis=-1, keepdims=True)
    i1 = jnp.min(jnp.where(el == v1, lane, big), axis=-1, keepdims=True)
    el2 = jnp.where(lane == i1, neg, el)
    v2 = jnp.max(el2, axis=-1, keepdims=True)
    i2 = jnp.min(jnp.where(el2 == v2, lane, big), axis=-1, keepdims=True)
    e21 = jnp.exp(v2 - v1)
    w1 = p_group / (1.0 + e21)
    w2 = p_group * e21 / (1.0 + e21)
    info = (jnp.where(lane == i1 - lo_lane, w1, 0.0) + jnp.where(lane == i2 - lo_lane, w2, 0.0)
            + jnp.where(lane == EXPERTS_PER_GROUP, gidx.astype(F32), 0.0))
    info_ref[...] = info
    ext_ref[:, :D_MODEL] = h1
    ext_ref[:, D_MODEL:] = info


def _merge_call(h, oa, ob, oc, w, tm=512):
    t, d = oa.shape[0], h.shape[1]
    row = lambda n: pl.BlockSpec((tm, n), lambda i: (i, 0))
    wa, wb, wc = w["wbr_a"], w["wbr_b"], w["wbr_c"]
    return pl.pallas_call(
        _merge_kernel,
        out_shape=(jax.ShapeDtypeStruct((t, EXT_COLS), F32),
                   jax.ShapeDtypeStruct((t, LANES), F32)),
        grid=(t // tm,),
        in_specs=[row(d), row(oa.shape[1]), row(ob.shape[1]), row(oc.shape[1]),
                  _full((d, 3 * d)), _full(wa.shape), _full(wb.shape), _full(wc.shape),
                  _full((d, d)), _full((1, d)), _full((1, d)),
                  _full((d, LANES)), _full((d, LANES)), _full((1, LANES))],
        out_specs=(row(EXT_COLS), row(LANES)),
        compiler_params=_params(("parallel",)),
        name="merge_ln_router",
    )(h, oa, ob, oc, w["wgate"], wa, wb, wc, w["wout"], w["ln1_g"], w["ln1_b"],
      w["r_hi"], w["r_lo"], w["r_b"])


def _rank_kernel(info_ref, route_ref, cnt_ref, base_ref):
    @pl.when(pl.program_id(0) == 0)
    def _():
        base_ref[...] = jnp.zeros(base_ref.shape, F32)

    info = info_ref[...]
    tb = info.shape[0]
    lane = lax.broadcasted_iota(jnp.int32, info.shape, 1)
    gid = info[:, EXPERTS_PER_GROUP:EXPERTS_PER_GROUP + 1]
    onehot = jnp.where((lane.astype(F32) == gid) & (lane < N_GROUPS), 1.0, 0.0)
    r = lax.broadcasted_iota(jnp.int32, (tb, tb), 0)
    c = lax.broadcasted_iota(jnp.int32, (tb, tb), 1)
    earlier = jnp.where(c < r, 1.0, 0.0)
    prefix = _dot(earlier, onehot)
    base = base_ref[...]
    rank = jnp.sum(onehot * (prefix + base), axis=1, keepdims=True)
    route_ref[...] = jnp.where(lane == 0, rank, jnp.where(lane == 1, gid, 0.0))
    base = base + jnp.sum(onehot, axis=0, keepdims=True)
    base_ref[...] = base
    cnt_ref[...] = base


def _rank_call(info, tb=512):
    t = info.shape[0]
    return pl.pallas_call(
        _rank_kernel,
        out_shape=(jax.ShapeDtypeStruct((t, LANES), F32), jax.ShapeDtypeStruct((1, LANES), F32)),
        grid=(t // tb,),
        in_specs=[pl.BlockSpec((tb, LANES), lambda i: (i, 0))],
        out_specs=(pl.BlockSpec((tb, LANES), lambda i: (i, 0)),
                   pl.BlockSpec((1, LANES), lambda i: (0, 0))),
        scratch_shapes=[pltpu.VMEM((1, LANES), F32)],
        compiler_params=_params(("arbitrary",)),
        name="moe_rank",
    )(info)


def _dispatch_plan(route, cnt, t, tm):
    n_tiles = t // tm + N_GROUPS
    rows = n_tiles * tm
    rank = route[:, 0].astype(jnp.int32)
    gid = route[:, 1].astype(jnp.int32)
    counts = cnt[0, :N_GROUPS].astype(jnp.int32)
    padded = (counts + tm - 1) // tm * tm
    ends = jnp.cumsum(padded)
    pos = (ends - padded)[gid] + rank
    tok = jnp.arange(t, dtype=jnp.int32)
    row = jnp.arange(rows, dtype=jnp.int32)
    owner = jnp.full((rows,), -1, jnp.int32).at[pos].set(tok, unique_indices=True,
                                                          mode="promise_in_bounds")
    src = jnp.maximum(owner, 0)
    dst = jnp.where(owner < 0, t + row % tm, owner)
    dst = jnp.concatenate([t + row[:tm], dst])
    tile_start = jnp.arange(n_tiles, dtype=jnp.int32) * tm
    tile_group = jnp.minimum(jnp.sum(tile_start[:, None] >= ends[None, :], axis=1), N_GROUPS - 1)
    return src, dst, tile_group.astype(jnp.int32), n_tiles


def _expert_kernel(src_ref, dst_ref, tg_ref, ext_hbm, wg_ref, wu_ref, wd_ref, g_ref, b_ref, out_hbm,
                   xa, xb, oa, ob, gsem, ssem, *, tm, n_tiles):
    i = pl.program_id(0)
    bufs = ((xa, oa, 0), (xb, ob, 1))

    def gather_copy(tile, buf, r):
        return pltpu.make_async_copy(ext_hbm.at[pl.ds(src_ref[tile * tm + r], 1)],
                                     buf[0].at[pl.ds(r, 1)], gsem.at[buf[2]])

    def scatter_copy(tile, buf, r):
        return pltpu.make_async_copy(buf[1].at[pl.ds(r, 1)],
                                     out_hbm.at[pl.ds(dst_ref[(tile + 1) * tm + r], 1)],
                                     ssem.at[buf[2]])

    def wait_gather(buf):
        pltpu.make_async_copy(ext_hbm.at[pl.ds(0, tm)], buf[0], gsem.at[buf[2]]).wait()

    def wait_scatter(buf):
        pltpu.make_async_copy(buf[1], out_hbm.at[pl.ds(0, tm)], ssem.at[buf[2]]).wait()

    @pl.when(i == 0)
    def _():
        ob[...] = jnp.zeros(ob.shape, F32)
        for r in range(tm):
            gather_copy(0, bufs[0], r).start()

    def tile_body(cur, oth):
        wait_gather(cur)

        @pl.when(i >= 1)
        def _():
            wait_scatter(cur)

        x = cur[0][:, :D_MODEL]
        info = cur[0][:, D_MODEL:]
        xb_ = x.astype(BF16)
        acc = jnp.zeros((tm, D_MODEL), F32)
        per = tm // EXPERTS_PER_GROUP
        nxt = jnp.minimum(i + 1, n_tiles - 1)
        for e in range(EXPERTS_PER_GROUP):
            act = (_silu(jnp.dot(xb_, wg_ref[0, e], preferred_element_type=F32))
                   * jnp.dot(xb_, wu_ref[0, e], preferred_element_type=F32) * info[:, e:e + 1])
            acc = acc + _dot(act, wd_ref[0, e])
            half = EXPERTS_PER_GROUP // 2
            for r in range((e % half) * 2 * per, (e % half + 1) * 2 * per):
                if e < half:
                    gather_copy(nxt, oth, r).start(priority=r % 2)
                else:
                    scatter_copy(i - 1, oth, r).start(priority=r % 2)
        cur[1][...] = _layer_norm(DEEPNORM_ALPHA * x + acc, g_ref[...], b_ref[...])

        @pl.when(i == n_tiles - 1)
        def _():
            wait_scatter(oth)
            for r in range(tm):
                scatter_copy(i, cur, r).start()
            wait_scatter(cur)
            wait_gather(oth)

    @pl.when(i % 2 == 0)
    def _():
        tile_body(bufs[0], bufs[1])

    @pl.when(i % 2 == 1)
    def _():
        tile_body(bufs[1], bufs[0])


def _expert_call(ext, plan, w, tm):
    src, dst, tile_group, n_tiles = plan
    t = ext.shape[0]
    d = D_MODEL
    chunks = d // LANES
    wspec = lambda shape: pl.BlockSpec((1,) + shape, lambda i, s, d_, tg: (tg[i], 0, 0, 0))
    vec = pl.BlockSpec((1, d), lambda i, s, d_, tg: (0, 0))
    grid_spec = pltpu.PrefetchScalarGridSpec(
        num_scalar_prefetch=3,
        grid=(n_tiles,),
        in_specs=[pl.BlockSpec(memory_space=pl.ANY), wspec((EXPERTS_PER_GROUP, d, EXPERT_FF)),
                  wspec((EXPERTS_PER_GROUP, d, EXPERT_FF)), wspec((EXPERTS_PER_GROUP, EXPERT_FF, d)),
                  vec, vec],
        out_specs=pl.BlockSpec(memory_space=pl.ANY),
        scratch_shapes=[pltpu.VMEM((tm, EXT_COLS), F32), pltpu.VMEM((tm, EXT_COLS), F32),
                        pltpu.VMEM((tm, d), F32), pltpu.VMEM((tm, d), F32),
                        pltpu.SemaphoreType.DMA((2,)), pltpu.SemaphoreType.DMA((2,))],
    )
    out = pl.pallas_call(
        functools.partial(_expert_kernel, tm=tm, n_tiles=n_tiles),
        out_shape=jax.ShapeDtypeStruct((t + tm, d), F32),
        grid_spec=grid_spec,
        compiler_params=_params(("arbitrary",)),
        name="moe_experts",
    )(src, dst, tile_group, ext, w["e_gate"], w["e_up"], w["e_down"], w["ln2_g"], w["ln2_b"])
    return out


def _moe_call(ext, info, w, tm=MOE_TILE):
    route, cnt = _rank_call(info)
    return _expert_call(ext, _dispatch_plan(route, cnt, ext.shape[0], tm), w, tm)


def _pad_cols(w, lo, total):
    return jnp.zeros((w.shape[0], total), w.dtype).at[:, lo:lo + w.shape[1]].set(w)


def _layer_weights(l, p):
    d = D_MODEL
    w_in = p["w_in"][l]
    sizes = (MLA_Q_RANK, MLA_KV_RANK, MLA_ROPE, DN_HEADS * (2 * DN_DK + DN_DV), DN_HEADS, DN_HEADS,
             DN_HEADS * DN_DV, HG_HEADS * HG_DK, HG_HEADS * HG_DK, HG_HEADS * HG_DV,
             HG_HEADS * HG_DV, 3 * d)
    offs = np.concatenate([[0], np.cumsum(sizes)])
    col = lambda i: w_in[:, offs[i]:offs[i + 1]]
    qk_w = MLA_NOPE + MLA_ROPE
    uq = p["mla_w_uq"][l].reshape(MLA_Q_RANK, MLA_HEADS, qk_w)
    uq = jnp.pad(uq, ((0, 0), (0, 0), (0, HEAD_PAD - qk_w))).reshape(MLA_Q_RANK, MLA_HEADS * HEAD_PAD)
    ukv = p["mla_w_ukv"][l].reshape(MLA_KV_RANK, MLA_HEADS, MLA_NOPE + MLA_V)
    uk = jnp.pad(ukv[:, :, :MLA_NOPE], ((0, 0), (0, 0), (0, HEAD_PAD - MLA_NOPE)))
    uk = uk.reshape(MLA_KV_RANK, MLA_HEADS * HEAD_PAD)
    uv = jnp.pad(ukv[:, :, MLA_NOPE:], ((0, 0), (0, 0), (0, HEAD_PAD - MLA_V)))
    uv = uv.reshape(MLA_KV_RANK, MLA_HEADS * HEAD_PAD)
    ba = jnp.concatenate([col(4), col(5)], axis=1)
    w_dn = jnp.concatenate([col(3), col(6), _pad_cols(ba, 0, LANES)], axis=1)
    w_hg = jnp.concatenate([col(7), col(8), col(9), col(10)], axis=1)
    router = jnp.concatenate([p["router_group_w"][l], p["router_expert_w"][l]], axis=1)
    router = _pad_cols(router, 0, LANES)
    r_hi = router.astype(BF16)
    r_b = jnp.concatenate([p["router_group_b"][l], p["router_expert_b"][l]])
    bf = lambda a: a.astype(BF16)
    return dict(
        wcq=bf(col(0)), wckv=bf(col(1)), wkr=bf(_pad_cols(col(2), ROPE_LO, LANES)),
        qn=p["mla_q_norm"][l].reshape(1, -1), kvn=p["mla_kv_norm"][l].reshape(1, -1),
        wuq=bf(uq), wuk=bf(uk), wuv=bf(uv),
        w_dn=bf(w_dn), w_hg=bf(w_hg), wgate=bf(col(11)),
        dn_conv=p["dn_conv"][l],
        alog128=_pad_cols(p["dn_a_log"][l].reshape(1, -1), DN_HEADS, LANES),
        dtb128=_pad_cols(p["dn_dt_bias"][l].reshape(1, -1), DN_HEADS, LANES),
        dn_onorm=p["dn_o_norm"][l].reshape(1, -1), hg_onorm=p["hg_o_norm"][l].reshape(1, -1),
        wbr_a=bf(p["w_br_a"][l]), wbr_b=bf(p["w_br_b"][l]), wbr_c=bf(p["w_br_c"][l]),
        wout=bf(p["w_out"][l]),
        ln1_g=p["ln1_g"][l].reshape(1, d), ln1_b=p["ln1_b"][l].reshape(1, d),
        r_hi=r_hi, r_lo=bf(router - r_hi.astype(F32)), r_b=_pad_cols(r_b.reshape(1, -1), 0, LANES),
        e_gate=bf(p["exp_w_gate"][l]).reshape(N_GROUPS, EXPERTS_PER_GROUP, d, EXPERT_FF),
        e_up=bf(p["exp_w_up"][l]).reshape(N_GROUPS, EXPERTS_PER_GROUP, d, EXPERT_FF),
        e_down=bf(p["exp_w_down"][l]).reshape(N_GROUPS, EXPERTS_PER_GROUP, EXPERT_FF, d),
        ln2_g=p["ln2_g"][l].reshape(1, d), ln2_b=p["ln2_b"][l].reshape(1, d),
    )


def _forward(p):
    x = p["x"]
    batch, s, d = x.shape
    t = batch * s
    half = MLA_ROPE // 2
    inv_freq = ROPE_BASE ** (-jnp.arange(half, dtype=F32) / half)
    inv128 = _pad_cols(jnp.concatenate([inv_freq, inv_freq]).reshape(1, -1), ROPE_LO, LANES)
    tables = _rope_call(p["positions"].reshape(t, 1), inv128)
    h = _ln_call(x.reshape(t, d), p["ln_in_g"], p["ln_in_b"])
    for l in range(DEPTH):
        w = _layer_weights(l, p)
        q, k, v = _mla_proj_call(h, t, w, tables)
        oa = _flash_call(q, k, v, batch)
        x_dn = _proj_call(h, t, w["w_dn"], "proj_deltanet")
        ob = _dn_call(x_dn.reshape(batch, s, -1), w["dn_conv"], w["alog128"], w["dtb128"],
                      w["dn_onorm"])
        x_hg = _proj_call(h, t, w["w_hg"], "proj_hgrn2")
        oc = _hg_call(x_hg.reshape(batch, s, -1), p["hg_lower_bounds"], w["hg_onorm"], l)
        ext, info = _merge_call(h, oa, ob.reshape(t, -1), oc.reshape(t, -1), w)
        h = _moe_call(ext, info, w)
    return h[:t].reshape(batch, s, d)


def kernel(x, positions, ln_in_g, ln_in_b, hg_lower_bounds, w_in, mla_q_norm, mla_w_uq, mla_kv_norm, mla_w_ukv, dn_conv, dn_a_log, dn_dt_bias, dn_o_norm, hg_o_norm, w_br_a, w_br_b, w_br_c, w_out, ln1_g, ln1_b, router_group_w, router_group_b, router_expert_w, router_expert_b, exp_w_gate, exp_w_up, exp_w_down, ln2_g, ln2_b):
    return _forward(dict(
        x=x, positions=positions, ln_in_g=ln_in_g, ln_in_b=ln_in_b,
        hg_lower_bounds=hg_lower_bounds, w_in=w_in, mla_q_norm=mla_q_norm, mla_w_uq=mla_w_uq,
        mla_kv_norm=mla_kv_norm, mla_w_ukv=mla_w_ukv, dn_conv=dn_conv, dn_a_log=dn_a_log,
        dn_dt_bias=dn_dt_bias, dn_o_norm=dn_o_norm, hg_o_norm=hg_o_norm, w_br_a=w_br_a,
        w_br_b=w_br_b, w_br_c=w_br_c, w_out=w_out, ln1_g=ln1_g, ln1_b=ln1_b,
        router_group_w=router_group_w, router_group_b=router_group_b,
        router_expert_w=router_expert_w, router_expert_b=router_expert_b,
        exp_w_gate=exp_w_gate, exp_w_up=exp_w_up, exp_w_down=exp_w_down, ln2_g=ln2_g, ln2_b=ln2_b))
```

```python
import functools
import math

import jax
import jax.numpy as jnp
import numpy as np
from jax import lax
from jax.experimental import pallas as pl
from jax.experimental.pallas import tpu as pltpu

D_MODEL = 1024
DEPTH = 2
MLA_HEADS = 8
MLA_Q_RANK = 256
MLA_KV_RANK = 128
MLA_NOPE = 64
MLA_ROPE = 32
MLA_V = 64
ROPE_BASE = 10000.0
MASK_VALUE = -1e30
DN_HEADS = 4
DN_DK = 128
DN_DV = 128
DN_CONV = 4
HG_HEADS = 4
HG_DK = 128
HG_DV = 128
CHUNK = 64
MIN_FORGET = 1e-30
N_GROUPS = 4
EXPERTS_PER_GROUP = 8
EXPERT_FF = 256
DEEPNORM_ALPHA = (2 * DEPTH) ** 0.25
NORM_EPS = 1e-6

LANES = 128
SUBLANES = 8
HEAD_PAD = 128
ROPE_LO = MLA_NOPE
ROPE_MID = MLA_NOPE + MLA_ROPE // 2
ROPE_HI = MLA_NOPE + MLA_ROPE
EXT_COLS = D_MODEL + LANES

ROW_TILE = 1024
PROJ_TILE = 512
ATTN_TILE = 512
SCAN_ROWS = 256
MOE_TILE = 512
VMEM_LIMIT = 56 * 1024 * 1024

BF16 = jnp.bfloat16
F32 = jnp.float32


def _dot(a, b):
    return jnp.dot(a.astype(BF16), b.astype(BF16), preferred_element_type=F32)


def _dot_nt(a, b):
    return lax.dot_general(a.astype(BF16), b.astype(BF16), (((1,), (1,)), ((), ())),
                           preferred_element_type=F32)


def _dot_tn(a, b):
    return lax.dot_general(a.astype(BF16), b.astype(BF16), (((0,), (0,)), ((), ())),
                           preferred_element_type=F32)


def _sigmoid(x):
    return 1.0 / (1.0 + jnp.exp(-x))


def _silu(x):
    return x * _sigmoid(x)


def _layer_norm(x, g, b):
    mu = jnp.mean(x, axis=-1, keepdims=True)
    xc = x - mu
    var = jnp.mean(xc * xc, axis=-1, keepdims=True)
    return xc * lax.rsqrt(var + NORM_EPS) * g + b


def _rms_norm(x, g):
    return x * lax.rsqrt(jnp.mean(x * x, axis=-1, keepdims=True) + NORM_EPS) * g


def _params(sem):
    return pltpu.CompilerParams(dimension_semantics=sem, vmem_limit_bytes=VMEM_LIMIT)


def _full(shape):
    n = len(shape)
    return pl.BlockSpec(shape, lambda *_: (0,) * n)


def _ln_kernel(x_ref, g_ref, b_ref, o_ref):
    o_ref[...] = _layer_norm(x_ref[...], g_ref[...], b_ref[...])


def _ln_call(x, g, b, tm=ROW_TILE):
    t, d = x.shape
    return pl.pallas_call(
        _ln_kernel,
        out_shape=jax.ShapeDtypeStruct((t, d), F32),
        grid=(t // tm,),
        in_specs=[pl.BlockSpec((tm, d), lambda i: (i, 0)), _full((1, d)), _full((1, d))],
        out_specs=pl.BlockSpec((tm, d), lambda i: (i, 0)),
        compiler_params=_params(("parallel",)),
        name="ln_in",
    )(x, g.reshape(1, d), b.reshape(1, d))


def _rope_kernel(pos_ref, inv_ref, c_ref, sa_ref, sb_ref):
    ang = pos_ref[...].astype(F32) * inv_ref[...]
    lane = lax.broadcasted_iota(jnp.int32, ang.shape, 1)
    sin = jnp.sin(ang)
    c_ref[...] = jnp.cos(ang)
    sa_ref[...] = jnp.where((lane >= ROPE_MID) & (lane < ROPE_HI), sin, 0.0)
    sb_ref[...] = jnp.where((lane >= ROPE_LO) & (lane < ROPE_MID), -sin, 0.0)


def _rope_call(pos_col, inv128, tm=ROW_TILE):
    t = pos_col.shape[0]
    spec = pl.BlockSpec((tm, LANES), lambda i: (i, 0))
    shp = jax.ShapeDtypeStruct((t, LANES), F32)
    return pl.pallas_call(
        _rope_kernel,
        out_shape=(shp, shp, shp),
        grid=(t // tm,),
        in_specs=[pl.BlockSpec((tm, 1), lambda i: (i, 0)), _full((1, LANES))],
        out_specs=(spec, spec, spec),
        compiler_params=_params(("parallel",)),
        name="rope_tables",
    )(pos_col, inv128)


def _rope_apply(x, c, sa, sb):
    return x * c + pltpu.roll(x, 16, 1) * sa + pltpu.roll(x, LANES - 16, 1) * sb


def _mla_proj_kernel(h_ref, wcq_ref, wckv_ref, wkr_ref, qn_ref, kvn_ref, wuq_ref, wuk_ref,
                     wuv_ref, c_ref, sa_ref, sb_ref, qt_ref, k_ref, vt_ref):
    hb = h_ref[...].astype(BF16)
    cq = _rms_norm(_dot(hb, wcq_ref[...]), qn_ref[...])
    ckv = _rms_norm(_dot(hb, wckv_ref[...]), kvn_ref[...])
    kr = _dot(hb, wkr_ref[...])
    q = _dot(cq, wuq_ref[...])
    k = _dot(ckv, wuk_ref[...])
    v = _dot(ckv, wuv_ref[...])
    lane = lax.broadcasted_iota(jnp.int32, (1, HEAD_PAD), 1)
    one_lane = jnp.where(lane == MLA_V, 1.0, 0.0)
    for hd in range(MLA_HEADS):
        sl = slice(hd * HEAD_PAD, (hd + 1) * HEAD_PAD)
        vt_ref[0, hd] = (v[:, sl] + one_lane).T.astype(BF16)
    c, sa, sb = c_ref[...], sa_ref[...], sb_ref[...]
    scale = (MLA_NOPE + MLA_ROPE) ** -0.5 * math.log2(math.e)
    kr = _rope_apply(kr, c, sa, sb)
    for hd in range(MLA_HEADS):
        sl = slice(hd * HEAD_PAD, (hd + 1) * HEAD_PAD)
        qt_ref[0, sl, :] = (_rope_apply(q[:, sl], c, sa, sb) * scale).T.astype(BF16)
        k_ref[:, sl] = (k[:, sl] + kr).astype(BF16)


def _mla_proj_call(h, t, w, tables, tm=ATTN_TILE):
    d = h.shape[1]
    hq = MLA_HEADS * HEAD_PAD
    row = lambda n: pl.BlockSpec((tm, n), lambda i: (i, 0))
    tile = lambda n: pl.BlockSpec((1, n, tm), lambda i: (i, 0, 0))
    return pl.pallas_call(
        _mla_proj_kernel,
        out_shape=(jax.ShapeDtypeStruct((t // tm, hq, tm), BF16),
                   jax.ShapeDtypeStruct((t, hq), BF16),
                   jax.ShapeDtypeStruct((t // tm, MLA_HEADS, HEAD_PAD, tm), BF16)),
        grid=(t // tm,),
        in_specs=[row(d), _full((d, MLA_Q_RANK)), _full((d, MLA_KV_RANK)), _full((d, LANES)),
                  _full((1, MLA_Q_RANK)), _full((1, MLA_KV_RANK)), _full((MLA_Q_RANK, hq)),
                  _full((MLA_KV_RANK, hq)), _full((MLA_KV_RANK, hq)),
                  row(LANES), row(LANES), row(LANES)],
        out_specs=(tile(hq), row(hq),
                   pl.BlockSpec((1, MLA_HEADS, HEAD_PAD, tm), lambda i: (i, 0, 0, 0))),
        compiler_params=_params(("parallel",)),
        name="mla_proj",
    )(h, w["wcq"], w["wckv"], w["wkr"], w["qn"], w["kvn"], w["wuq"], w["wuk"], w["wuv"], *tables)


def _flash_kernel(qt_ref, k_ref, vt_ref, o_ref, m0, a0, m1, a1, sa_ref, sb_ref, *, tk):
    qi = pl.program_id(2)
    stats = ((m0, a0), (m1, a1))
    for m_ref, acc_ref in stats:
        m_ref[...] = jnp.full(m_ref.shape, MASK_VALUE, F32)
        acc_ref[...] = jnp.zeros(acc_ref.shape, F32)

    def score(j, s_ref, lo):
        rows = pl.ds(pl.multiple_of(j * tk, tk), tk)
        for hh in range(2):
            sl = slice(hh * HEAD_PAD, (hh + 1) * HEAD_PAD)
            qs = [qt_ref[0, u, sl, :] for u in range(lo // tk, 2)]
            qh = qs[0] if len(qs) == 1 else jnp.concatenate(qs, axis=1)
            s_ref[hh, :, lo:] = jnp.dot(k_ref[0, rows, sl], qh, preferred_element_type=F32)

    def consume(j, s_ref, lo, masked):
        for hh, (m_ref, acc_ref) in enumerate(stats):
            vt = vt_ref[0, j, 0, hh]
            st = s_ref[hh, :, lo:]
            if masked:
                key = lax.broadcasted_iota(jnp.int32, st.shape, 0)
                qry = lax.broadcasted_iota(jnp.int32, st.shape, 1)
                st = jnp.where(key <= qry, st, MASK_VALUE)
            m_old = m_ref[:, lo:]
            m_new = jnp.maximum(m_old, jnp.max(st, axis=0, keepdims=True))
            alpha = jnp.exp2(m_old - m_new)
            p = jnp.exp2(st - m_new)
            acc_ref[:, lo:] = alpha * acc_ref[:, lo:] + jnp.dot(vt, p.astype(BF16),
                                                                preferred_element_type=F32)
            m_ref[:, lo:] = m_new

    score(0, sa_ref, 0)

    def body(i, carry):
        j = 2 * i
        score(j + 1, sb_ref, 0)
        consume(j, sa_ref, 0, False)
        score(j + 2, sa_ref, 0)
        consume(j + 1, sb_ref, 0, False)
        return carry

    lax.fori_loop(0, qi, body, 0)
    score(2 * qi + 1, sb_ref, tk)
    consume(2 * qi, sa_ref, 0, True)
    consume(2 * qi + 1, sb_ref, tk, True)
    ot = jnp.concatenate([acc_ref[:MLA_V, :] * (1.0 / acc_ref[MLA_V:MLA_V + 1, :])
                          for _, acc_ref in stats], axis=0)
    o_ref[0] = ot.T.astype(BF16)


def _flash_call(qt, k, vt, batch, tk=ATTN_TILE):
    nt, hq, _ = qt.shape
    t = nt * tk
    s = t // batch
    ns = s // tk
    tq = 2 * tk
    pairs = MLA_HEADS // 2
    hv = MLA_HEADS * MLA_V
    qt5 = qt.reshape(batch, ns, hq, tk)
    k3 = k.reshape(batch, s, hq)
    vt6 = vt.reshape(batch, ns, pairs, 2, HEAD_PAD, tk)
    stat = lambda: pltpu.VMEM((1, tq), F32)
    acc = lambda: pltpu.VMEM((HEAD_PAD, tq), F32)
    out = pl.pallas_call(
        functools.partial(_flash_kernel, tk=tk),
        out_shape=jax.ShapeDtypeStruct((batch, s, hv), BF16),
        grid=(batch, pairs, s // tq),
        in_specs=[pl.BlockSpec((1, 2, 2 * HEAD_PAD, tk), lambda b, p, i: (b, i, p, 0)),
                  pl.BlockSpec((1, s, 2 * HEAD_PAD), lambda b, p, i: (b, 0, p)),
                  pl.BlockSpec((1, ns, 1, 2, HEAD_PAD, tk),
                               lambda b, p, i: (b, 0, p, 0, 0, 0))],
        out_specs=pl.BlockSpec((1, tq, 2 * MLA_V), lambda b, p, i: (b, i, p)),
        scratch_shapes=[stat(), acc(), stat(), acc(),
                        pltpu.VMEM((2, tk, tq), F32), pltpu.VMEM((2, tk, tq), F32)],
        compiler_params=_params(("parallel", "parallel", "arbitrary")),
        name="flash_attn",
    )(qt5, k3, vt6)
    return out.reshape(t, hv)


def _proj_kernel(h_ref, w_ref, o_ref):
    o_ref[...] = _dot(h_ref[...], w_ref[...])


def _proj_call(h, t, w, name, tm=PROJ_TILE):
    d = h.shape[1]
    n = w.shape[1]
    return pl.pallas_call(
        _proj_kernel,
        out_shape=jax.ShapeDtypeStruct((t, n), F32),
        grid=(t // tm,),
        in_specs=[pl.BlockSpec((tm, d), lambda i: (i, 0)), _full((d, n))],
        out_specs=pl.BlockSpec((tm, n), lambda i: (i, 0)),
        compiler_params=_params(("parallel",)),
        name=name,
    )(h, w)


def _chunk_cumsum(x):
    rows = lax.broadcasted_iota(jnp.int32, x.shape, 0) % CHUNK
    shift = 1
    while shift < CHUNK:
        x = x + jnp.where(rows >= shift, pltpu.roll(x, shift, 0), 0.0)
        shift *= 2
    return x


def _chunk_last(x):
    r, n = x.shape
    x3 = x.reshape(r // CHUNK, CHUNK, n)
    return jnp.broadcast_to(x3[:, CHUNK - 1:CHUNK, :], x3.shape).reshape(r, n)


def _chunk_masks(r):
    i = lax.broadcasted_iota(jnp.int32, (r, r), 0)
    j = lax.broadcasted_iota(jnp.int32, (r, r), 1)
    same = (i // CHUNK) == (j // CHUNK)
    return same & (j <= i), same & (j < i), i == j


def _dn_solve(heads, eye):
    invs = [eye - p["m"] for p in heads]
    pws = [p["m"].astype(BF16) for p in heads]
    for _ in range(5):
        pws = [jnp.dot(pw, pw, preferred_element_type=F32).astype(BF16) for pw in pws]
        invs = [inv + jnp.dot(inv.astype(BF16), pw, preferred_element_type=F32)
                for inv, pw in zip(invs, pws)]
    for p, inv in zip(heads, invs):
        uw = jnp.dot(inv.astype(BF16), p["rhs"], preferred_element_type=F32).astype(BF16)
        quw = jnp.dot(p["qk"].astype(BF16), uw, preferred_element_type=F32)
        p["o_own"] = quw[:, :DN_DV]
        q_eff = (p["qd"] - quw[:, DN_DV:]).astype(BF16)
        kd = p["kd"].astype(BF16)
        p["lhs"], p["add"] = [], []
        for c in range(uw.shape[0] // CHUNK):
            rc = slice(c * CHUNK, (c + 1) * CHUNK)
            kuw = lax.dot_general(kd[rc], uw[rc], (((0,), (0,)), ((), ())),
                                  preferred_element_type=F32)
            p["add"].append(kuw[:, :DN_DV])
            p["lhs"].append(jnp.concatenate([kuw[:, DN_DV:].astype(BF16), q_eff[rc]], axis=0))


def _dn_kernel(x_ref, conv_ref, alog_ref, dtb_ref, onorm_ref, o_ref, halo_ref, state_ref,
               *, batch):
    r = SCAN_ROWS
    nqkv = DN_HEADS * (2 * DN_DK + DN_DV)
    gate_off = nqkv
    ba_off = nqkv + DN_HEADS * DN_DV

    @pl.when(pl.program_id(0) == 0)
    def _():
        halo_ref[...] = jnp.zeros(halo_ref.shape, F32)
        state_ref[...] = jnp.zeros(state_ref.shape, F32)

    incl, strict, diag = _chunk_masks(r)
    eye = jnp.where(diag, 1.0, 0.0)
    cw = conv_ref[...]
    pre = []
    for b in range(batch):
        xq = x_ref[b, :, :nqkv]
        xe = jnp.concatenate([halo_ref[b], xq], axis=0)
        halo_ref[b] = xq[r - SUBLANES:, :]
        y = xe[SUBLANES:, :] * cw[DN_CONV - 1:DN_CONV, :]
        for i in range(DN_CONV - 1):
            y = y + pltpu.roll(xe, DN_CONV - 1 - i, 0)[SUBLANES:, :] * cw[i:i + 1, :]
        y = _silu(y)
        ba = x_ref[b, :, ba_off:ba_off + LANES]
        beta_all = _sigmoid(ba)
        av = ba + dtb_ref[...]
        softplus = jnp.maximum(av, 0.0) + jnp.log(1.0 + jnp.exp(-jnp.abs(av)))
        g_all = _chunk_cumsum(-jnp.exp(alog_ref[...]) * softplus) * math.log2(math.e)
        g_all_t = g_all.T
        g_last_all = _chunk_last(g_all)
        for hd in range(DN_HEADS):
            q = y[:, hd * DN_DK:(hd + 1) * DN_DK]
            k = y[:, DN_HEADS * DN_DK + hd * DN_DK:DN_HEADS * DN_DK + (hd + 1) * DN_DK]
            v = y[:, 2 * DN_HEADS * DN_DK + hd * DN_DV:2 * DN_HEADS * DN_DK + (hd + 1) * DN_DV]
            q = q * lax.rsqrt(jnp.sum(q * q, axis=-1, keepdims=True) + NORM_EPS) * DN_DK ** -0.5
            k = k * lax.rsqrt(jnp.sum(k * k, axis=-1, keepdims=True) + NORM_EPS)
            beta = beta_all[:, hd:hd + 1]
            g_col = g_all[:, DN_HEADS + hd:DN_HEADS + hd + 1]
            g_row = g_all_t[DN_HEADS + hd:DN_HEADS + hd + 1, :]
            g_last = g_last_all[:, DN_HEADS + hd:DN_HEADS + hd + 1]
            decay = jnp.where(incl, jnp.exp2(jnp.where(incl, g_col - g_row, 0.0)), 0.0)
            kb = k * beta
            kk = _dot_nt(kb, k)
            m = jnp.where(strict, kk * decay, 0.0)
            qk = jnp.where(incl, _dot_nt(q, k) * decay, 0.0)
            eg = jnp.exp2(g_col)
            pre.append(dict(m=m, rhs=jnp.concatenate([v * beta, kb * eg], axis=1).astype(BF16),
                            qd=q * eg, kd=k * jnp.exp2(g_last - g_col), qk=qk,
                            gl=jnp.exp2(g_last)))
        _dn_solve(pre[b * DN_HEADS:], eye)
    outs = [[] for _ in pre]
    for c in range(r // CHUNK):
        rc = slice(c * CHUNK, (c + 1) * CHUNK)
        for idx, p in enumerate(pre):
            st = state_ref[idx]
            prod = jnp.dot(p["lhs"][c], st.astype(BF16), preferred_element_type=F32)
            outs[idx].append(prod[DN_DK:] + p["o_own"][rc])
            state_ref[idx] = (st * p["gl"][c * CHUNK:c * CHUNK + 1, :] - prod[:DN_DK]
                              + p["add"][c])
    for idx in range(len(pre)):
        b, hd = divmod(idx, DN_HEADS)
        o = jnp.concatenate(outs[idx], axis=0)
        gate = x_ref[b, :, gate_off + hd * DN_DV:gate_off + (hd + 1) * DN_DV]
        o_ref[b, :, hd * DN_DV:(hd + 1) * DN_DV] = (
            _rms_norm(o, onorm_ref[...]) * _silu(gate)).astype(BF16)


def _dn_call(x3, conv_w, alog128, dtb128, onorm):
    batch, s, n = x3.shape
    r = SCAN_ROWS
    nqkv = DN_HEADS * (2 * DN_DK + DN_DV)
    width = DN_HEADS * DN_DV
    return pl.pallas_call(
        functools.partial(_dn_kernel, batch=batch),
        out_shape=jax.ShapeDtypeStruct((batch, s, width), BF16),
        grid=(s // r,),
        in_specs=[pl.BlockSpec((batch, r, n), lambda i: (0, i, 0)), _full((DN_CONV, nqkv)),
                  _full((1, LANES)), _full((1, LANES)), _full((1, DN_DV))],
        out_specs=pl.BlockSpec((batch, r, width), lambda i: (0, i, 0)),
        scratch_shapes=[pltpu.VMEM((batch, SUBLANES, nqkv), F32),
                        pltpu.VMEM((batch * DN_HEADS, DN_DK, DN_DV), F32)],
        compiler_params=_params(("arbitrary",)),
        name="deltanet",
    )(x3, conv_w, alog128, dtb128, onorm)


HG_LEVELS = (32, 16, 8, 4, 2, 1)
HG_SEL_LEVELS = tuple(s for s in HG_LEVELS if 2 * s < SUBLANES)


def _hg_kernel(x_ref, lbp_ref, onorm_ref, sel_ref, o_ref, state_ref, *, batch, layer):
    r = SCAN_ROWS
    width = HG_HEADS * HG_DK

    @pl.when(pl.program_id(0) == 0)
    def _():
        state_ref[...] = jnp.zeros(state_ref.shape, F32)

    lbp = lbp_ref[...]
    e = jnp.exp(lbp - jnp.max(lbp, axis=0, keepdims=True))
    soft = e / jnp.sum(e, axis=0, keepdims=True)
    lb_all = jnp.zeros((1, width), F32)
    for i in range(1, layer + 1):
        lb_all = lb_all + soft[i:i + 1, :]

    i_idx = lax.broadcasted_iota(jnp.int32, (r, r), 0)
    j_idx = lax.broadcasted_iota(jnp.int32, (r, r), 1)
    level_masks = [((i_idx // (2 * s)) == (j_idx // (2 * s))) & ((i_idx % (2 * s)) >= s)
                   & ((j_idx % (2 * s)) < s) for s in HG_LEVELS]
    diag = i_idx == j_idx

    for b in range(batch):
        for hd in range(HG_HEADS):
            idx = b * HG_HEADS + hd
            hs = slice(hd * HG_DK, (hd + 1) * HG_DK)
            lb = lb_all[:, hs]
            q = _silu(x_ref[b, :, hd * HG_DK:(hd + 1) * HG_DK])
            z = x_ref[b, :, width + hd * HG_DK:width + (hd + 1) * HG_DK]
            v = x_ref[b, :, 2 * width + hd * HG_DV:2 * width + (hd + 1) * HG_DV]
            gate = x_ref[b, :, 3 * width + hd * HG_DV:3 * width + (hd + 1) * HG_DV]
            f = lb + (1.0 - lb) * _sigmoid(z)
            k = (1.0 - lb) * _sigmoid(-z)
            cum = _chunk_cumsum(jnp.log(jnp.maximum(f, MIN_FORGET))) * math.log2(math.e)
            cum_last = _chunk_last(cum)
            qd = q * jnp.exp2(cum)
            kd = k * jnp.exp2(cum_last - cum)
            dl = jnp.exp2(cum_last)
            att = jnp.where(diag, _dot_nt(q, k), 0.0)
            cum_b = cum.astype(BF16)
            for s, mask in zip(HG_LEVELS, level_masks):
                if s in HG_SEL_LEVELS:
                    ref = jnp.dot(sel_ref[HG_SEL_LEVELS.index(s)], cum_b,
                                  preferred_element_type=F32)
                else:
                    grp = cum.reshape(r // (2 * s), 2 * s, HG_DK)
                    ref = jnp.broadcast_to(grp[:, s:s + 1, :], grp.shape).reshape(r, HG_DK)
                fac = jnp.exp2(cum - ref)
                att = att + jnp.where(mask, _dot_nt(q * fac, k * (1.0 / fac)), 0.0)
            o_intra = _dot(att, v)
            outs = []
            st = state_ref[idx]
            for c in range(r // CHUNK):
                rc = slice(c * CHUNK, (c + 1) * CHUNK)
                outs.append(_dot_nt(qd[rc], st) + o_intra[rc])
                st = st * dl[c * CHUNK:c * CHUNK + 1, :] + _dot_tn(v[rc], kd[rc])
            state_ref[idx] = st
            o = jnp.concatenate(outs, axis=0)
            o_ref[b, :, hs] = (_rms_norm(o, onorm_ref[...]) * _silu(gate)).astype(BF16)


def _hg_sel():
    r = SCAN_ROWS
    i = np.arange(r)
    sel = np.zeros((len(HG_SEL_LEVELS), r, r), np.float32)
    for lv, s in enumerate(HG_SEL_LEVELS):
        sel[lv, i, (i // (2 * s)) * (2 * s) + s] = 1.0
    return jnp.asarray(sel, BF16)


def _hg_call(x3, lbp, onorm, layer):
    batch, s, n = x3.shape
    r = SCAN_ROWS
    width = HG_HEADS * HG_DV
    return pl.pallas_call(
        functools.partial(_hg_kernel, batch=batch, layer=layer),
        out_shape=jax.ShapeDtypeStruct((batch, s, width), BF16),
        grid=(s // r,),
        in_specs=[pl.BlockSpec((batch, r, n), lambda i: (0, i, 0)), _full(lbp.shape),
                  _full((1, HG_DV)), _full((len(HG_SEL_LEVELS), r, r))],
        out_specs=pl.BlockSpec((batch, r, width), lambda i: (0, i, 0)),
        scratch_shapes=[pltpu.VMEM((batch * HG_HEADS, HG_DV, HG_DK), F32)],
        compiler_params=_params(("arbitrary",)),
        name="hgrn2",
    )(x3, lbp, onorm, _hg_sel())


def _merge_kernel(h_ref, oa_ref, ob_ref, oc_ref, wgate_ref, wa_ref, wb_ref, wc_ref, wout_ref,
                  g_ref, b_ref, rcat_ref, rb_ref, ext_ref, info_ref):
    h = h_ref[...]
    gates = _sigmoid(_dot(h, wgate_ref[...]))
    d = D_MODEL
    mixed = (gates[:, :d] * _dot(oa_ref[...], wa_ref[...])
             + gates[:, d:2 * d] * _dot(ob_ref[...], wb_ref[...])
             + gates[:, 2 * d:] * _dot(oc_ref[...], wc_ref[...]))
    h1 = _layer_norm(DEEPNORM_ALPHA * h + _dot(mixed, wout_ref[...]), g_ref[...], b_ref[...])
    hi = h1.astype(BF16)
    lo = (h1 - hi.astype(F32)).astype(BF16)
    both = jnp.dot(hi, rcat_ref[...], preferred_element_type=F32)
    logits = (both[:, :LANES] + both[:, LANES:]
              + jnp.dot(lo, rcat_ref[:, :LANES], preferred_element_type=F32)) + rb_ref[...]
    lane = lax.broadcasted_iota(jnp.int32, logits.shape, 1)
    neg = jnp.float32(-jnp.inf)
    big = jnp.int32(1 << 20)
    is_g = lane < N_GROUPS
    gl = jnp.where(is_g, logits, neg)
    gmax = jnp.max(gl, axis=-1, keepdims=True)
    gidx = jnp.min(jnp.where(gl == gmax, lane, big), axis=-1, keepdims=True)
    p_group = 1.0 / jnp.sum(jnp.where(is_g, jnp.exp(gl - gmax), 0.0), axis=-1, keepdims=True)
    lo_lane = N_GROUPS + gidx * EXPERTS_PER_GROUP
    in_g = (lane >= lo_lane) & (lane < lo_lane + EXPERTS_PER_GROUP)
    el = jnp.where(in_g, logits, neg)
    v1 = jnp.max(el, axis=-1, keepdims=True)
    i1 = jnp.min(jnp.where(el == v1, lane, big), axis=-1, keepdims=True)
    el2 = jnp.where(lane == i1, neg, el)
    v2 = jnp.max(el2, axis=-1, keepdims=True)
    i2 = jnp.min(jnp.where(el2 == v2, lane, big), axis=-1, keepdims=True)
    e21 = jnp.exp(v2 - v1)
    w1 = p_group / (1.0 + e21)
    w2 = p_group * e21 / (1.0 + e21)
    info = (jnp.where(lane == i1 - lo_lane, w1, 0.0) + jnp.where(lane == i2 - lo_lane, w2, 0.0)
            + jnp.where(lane == EXPERTS_PER_GROUP, gidx.astype(F32), 0.0))
    info_ref[...] = info
    ext_ref[:, :D_MODEL] = h1
    ext_ref[:, D_MODEL:] = info


def _merge_call(h, oa, ob, oc, w, tm=PROJ_TILE):
    t, d = oa.shape[0], h.shape[1]
    row = lambda n: pl.BlockSpec((tm, n), lambda i: (i, 0))
    wa, wb, wc = w["wbr_a"], w["wbr_b"], w["wbr_c"]
    return pl.pallas_call(
        _merge_kernel,
        out_shape=(jax.ShapeDtypeStruct((t, EXT_COLS), F32),
                   jax.ShapeDtypeStruct((t, LANES), F32)),
        grid=(t // tm,),
        in_specs=[row(d), row(oa.shape[1]), row(ob.shape[1]), row(oc.shape[1]),
                  _full((d, 3 * d)), _full(wa.shape), _full(wb.shape), _full(wc.shape),
                  _full((d, d)), _full((1, d)), _full((1, d)),
                  _full((d, 2 * LANES)), _full((1, LANES))],
        out_specs=(row(EXT_COLS), row(LANES)),
        compiler_params=_params(("parallel",)),
        name="merge_ln_router",
    )(h, oa, ob, oc, w["wgate"], wa, wb, wc, w["wout"], w["ln1_g"], w["ln1_b"],
      w["r_cat"], w["r_b"])


def _rank_kernel(info_ref, route_ref, cnt_ref, base_ref):
    @pl.when(pl.program_id(0) == 0)
    def _():
        base_ref[...] = jnp.zeros(base_ref.shape, F32)

    info = info_ref[...]
    tb = info.shape[0]
    lane = lax.broadcasted_iota(jnp.int32, info.shape, 1)
    gid = info[:, EXPERTS_PER_GROUP:EXPERTS_PER_GROUP + 1]
    onehot = jnp.where((lane.astype(F32) == gid) & (lane < N_GROUPS), 1.0, 0.0)
    r = lax.broadcasted_iota(jnp.int32, (tb, tb), 0)
    c = lax.broadcasted_iota(jnp.int32, (tb, tb), 1)
    earlier = jnp.where(c < r, 1.0, 0.0)
    prefix = _dot(earlier, onehot)
    base = base_ref[...]
    rank = jnp.sum(onehot * (prefix + base), axis=1, keepdims=True)
    route_ref[...] = jnp.where(lane == 0, rank, jnp.where(lane == 1, gid, 0.0))
    base = base + jnp.sum(onehot, axis=0, keepdims=True)
    base_ref[...] = base
    cnt_ref[...] = base


def _rank_call(info, tb=PROJ_TILE):
    t = info.shape[0]
    return pl.pallas_call(
        _rank_kernel,
        out_shape=(jax.ShapeDtypeStruct((t, LANES), F32), jax.ShapeDtypeStruct((1, LANES), F32)),
        grid=(t // tb,),
        in_specs=[pl.BlockSpec((tb, LANES), lambda i: (i, 0))],
        out_specs=(pl.BlockSpec((tb, LANES), lambda i: (i, 0)),
                   pl.BlockSpec((1, LANES), lambda i: (0, 0))),
        scratch_shapes=[pltpu.VMEM((1, LANES), F32)],
        compiler_params=_params(("arbitrary",)),
        name="moe_rank",
    )(info)


def _dispatch_plan(route, cnt, t, tm):
    n_tiles = t // tm + N_GROUPS
    rows = n_tiles * tm
    rank = route[:, 0].astype(jnp.int32)
    gid = route[:, 1].astype(jnp.int32)
    counts = cnt[0, :N_GROUPS].astype(jnp.int32)
    padded = (counts + tm - 1) // tm * tm
    ends = jnp.cumsum(padded)
    pos = (ends - padded)[gid] + rank
    tok = jnp.arange(t, dtype=jnp.int32)
    row = jnp.arange(rows, dtype=jnp.int32)
    owner = jnp.full((rows,), -1, jnp.int32).at[pos].set(tok, unique_indices=True,
                                                          mode="promise_in_bounds")
    src = jnp.maximum(owner, 0)
    dst = jnp.where(owner < 0, t + row % tm, owner)
    dst = jnp.concatenate([t + row[:tm], dst])
    tile_start = jnp.arange(n_tiles, dtype=jnp.int32) * tm
    tile_group = jnp.minimum(jnp.sum(tile_start[:, None] >= ends[None, :], axis=1), N_GROUPS - 1)
    return src, dst, tile_group.astype(jnp.int32), n_tiles


def _expert_kernel(src_ref, dst_ref, tg_ref, ext_hbm, wg_ref, wu_ref, wd_ref, g_ref, b_ref, out_hbm,
                   xa, xb, oa, ob, gsem, ssem, *, tm, n_tiles):
    i = pl.program_id(0)
    bufs = ((xa, oa, 0), (xb, ob, 1))

    def gather_copy(tile, buf, r):
        return pltpu.make_async_copy(ext_hbm.at[pl.ds(src_ref[tile * tm + r], 1)],
                                     buf[0].at[pl.ds(r, 1)], gsem.at[buf[2]])

    def scatter_copy(tile, buf, r):
        return pltpu.make_async_copy(buf[1].at[pl.ds(r, 1)],
                                     out_hbm.at[pl.ds(dst_ref[(tile + 1) * tm + r], 1)],
                                     ssem.at[buf[2]])

    def wait_gather(buf):
        pltpu.make_async_copy(ext_hbm.at[pl.ds(0, tm)], buf[0], gsem.at[buf[2]]).wait()

    def wait_scatter(buf):
        pltpu.make_async_copy(buf[1], out_hbm.at[pl.ds(0, tm)], ssem.at[buf[2]]).wait()

    @pl.when(i == 0)
    def _():
        ob[...] = jnp.zeros(ob.shape, F32)
        for r in range(tm):
            gather_copy(0, bufs[0], r).start()

    def tile_body(cur, oth):
        wait_gather(cur)

        @pl.when(i >= 1)
        def _():
            wait_scatter(cur)

        x = cur[0][:, :D_MODEL]
        info = cur[0][:, D_MODEL:]
        xb_ = x.astype(BF16)
        acc = jnp.zeros((tm, D_MODEL), F32)
        per = tm // EXPERTS_PER_GROUP
        nxt = jnp.minimum(i + 1, n_tiles - 1)
        for e in range(EXPERTS_PER_GROUP):
            act = (_silu(jnp.dot(xb_, wg_ref[0, e], preferred_element_type=F32))
                   * jnp.dot(xb_, wu_ref[0, e], preferred_element_type=F32) * info[:, e:e + 1])
            acc = acc + _dot(act, wd_ref[0, e])
            half = EXPERTS_PER_GROUP // 2
            for r in range((e % half) * 2 * per, (e % half + 1) * 2 * per):
                if e < half:
                    gather_copy(nxt, oth, r).start()
                else:
                    scatter_copy(i - 1, oth, r).start()
        cur[1][...] = _layer_norm(DEEPNORM_ALPHA * x + acc, g_ref[...], b_ref[...])

        @pl.when(i == n_tiles - 1)
        def _():
            wait_scatter(oth)
            for r in range(tm):
                scatter_copy(i, cur, r).start()
            wait_scatter(cur)
            wait_gather(oth)

    @pl.when(i % 2 == 0)
    def _():
        tile_body(bufs[0], bufs[1])

    @pl.when(i % 2 == 1)
    def _():
        tile_body(bufs[1], bufs[0])


def _expert_call(ext, plan, w, tm):
    src, dst, tile_group, n_tiles = plan
    t = ext.shape[0]
    d = D_MODEL
    wspec = lambda shape: pl.BlockSpec((1,) + shape, lambda i, s, d_, tg: (tg[i], 0, 0, 0))
    vec = pl.BlockSpec((1, d), lambda i, s, d_, tg: (0, 0))
    grid_spec = pltpu.PrefetchScalarGridSpec(
        num_scalar_prefetch=3,
        grid=(n_tiles,),
        in_specs=[pl.BlockSpec(memory_space=pl.ANY), wspec((EXPERTS_PER_GROUP, d, EXPERT_FF)),
                  wspec((EXPERTS_PER_GROUP, d, EXPERT_FF)), wspec((EXPERTS_PER_GROUP, EXPERT_FF, d)),
                  vec, vec],
        out_specs=pl.BlockSpec(memory_space=pl.ANY),
        scratch_shapes=[pltpu.VMEM((tm, EXT_COLS), F32), pltpu.VMEM((tm, EXT_COLS), F32),
                        pltpu.VMEM((tm, d), F32), pltpu.VMEM((tm, d), F32),
                        pltpu.SemaphoreType.DMA((2,)), pltpu.SemaphoreType.DMA((2,))],
    )
    out = pl.pallas_call(
        functools.partial(_expert_kernel, tm=tm, n_tiles=n_tiles),
        out_shape=jax.ShapeDtypeStruct((t + tm, d), F32),
        grid_spec=grid_spec,
        compiler_params=_params(("arbitrary",)),
        name="moe_experts",
    )(src, dst, tile_group, ext, w["e_gate"], w["e_up"], w["e_down"], w["ln2_g"], w["ln2_b"])
    return out


def _moe_call(ext, info, w, tm=MOE_TILE):
    route, cnt = _rank_call(info)
    return _expert_call(ext, _dispatch_plan(route, cnt, ext.shape[0], tm), w, tm)


def _pad_cols(w, lo, total):
    return jnp.zeros((w.shape[0], total), w.dtype).at[:, lo:lo + w.shape[1]].set(w)


def _layer_weights(l, p):
    d = D_MODEL
    w_in = p["w_in"][l]
    sizes = (MLA_Q_RANK, MLA_KV_RANK, MLA_ROPE, DN_HEADS * (2 * DN_DK + DN_DV), DN_HEADS, DN_HEADS,
             DN_HEADS * DN_DV, HG_HEADS * HG_DK, HG_HEADS * HG_DK, HG_HEADS * HG_DV,
             HG_HEADS * HG_DV, 3 * d)
    offs = np.concatenate([[0], np.cumsum(sizes)])
    col = lambda i: w_in[:, offs[i]:offs[i + 1]]
    qk_w = MLA_NOPE + MLA_ROPE
    uq = p["mla_w_uq"][l].reshape(MLA_Q_RANK, MLA_HEADS, qk_w)
    uq = jnp.pad(uq, ((0, 0), (0, 0), (0, HEAD_PAD - qk_w))).reshape(MLA_Q_RANK, MLA_HEADS * HEAD_PAD)
    ukv = p["mla_w_ukv"][l].reshape(MLA_KV_RANK, MLA_HEADS, MLA_NOPE + MLA_V)
    uk = jnp.pad(ukv[:, :, :MLA_NOPE], ((0, 0), (0, 0), (0, HEAD_PAD - MLA_NOPE)))
    uk = uk.reshape(MLA_KV_RANK, MLA_HEADS * HEAD_PAD)
    uv = jnp.pad(ukv[:, :, MLA_NOPE:], ((0, 0), (0, 0), (0, HEAD_PAD - MLA_V)))
    uv = uv.reshape(MLA_KV_RANK, MLA_HEADS * HEAD_PAD)
    ba = jnp.concatenate([col(4), col(5)], axis=1)
    w_dn = jnp.concatenate([col(3), col(6), _pad_cols(ba, 0, LANES)], axis=1)
    w_hg = jnp.concatenate([col(7), col(8), col(9), col(10)], axis=1)
    router = jnp.concatenate([p["router_group_w"][l], p["router_expert_w"][l]], axis=1)
    router = _pad_cols(router, 0, LANES)
    r_hi = router.astype(BF16)
    r_b = jnp.concatenate([p["router_group_b"][l], p["router_expert_b"][l]])
    bf = lambda a: a.astype(BF16)
    return dict(
        wcq=bf(col(0)), wckv=bf(col(1)), wkr=bf(_pad_cols(col(2), ROPE_LO, LANES)),
        qn=p["mla_q_norm"][l].reshape(1, -1), kvn=p["mla_kv_norm"][l].reshape(1, -1),
        wuq=bf(uq), wuk=bf(uk), wuv=bf(uv),
        w_dn=bf(w_dn), w_hg=bf(w_hg), wgate=bf(col(11)),
        dn_conv=p["dn_conv"][l],
        alog128=_pad_cols(p["dn_a_log"][l].reshape(1, -1), DN_HEADS, LANES),
        dtb128=_pad_cols(p["dn_dt_bias"][l].reshape(1, -1), DN_HEADS, LANES),
        dn_onorm=p["dn_o_norm"][l].reshape(1, -1), hg_onorm=p["hg_o_norm"][l].reshape(1, -1),
        wbr_a=bf(p["w_br_a"][l]), wbr_b=bf(p["w_br_b"][l]), wbr_c=bf(p["w_br_c"][l]),
        wout=bf(p["w_out"][l]),
        ln1_g=p["ln1_g"][l].reshape(1, d), ln1_b=p["ln1_b"][l].reshape(1, d),
        r_cat=jnp.concatenate([r_hi, bf(router - r_hi.astype(F32))], axis=1),
        r_b=_pad_cols(r_b.reshape(1, -1), 0, LANES),
        e_gate=bf(p["exp_w_gate"][l]).reshape(N_GROUPS, EXPERTS_PER_GROUP, d, EXPERT_FF),
        e_up=bf(p["exp_w_up"][l]).reshape(N_GROUPS, EXPERTS_PER_GROUP, d, EXPERT_FF),
        e_down=bf(p["exp_w_down"][l]).reshape(N_GROUPS, EXPERTS_PER_GROUP, EXPERT_FF, d),
        ln2_g=p["ln2_g"][l].reshape(1, d), ln2_b=p["ln2_b"][l].reshape(1, d),
    )


def _forward(p):
    x = p["x"]
    batch, s, d = x.shape
    t = batch * s
    half = MLA_ROPE // 2
    inv_freq = ROPE_BASE ** (-jnp.arange(half, dtype=F32) / half)
    inv128 = _pad_cols(jnp.concatenate([inv_freq, inv_freq]).reshape(1, -1), ROPE_LO, LANES)
    tables = _rope_call(p["positions"].reshape(t, 1), inv128)
    h = _ln_call(x.reshape(t, d), p["ln_in_g"], p["ln_in_b"])
    for l in range(DEPTH):
        w = _layer_weights(l, p)
        q, k, v = _mla_proj_call(h, t, w, tables)
        oa = _flash_call(q, k, v, batch)
        x_dn = _proj_call(h, t, w["w_dn"], "proj_deltanet")
        ob = _dn_call(x_dn.reshape(batch, s, -1), w["dn_conv"], w["alog128"], w["dtb128"],
                      w["dn_onorm"])
        x_hg = _proj_call(h, t, w["w_hg"], "proj_hgrn2")
        oc = _hg_call(x_hg.reshape(batch, s, -1), p["hg_lower_bounds"], w["hg_onorm"], l)
        ext, info = _merge_call(h, oa, ob.reshape(t, -1), oc.reshape(t, -1), w)
        h = _moe_call(ext, info, w)
    return h[:t].reshape(batch, s, d)


def kernel(x, positions, ln_in_g, ln_in_b, hg_lower_bounds, w_in, mla_q_norm, mla_w_uq, mla_kv_norm, mla_w_ukv, dn_conv, dn_a_log, dn_dt_bias, dn_o_norm, hg_o_norm, w_br_a, w_br_b, w_br_c, w_out, ln1_g, ln1_b, router_group_w, router_group_b, router_expert_w, router_expert_b, exp_w_gate, exp_w_up, exp_w_down, ln2_g, ln2_b):
    return _forward(dict(
        x=x, positions=positions, ln_in_g=ln_in_g, ln_in_b=ln_in_b,
        hg_lower_bounds=hg_lower_bounds, w_in=w_in, mla_q_norm=mla_q_norm, mla_w_uq=mla_w_uq,
        mla_kv_norm=mla_kv_norm, mla_w_ukv=mla_w_ukv, dn_conv=dn_conv, dn_a_log=dn_a_log,
        dn_dt_bias=dn_dt_bias, dn_o_norm=dn_o_norm, hg_o_norm=hg_o_norm, w_br_a=w_br_a,
        w_br_b=w_br_b, w_br_c=w_br_c, w_out=w_out, ln1_g=ln1_g, ln1_b=ln1_b,
        router_group_w=router_group_w, router_group_b=router_group_b,
        router_expert_w=router_expert_w, router_expert_b=router_expert_b,
        exp_w_gate=exp_w_gate, exp_w_up=exp_w_up, exp_w_down=exp_w_down, ln2_g=ln2_g, ln2_b=ln2_b))
```

```python
import functools
import math

import jax
import jax.numpy as jnp
import numpy as np
from jax import lax
from jax.experimental import pallas as pl
from jax.experimental.pallas import tpu as pltpu

D_MODEL = 1024
DEPTH = 2
MLA_HEADS = 8
MLA_Q_RANK = 256
MLA_KV_RANK = 128
MLA_NOPE = 64
MLA_ROPE = 32
MLA_V = 64
ROPE_BASE = 10000.0
MASK_VALUE = -1e30
DN_HEADS = 4
DN_DK = 128
DN_DV = 128
DN_CONV = 4
HG_HEADS = 4
HG_DK = 128
HG_DV = 128
CHUNK = 64
MIN_FORGET = 1e-30
N_GROUPS = 4
EXPERTS_PER_GROUP = 8
EXPERT_FF = 256
DEEPNORM_ALPHA = (2 * DEPTH) ** 0.25
NORM_EPS = 1e-6

LANES = 128
SUBLANES = 8
HEAD_PAD = 128
ROPE_LO = MLA_NOPE
ROPE_MID = MLA_NOPE + MLA_ROPE // 2
ROPE_HI = MLA_NOPE + MLA_ROPE
EXT_COLS = D_MODEL + LANES

ROW_TILE = 1024
PROJ_TILE = 512
ATTN_TILE = 512
ATTN_Q_SUBTILES = 4
SCAN_ROWS = 256
MOE_TILE = 512
VMEM_LIMIT = 56 * 1024 * 1024

BF16 = jnp.bfloat16
F32 = jnp.float32


def _dot(a, b):
    return jnp.dot(a.astype(BF16), b.astype(BF16), preferred_element_type=F32)


def _dot_nt(a, b):
    return lax.dot_general(a.astype(BF16), b.astype(BF16), (((1,), (1,)), ((), ())),
                           preferred_element_type=F32)


def _dot_tn(a, b):
    return lax.dot_general(a.astype(BF16), b.astype(BF16), (((0,), (0,)), ((), ())),
                           preferred_element_type=F32)


def _sigmoid(x):
    return 1.0 / (1.0 + jnp.exp(-x))


def _silu(x):
    return x * _sigmoid(x)


def _layer_norm(x, g, b):
    mu = jnp.mean(x, axis=-1, keepdims=True)
    xc = x - mu
    var = jnp.mean(xc * xc, axis=-1, keepdims=True)
    return xc * lax.rsqrt(var + NORM_EPS) * g + b


def _rms_norm(x, g):
    return x * lax.rsqrt(jnp.mean(x * x, axis=-1, keepdims=True) + NORM_EPS) * g


def _params(sem):
    return pltpu.CompilerParams(dimension_semantics=sem, vmem_limit_bytes=VMEM_LIMIT)


def _full(shape):
    n = len(shape)
    return pl.BlockSpec(shape, lambda *_: (0,) * n)


def _ln_kernel(x_ref, g_ref, b_ref, o_ref):
    o_ref[...] = _layer_norm(x_ref[...], g_ref[...], b_ref[...])


def _ln_call(x, g, b, tm=ROW_TILE):
    t, d = x.shape
    return pl.pallas_call(
        _ln_kernel,
        out_shape=jax.ShapeDtypeStruct((t, d), F32),
        grid=(t // tm,),
        in_specs=[pl.BlockSpec((tm, d), lambda i: (i, 0)), _full((1, d)), _full((1, d))],
        out_specs=pl.BlockSpec((tm, d), lambda i: (i, 0)),
        compiler_params=_params(("parallel",)),
        name="ln_in",
    )(x, g.reshape(1, d), b.reshape(1, d))


def _rope_kernel(pos_ref, inv_ref, c_ref, sa_ref, sb_ref):
    ang = pos_ref[...].astype(F32) * inv_ref[...]
    lane = lax.broadcasted_iota(jnp.int32, ang.shape, 1)
    sin = jnp.sin(ang)
    c_ref[...] = jnp.cos(ang)
    sa_ref[...] = jnp.where((lane >= ROPE_MID) & (lane < ROPE_HI), sin, 0.0)
    sb_ref[...] = jnp.where((lane >= ROPE_LO) & (lane < ROPE_MID), -sin, 0.0)


def _rope_call(pos_col, inv128, tm=ROW_TILE):
    t = pos_col.shape[0]
    spec = pl.BlockSpec((tm, LANES), lambda i: (i, 0))
    shp = jax.ShapeDtypeStruct((t, LANES), F32)
    return pl.pallas_call(
        _rope_kernel,
        out_shape=(shp, shp, shp),
        grid=(t // tm,),
        in_specs=[pl.BlockSpec((tm, 1), lambda i: (i, 0)), _full((1, LANES))],
        out_specs=(spec, spec, spec),
        compiler_params=_params(("parallel",)),
        name="rope_tables",
    )(pos_col, inv128)


def _rope_apply(x, c, sa, sb):
    return x * c + pltpu.roll(x, 16, 1) * sa + pltpu.roll(x, LANES - 16, 1) * sb


def _mla_proj_kernel(h_ref, wcq_ref, wckv_ref, wkr_ref, qn_ref, kvn_ref, wuq_ref, wuk_ref,
                     wuv_ref, c_ref, sa_ref, sb_ref, qt_ref, k_ref, vt_ref):
    hb = h_ref[...].astype(BF16)
    cq = _rms_norm(_dot(hb, wcq_ref[...]), qn_ref[...])
    ckv = _rms_norm(_dot(hb, wckv_ref[...]), kvn_ref[...])
    kr = _dot(hb, wkr_ref[...])
    q = _dot(cq, wuq_ref[...])
    k = _dot(ckv, wuk_ref[...])
    v = _dot(ckv, wuv_ref[...])
    lane = lax.broadcasted_iota(jnp.int32, (1, HEAD_PAD), 1)
    one_lane = jnp.where(lane == MLA_V, 1.0, 0.0)
    for hd in range(MLA_HEADS):
        sl = slice(hd * HEAD_PAD, (hd + 1) * HEAD_PAD)
        vt_ref[0, hd] = (v[:, sl] + one_lane).T.astype(BF16)
    c, sa, sb = c_ref[...], sa_ref[...], sb_ref[...]
    scale = (MLA_NOPE + MLA_ROPE) ** -0.5 * math.log2(math.e)
    kr = _rope_apply(kr, c, sa, sb)
    for hd in range(MLA_HEADS):
        sl = slice(hd * HEAD_PAD, (hd + 1) * HEAD_PAD)
        qt_ref[0, sl, :] = (_rope_apply(q[:, sl], c, sa, sb) * scale).T.astype(BF16)
        k_ref[:, sl] = (k[:, sl] + kr).astype(BF16)


def _mla_proj_call(h, t, w, tables, tm=ATTN_TILE):
    d = h.shape[1]
    hq = MLA_HEADS * HEAD_PAD
    row = lambda n: pl.BlockSpec((tm, n), lambda i: (i, 0))
    tile = lambda n: pl.BlockSpec((1, n, tm), lambda i: (i, 0, 0))
    return pl.pallas_call(
        _mla_proj_kernel,
        out_shape=(jax.ShapeDtypeStruct((t // tm, hq, tm), BF16),
                   jax.ShapeDtypeStruct((t, hq), BF16),
                   jax.ShapeDtypeStruct((t // tm, MLA_HEADS, HEAD_PAD, tm), BF16)),
        grid=(t // tm,),
        in_specs=[row(d), _full((d, MLA_Q_RANK)), _full((d, MLA_KV_RANK)), _full((d, LANES)),
                  _full((1, MLA_Q_RANK)), _full((1, MLA_KV_RANK)), _full((MLA_Q_RANK, hq)),
                  _full((MLA_KV_RANK, hq)), _full((MLA_KV_RANK, hq)),
                  row(LANES), row(LANES), row(LANES)],
        out_specs=(tile(hq), row(hq),
                   pl.BlockSpec((1, MLA_HEADS, HEAD_PAD, tm), lambda i: (i, 0, 0, 0))),
        compiler_params=_params(("parallel",)),
        name="mla_proj",
    )(h, w["wcq"], w["wckv"], w["wkr"], w["qn"], w["kvn"], w["wuq"], w["wuk"], w["wuv"], *tables)


def _flash_kernel(qt_ref, k_ref, vt_ref, o_ref, m0, a0, m1, a1, sa_ref, sb_ref, *, tk, nsub):
    qi = pl.program_id(2)
    stats = ((m0, a0), (m1, a1))
    for m_ref, acc_ref in stats:
        m_ref[...] = jnp.full(m_ref.shape, MASK_VALUE, F32)
        acc_ref[...] = jnp.zeros(acc_ref.shape, F32)

    def score(j, s_ref, lo):
        rows = pl.ds(pl.multiple_of(j * tk, tk), tk)
        for hh in range(2):
            sl = slice(hh * HEAD_PAD, (hh + 1) * HEAD_PAD)
            qs = [qt_ref[0, u, sl, :] for u in range(lo // tk, nsub)]
            qh = qs[0] if len(qs) == 1 else jnp.concatenate(qs, axis=1)
            s_ref[hh, :, lo:] = jnp.dot(k_ref[0, rows, sl], qh, preferred_element_type=F32)

    def consume(j, s_ref, lo, masked):
        for hh, (m_ref, acc_ref) in enumerate(stats):
            vt = vt_ref[0, j, 0, hh]
            st = s_ref[hh, :, lo:]
            if masked:
                key = lax.broadcasted_iota(jnp.int32, st.shape, 0)
                qry = lax.broadcasted_iota(jnp.int32, st.shape, 1)
                st = jnp.where(key <= qry, st, MASK_VALUE)
            m_old = m_ref[:, lo:]
            m_new = jnp.maximum(m_old, jnp.max(st, axis=0, keepdims=True))
            alpha = jnp.exp2(m_old - m_new)
            p = jnp.exp2(st - m_new)
            acc_ref[:, lo:] = alpha * acc_ref[:, lo:] + jnp.dot(vt, p.astype(BF16),
                                                                preferred_element_type=F32)
            m_ref[:, lo:] = m_new

    score(0, sa_ref, 0)

    def body(i, carry):
        j = 2 * i
        score(j + 1, sb_ref, 0)
        consume(j, sa_ref, 0, False)
        score(j + 2, sa_ref, 0)
        consume(j + 1, sb_ref, 0, False)
        return carry

    lax.fori_loop(0, qi * (nsub // 2), body, 0)
    base = nsub * qi
    bufs = (sa_ref, sb_ref)
    for u in range(nsub):
        if u + 1 < nsub:
            score(base + u + 1, bufs[(u + 1) % 2], (u + 1) * tk)
        consume(base + u, bufs[u % 2], u * tk, True)
    ot = jnp.concatenate([acc_ref[:MLA_V, :] * (1.0 / acc_ref[MLA_V:MLA_V + 1, :])
                          for _, acc_ref in stats], axis=0)
    o_ref[0] = ot.T.astype(BF16)


def _flash_call(qt, k, vt, batch, tk=ATTN_TILE):
    nt, hq, _ = qt.shape
    t = nt * tk
    s = t // batch
    ns = s // tk
    nsub = ATTN_Q_SUBTILES
    tq = nsub * tk
    pairs = MLA_HEADS // 2
    hv = MLA_HEADS * MLA_V
    qt5 = qt.reshape(batch, ns, hq, tk)
    k3 = k.reshape(batch, s, hq)
    vt6 = vt.reshape(batch, ns, pairs, 2, HEAD_PAD, tk)
    stat = lambda: pltpu.VMEM((1, tq), F32)
    acc = lambda: pltpu.VMEM((HEAD_PAD, tq), F32)
    out = pl.pallas_call(
        functools.partial(_flash_kernel, tk=tk, nsub=nsub),
        out_shape=jax.ShapeDtypeStruct((batch, s, hv), BF16),
        grid=(batch, pairs, s // tq),
        in_specs=[pl.BlockSpec((1, nsub, 2 * HEAD_PAD, tk), lambda b, p, i: (b, i, p, 0)),
                  pl.BlockSpec((1, s, 2 * HEAD_PAD), lambda b, p, i: (b, 0, p)),
                  pl.BlockSpec((1, ns, 1, 2, HEAD_PAD, tk),
                               lambda b, p, i: (b, 0, p, 0, 0, 0))],
        out_specs=pl.BlockSpec((1, tq, 2 * MLA_V), lambda b, p, i: (b, i, p)),
        scratch_shapes=[stat(), acc(), stat(), acc(),
                        pltpu.VMEM((2, tk, tq), F32), pltpu.VMEM((2, tk, tq), F32)],
        compiler_params=_params(("parallel", "parallel", "arbitrary")),
        name="flash_attn",
    )(qt5, k3, vt6)
    return out.reshape(t, hv)


def _proj_kernel(h_ref, w_ref, o_ref):
    o_ref[...] = _dot(h_ref[...], w_ref[...])


def _proj_call(h, t, w, name, tm=PROJ_TILE):
    d = h.shape[1]
    n = w.shape[1]
    return pl.pallas_call(
        _proj_kernel,
        out_shape=jax.ShapeDtypeStruct((t, n), F32),
        grid=(t // tm,),
        in_specs=[pl.BlockSpec((tm, d), lambda i: (i, 0)), _full((d, n))],
        out_specs=pl.BlockSpec((tm, n), lambda i: (i, 0)),
        compiler_params=_params(("parallel",)),
        name=name,
    )(h, w)


def _chunk_cumsum(x):
    rows = lax.broadcasted_iota(jnp.int32, x.shape, 0) % CHUNK
    shift = 1
    while shift < CHUNK:
        x = x + jnp.where(rows >= shift, pltpu.roll(x, shift, 0), 0.0)
        shift *= 2
    return x


def _chunk_last(x):
    r, n = x.shape
    x3 = x.reshape(r // CHUNK, CHUNK, n)
    return jnp.broadcast_to(x3[:, CHUNK - 1:CHUNK, :], x3.shape).reshape(r, n)


def _chunk_masks(r):
    i = lax.broadcasted_iota(jnp.int32, (r, r), 0)
    j = lax.broadcasted_iota(jnp.int32, (r, r), 1)
    same = (i // CHUNK) == (j // CHUNK)
    return same & (j <= i), same & (j < i), i == j


def _dn_solve(heads, eye):
    invs = [eye - p["m"] for p in heads]
    pws = [p["m"].astype(BF16) for p in heads]
    for _ in range(5):
        pws = [jnp.dot(pw, pw, preferred_element_type=F32).astype(BF16) for pw in pws]
        invs = [inv + jnp.dot(inv.astype(BF16), pw, preferred_element_type=F32)
                for inv, pw in zip(invs, pws)]
    for p, inv in zip(heads, invs):
        uw = jnp.dot(inv.astype(BF16), p["rhs"], preferred_element_type=F32).astype(BF16)
        quw = jnp.dot(p["qk"].astype(BF16), uw, preferred_element_type=F32)
        p["o_own"] = quw[:, :DN_DV]
        q_eff = (p["qd"] - quw[:, DN_DV:]).astype(BF16)
        kd = p["kd"].astype(BF16)
        p["lhs"], p["add"] = [], []
        for c in range(uw.shape[0] // CHUNK):
            rc = slice(c * CHUNK, (c + 1) * CHUNK)
            kuw = lax.dot_general(kd[rc], uw[rc], (((0,), (0,)), ((), ())),
                                  preferred_element_type=F32)
            p["add"].append(kuw[:, :DN_DV])
            p["lhs"].append(jnp.concatenate([kuw[:, DN_DV:].astype(BF16), q_eff[rc]], axis=0))


def _dn_kernel(x_ref, conv_ref, alog_ref, dtb_ref, onorm_ref, o_ref, halo_ref, state_ref,
               *, batch):
    r = SCAN_ROWS
    nqkv = DN_HEADS * (2 * DN_DK + DN_DV)
    gate_off = nqkv
    ba_off = nqkv + DN_HEADS * DN_DV

    @pl.when(pl.program_id(0) == 0)
    def _():
        halo_ref[...] = jnp.zeros(halo_ref.shape, F32)
        state_ref[...] = jnp.zeros(state_ref.shape, F32)

    incl, strict, diag = _chunk_masks(r)
    eye = jnp.where(diag, 1.0, 0.0)
    cw = conv_ref[...]
    pre = []
    for b in range(batch):
        xq = x_ref[b, :, :nqkv]
        xe = jnp.concatenate([halo_ref[b], xq], axis=0)
        halo_ref[b] = xq[r - SUBLANES:, :]
        y = xe[SUBLANES:, :] * cw[DN_CONV - 1:DN_CONV, :]
        for i in range(DN_CONV - 1):
            y = y + pltpu.roll(xe, DN_CONV - 1 - i, 0)[SUBLANES:, :] * cw[i:i + 1, :]
        y = _silu(y)
        ba = x_ref[b, :, ba_off:ba_off + LANES]
        beta_all = _sigmoid(ba)
        av = ba + dtb_ref[...]
        softplus = jnp.maximum(av, 0.0) + jnp.log(1.0 + jnp.exp(-jnp.abs(av)))
        g_all = _chunk_cumsum(-jnp.exp(alog_ref[...]) * softplus) * math.log2(math.e)
        g_all_t = g_all.T
        g_last_all = _chunk_last(g_all)
        for hd in range(DN_HEADS):
            q = y[:, hd * DN_DK:(hd + 1) * DN_DK]
            k = y[:, DN_HEADS * DN_DK + hd * DN_DK:DN_HEADS * DN_DK + (hd + 1) * DN_DK]
            v = y[:, 2 * DN_HEADS * DN_DK + hd * DN_DV:2 * DN_HEADS * DN_DK + (hd + 1) * DN_DV]
            q = q * lax.rsqrt(jnp.sum(q * q, axis=-1, keepdims=True) + NORM_EPS) * DN_DK ** -0.5
            k = k * lax.rsqrt(jnp.sum(k * k, axis=-1, keepdims=True) + NORM_EPS)
            beta = beta_all[:, hd:hd + 1]
            g_col = g_all[:, DN_HEADS + hd:DN_HEADS + hd + 1]
            g_row = g_all_t[DN_HEADS + hd:DN_HEADS + hd + 1, :]
            g_last = g_last_all[:, DN_HEADS + hd:DN_HEADS + hd + 1]
            decay = jnp.where(incl, jnp.exp2(jnp.where(incl, g_col - g_row, 0.0)), 0.0)
            kb = k * beta
            kk = _dot_nt(kb, k)
            m = jnp.where(strict, kk * decay, 0.0)
            qk = jnp.where(incl, _dot_nt(q, k) * decay, 0.0)
            eg = jnp.exp2(g_col)
            pre.append(dict(m=m, rhs=jnp.concatenate([v * beta, kb * eg], axis=1).astype(BF16),
                            qd=q * eg, kd=k * jnp.exp2(g_last - g_col), qk=qk,
                            gl=jnp.exp2(g_last)))
        _dn_solve(pre[b * DN_HEADS:], eye)
    outs = [[] for _ in pre]
    for c in range(r // CHUNK):
        rc = slice(c * CHUNK, (c + 1) * CHUNK)
        for idx, p in enumerate(pre):
            st = state_ref[idx]
            prod = jnp.dot(p["lhs"][c], st.astype(BF16), preferred_element_type=F32)
            outs[idx].append(prod[DN_DK:] + p["o_own"][rc])
            state_ref[idx] = (st * p["gl"][c * CHUNK:c * CHUNK + 1, :] - prod[:DN_DK]
                              + p["add"][c])
    for idx in range(len(pre)):
        b, hd = divmod(idx, DN_HEADS)
        o = jnp.concatenate(outs[idx], axis=0)
        gate = x_ref[b, :, gate_off + hd * DN_DV:gate_off + (hd + 1) * DN_DV]
        o_ref[b, :, hd * DN_DV:(hd + 1) * DN_DV] = (
            _rms_norm(o, onorm_ref[...]) * _silu(gate)).astype(BF16)


def _dn_call(x3, conv_w, alog128, dtb128, onorm):
    batch, s, n = x3.shape
    r = SCAN_ROWS
    nqkv = DN_HEADS * (2 * DN_DK + DN_DV)
    width = DN_HEADS * DN_DV
    return pl.pallas_call(
        functools.partial(_dn_kernel, batch=batch),
        out_shape=jax.ShapeDtypeStruct((batch, s, width), BF16),
        grid=(s // r,),
        in_specs=[pl.BlockSpec((batch, r, n), lambda i: (0, i, 0)), _full((DN_CONV, nqkv)),
                  _full((1, LANES)), _full((1, LANES)), _full((1, DN_DV))],
        out_specs=pl.BlockSpec((batch, r, width), lambda i: (0, i, 0)),
        scratch_shapes=[pltpu.VMEM((batch, SUBLANES, nqkv), F32),
                        pltpu.VMEM((batch * DN_HEADS, DN_DK, DN_DV), F32)],
        compiler_params=_params(("arbitrary",)),
        name="deltanet",
    )(x3, conv_w, alog128, dtb128, onorm)


HG_LEVELS = (32, 16, 8, 4, 2, 1)
HG_SEL_LEVELS = tuple(s for s in HG_LEVELS if 2 * s < SUBLANES)


def _hg_kernel(x_ref, lbp_ref, onorm_ref, sel_ref, o_ref, state_ref, *, batch, layer):
    r = SCAN_ROWS
    width = HG_HEADS * HG_DK

    @pl.when(pl.program_id(0) == 0)
    def _():
        state_ref[...] = jnp.zeros(state_ref.shape, F32)

    lbp = lbp_ref[...]
    e = jnp.exp(lbp - jnp.max(lbp, axis=0, keepdims=True))
    soft = e / jnp.sum(e, axis=0, keepdims=True)
    lb_all = jnp.zeros((1, width), F32)
    for i in range(1, layer + 1):
        lb_all = lb_all + soft[i:i + 1, :]

    i_idx = lax.broadcasted_iota(jnp.int32, (r, r), 0)
    j_idx = lax.broadcasted_iota(jnp.int32, (r, r), 1)
    level_masks = [((i_idx // (2 * s)) == (j_idx // (2 * s))) & ((i_idx % (2 * s)) >= s)
                   & ((j_idx % (2 * s)) < s) for s in HG_LEVELS]
    diag = i_idx == j_idx

    for b in range(batch):
        for hd in range(HG_HEADS):
            idx = b * HG_HEADS + hd
            hs = slice(hd * HG_DK, (hd + 1) * HG_DK)
            lb = lb_all[:, hs]
            q = _silu(x_ref[b, :, hd * HG_DK:(hd + 1) * HG_DK])
            z = x_ref[b, :, width + hd * HG_DK:width + (hd + 1) * HG_DK]
            v = x_ref[b, :, 2 * width + hd * HG_DV:2 * width + (hd + 1) * HG_DV]
            gate = x_ref[b, :, 3 * width + hd * HG_DV:3 * width + (hd + 1) * HG_DV]
            f = lb + (1.0 - lb) * _sigmoid(z)
            k = (1.0 - lb) * _sigmoid(-z)
            cum = _chunk_cumsum(jnp.log(jnp.maximum(f, MIN_FORGET))) * math.log2(math.e)
            cum_last = _chunk_last(cum)
            qd = q * jnp.exp2(cum)
            kd = k * jnp.exp2(cum_last - cum)
            dl = jnp.exp2(cum_last)
            att = jnp.where(diag, _dot_nt(q, k), 0.0)
            cum_b = cum.astype(BF16)
            for s, mask in zip(HG_LEVELS, level_masks):
                if s in HG_SEL_LEVELS:
                    ref = jnp.dot(sel_ref[HG_SEL_LEVELS.index(s)], cum_b,
                                  preferred_element_type=F32)
                else:
                    grp = cum.reshape(r // (2 * s), 2 * s, HG_DK)
                    ref = jnp.broadcast_to(grp[:, s:s + 1, :], grp.shape).reshape(r, HG_DK)
                fac = jnp.exp2(cum - ref)
                att = att + jnp.where(mask, _dot_nt(q * fac, k * (1.0 / fac)), 0.0)
            o_intra = _dot(att, v)
            outs = []
            st = state_ref[idx]
            for c in range(r // CHUNK):
                rc = slice(c * CHUNK, (c + 1) * CHUNK)
                outs.append(_dot_nt(qd[rc], st) + o_intra[rc])
                st = st * dl[c * CHUNK:c * CHUNK + 1, :] + _dot_tn(v[rc], kd[rc])
            state_ref[idx] = st
            o = jnp.concatenate(outs, axis=0)
            o_ref[b, :, hs] = (_rms_norm(o, onorm_ref[...]) * _silu(gate)).astype(BF16)


def _hg_sel():
    r = SCAN_ROWS
    i = np.arange(r)
    sel = np.zeros((len(HG_SEL_LEVELS), r, r), np.float32)
    for lv, s in enumerate(HG_SEL_LEVELS):
        sel[lv, i, (i // (2 * s)) * (2 * s) + s] = 1.0
    return jnp.asarray(sel, BF16)


def _hg_call(x3, lbp, onorm, layer):
    batch, s, n = x3.shape
    r = SCAN_ROWS
    width = HG_HEADS * HG_DV
    return pl.pallas_call(
        functools.partial(_hg_kernel, batch=batch, layer=layer),
        out_shape=jax.ShapeDtypeStruct((batch, s, width), BF16),
        grid=(s // r,),
        in_specs=[pl.BlockSpec((batch, r, n), lambda i: (0, i, 0)), _full(lbp.shape),
                  _full((1, HG_DV)), _full((len(HG_SEL_LEVELS), r, r))],
        out_specs=pl.BlockSpec((batch, r, width), lambda i: (0, i, 0)),
        scratch_shapes=[pltpu.VMEM((batch * HG_HEADS, HG_DV, HG_DK), F32)],
        compiler_params=_params(("arbitrary",)),
        name="hgrn2",
    )(x3, lbp, onorm, _hg_sel())


def _merge_kernel(h_ref, oa_ref, ob_ref, oc_ref, wgate_ref, wa_ref, wb_ref, wc_ref, wout_ref,
                  g_ref, b_ref, rcat_ref, rb_ref, ext_ref, info_ref):
    h = h_ref[...]
    gates = _sigmoid(_dot(h, wgate_ref[...]))
    d = D_MODEL
    mixed = (gates[:, :d] * _dot(oa_ref[...], wa_ref[...])
             + gates[:, d:2 * d] * _dot(ob_ref[...], wb_ref[...])
             + gates[:, 2 * d:] * _dot(oc_ref[...], wc_ref[...]))
    h1 = _layer_norm(DEEPNORM_ALPHA * h + _dot(mixed, wout_ref[...]), g_ref[...], b_ref[...])
    hi = h1.astype(BF16)
    lo = (h1 - hi.astype(F32)).astype(BF16)
    both = jnp.dot(hi, rcat_ref[...], preferred_element_type=F32)
    logits = (both[:, :LANES] + both[:, LANES:]
              + jnp.dot(lo, rcat_ref[:, :LANES], preferred_element_type=F32)) + rb_ref[...]
    lane = lax.broadcasted_iota(jnp.int32, logits.shape, 1)
    neg = jnp.float32(-jnp.inf)
    big = jnp.int32(1 << 20)
    is_g = lane < N_GROUPS
    gl = jnp.where(is_g, logits, neg)
    gmax = jnp.max(gl, axis=-1, keepdims=True)
    gidx = jnp.min(jnp.where(gl == gmax, lane, big), axis=-1, keepdims=True)
    p_group = 1.0 / jnp.sum(jnp.where(is_g, jnp.exp(gl - gmax), 0.0), axis=-1, keepdims=True)
    lo_lane = N_GROUPS + gidx * EXPERTS_PER_GROUP
    in_g = (lane >= lo_lane) & (lane < lo_lane + EXPERTS_PER_GROUP)
    el = jnp.where(in_g, logits, neg)
    v1 = jnp.max(el, axis=-1, keepdims=True)
    i1 = jnp.min(jnp.where(el == v1, lane, big), axis=-1, keepdims=True)
    el2 = jnp.where(lane == i1, neg, el)
    v2 = jnp.max(el2, axis=-1, keepdims=True)
    i2 = jnp.min(jnp.where(el2 == v2, lane, big), axis=-1, keepdims=True)
    e21 = jnp.exp(v2 - v1)
    w1 = p_group / (1.0 + e21)
    w2 = p_group * e21 / (1.0 + e21)
    info = (jnp.where(lane == i1 - lo_lane, w1, 0.0) + jnp.where(lane == i2 - lo_lane, w2, 0.0)
            + jnp.where(lane == EXPERTS_PER_GROUP, gidx.astype(F32), 0.0))
    info_ref[...] = info
    ext_ref[:, :D_MODEL] = h1
    ext_ref[:, D_MODEL:] = info


def _merge_call(h, oa, ob, oc, w, tm=PROJ_TILE):
    t, d = oa.shape[0], h.shape[1]
    row = lambda n: pl.BlockSpec((tm, n), lambda i: (i, 0))
    wa, wb, wc = w["wbr_a"], w["wbr_b"], w["wbr_c"]
    return pl.pallas_call(
        _merge_kernel,
        out_shape=(jax.ShapeDtypeStruct((t, EXT_COLS), F32),
                   jax.ShapeDtypeStruct((t, LANES), F32)),
        grid=(t // tm,),
        in_specs=[row(d), row(oa.shape[1]), row(ob.shape[1]), row(oc.shape[1]),
                  _full((d, 3 * d)), _full(wa.shape), _full(wb.shape), _full(wc.shape),
                  _full((d, d)), _full((1, d)), _full((1, d)),
                  _full((d, 2 * LANES)), _full((1, LANES))],
        out_specs=(row(EXT_COLS), row(LANES)),
        compiler_params=_params(("parallel",)),
        name="merge_ln_router",
    )(h, oa, ob, oc, w["wgate"], wa, wb, wc, w["wout"], w["ln1_g"], w["ln1_b"],
      w["r_cat"], w["r_b"])


def _rank_kernel(info_ref, route_ref, cnt_ref, base_ref):
    @pl.when(pl.program_id(0) == 0)
    def _():
        base_ref[...] = jnp.zeros(base_ref.shape, F32)

    info = info_ref[...]
    tb = info.shape[0]
    lane = lax.broadcasted_iota(jnp.int32, info.shape, 1)
    gid = info[:, EXPERTS_PER_GROUP:EXPERTS_PER_GROUP + 1]
    onehot = jnp.where((lane.astype(F32) == gid) & (lane < N_GROUPS), 1.0, 0.0)
    r = lax.broadcasted_iota(jnp.int32, (tb, tb), 0)
    c = lax.broadcasted_iota(jnp.int32, (tb, tb), 1)
    earlier = jnp.where(c < r, 1.0, 0.0)
    prefix = _dot(earlier, onehot)
    base = base_ref[...]
    rank = jnp.sum(onehot * (prefix + base), axis=1, keepdims=True)
    route_ref[...] = jnp.where(lane == 0, rank, jnp.where(lane == 1, gid, 0.0))
    base = base + jnp.sum(onehot, axis=0, keepdims=True)
    base_ref[...] = base
    cnt_ref[...] = base


def _rank_call(info, tb=PROJ_TILE):
    t = info.shape[0]
    return pl.pallas_call(
        _rank_kernel,
        out_shape=(jax.ShapeDtypeStruct((t, LANES), F32), jax.ShapeDtypeStruct((1, LANES), F32)),
        grid=(t // tb,),
        in_specs=[pl.BlockSpec((tb, LANES), lambda i: (i, 0))],
        out_specs=(pl.BlockSpec((tb, LANES), lambda i: (i, 0)),
                   pl.BlockSpec((1, LANES), lambda i: (0, 0))),
        scratch_shapes=[pltpu.VMEM((1, LANES), F32)],
        compiler_params=_params(("arbitrary",)),
        name="moe_rank",
    )(info)


def _dispatch_plan(route, cnt, t, tm):
    n_tiles = t // tm + N_GROUPS
    rows = n_tiles * tm
    rank = route[:, 0].astype(jnp.int32)
    gid = route[:, 1].astype(jnp.int32)
    counts = cnt[0, :N_GROUPS].astype(jnp.int32)
    padded = (counts + tm - 1) // tm * tm
    ends = jnp.cumsum(padded)
    pos = (ends - padded)[gid] + rank
    tok = jnp.arange(t, dtype=jnp.int32)
    row = jnp.arange(rows, dtype=jnp.int32)
    owner = jnp.full((rows,), -1, jnp.int32).at[pos].set(tok, unique_indices=True,
                                                          mode="promise_in_bounds")
    src = jnp.maximum(owner, 0)
    dst = jnp.where(owner < 0, t + row % tm, owner)
    dst = jnp.concatenate([t + row[:tm], dst])
    tile_start = jnp.arange(n_tiles, dtype=jnp.int32) * tm
    tile_group = jnp.minimum(jnp.sum(tile_start[:, None] >= ends[None, :], axis=1), N_GROUPS - 1)
    return src, dst, tile_group.astype(jnp.int32), n_tiles


def _expert_kernel(src_ref, dst_ref, tg_ref, ext_hbm, wg_ref, wu_ref, wd_ref, g_ref, b_ref, out_hbm,
                   xa, xb, oa, ob, gsem, ssem, *, tm, n_tiles):
    i = pl.program_id(0)
    bufs = ((xa, oa, 0), (xb, ob, 1))

    def gather_copy(tile, buf, r):
        return pltpu.make_async_copy(ext_hbm.at[pl.ds(src_ref[tile * tm + r], 1)],
                                     buf[0].at[pl.ds(r, 1)], gsem.at[buf[2]])

    def scatter_copy(tile, buf, r):
        return pltpu.make_async_copy(buf[1].at[pl.ds(r, 1)],
                                     out_hbm.at[pl.ds(dst_ref[(tile + 1) * tm + r], 1)],
                                     ssem.at[buf[2]])

    def wait_gather(buf):
        pltpu.make_async_copy(ext_hbm.at[pl.ds(0, tm)], buf[0], gsem.at[buf[2]]).wait()

    def wait_scatter(buf):
        pltpu.make_async_copy(buf[1], out_hbm.at[pl.ds(0, tm)], ssem.at[buf[2]]).wait()

    @pl.when(i == 0)
    def _():
        ob[...] = jnp.zeros(ob.shape, F32)
        for r in range(tm):
            gather_copy(0, bufs[0], r).start()

    def tile_body(cur, oth):
        wait_gather(cur)

        @pl.when(i >= 1)
        def _():
            wait_scatter(cur)

        x = cur[0][:, :D_MODEL]
        info = cur[0][:, D_MODEL:]
        xb_ = x.astype(BF16)
        acc = jnp.zeros((tm, D_MODEL), F32)
        per = tm // EXPERTS_PER_GROUP
        nxt = jnp.minimum(i + 1, n_tiles - 1)
        for e in range(EXPERTS_PER_GROUP):
            act = (_silu(jnp.dot(xb_, wg_ref[0, e], preferred_element_type=F32))
                   * jnp.dot(xb_, wu_ref[0, e], preferred_element_type=F32) * info[:, e:e + 1])
            acc = acc + _dot(act, wd_ref[0, e])
            half = EXPERTS_PER_GROUP // 2
            for r in range((e % half) * 2 * per, (e % half + 1) * 2 * per):
                if e < half:
                    gather_copy(nxt, oth, r).start()
                else:
                    scatter_copy(i - 1, oth, r).start()
        cur[1][...] = _layer_norm(DEEPNORM_ALPHA * x + acc, g_ref[...], b_ref[...])

        @pl.when(i == n_tiles - 1)
        def _():
            wait_scatter(oth)
            for r in range(tm):
                scatter_copy(i, cur, r).start()
            wait_scatter(cur)
            wait_gather(oth)

    @pl.when(i % 2 == 0)
    def _():
        tile_body(bufs[0], bufs[1])

    @pl.when(i % 2 == 1)
    def _():
        tile_body(bufs[1], bufs[0])


def _expert_call(ext, plan, w, tm):
    src, dst, tile_group, n_tiles = plan
    t = ext.shape[0]
    d = D_MODEL
    wspec = lambda shape: pl.BlockSpec((1,) + shape, lambda i, s, d_, tg: (tg[i], 0, 0, 0))
    vec = pl.BlockSpec((1, d), lambda i, s, d_, tg: (0, 0))
    grid_spec = pltpu.PrefetchScalarGridSpec(
        num_scalar_prefetch=3,
        grid=(n_tiles,),
        in_specs=[pl.BlockSpec(memory_space=pl.ANY), wspec((EXPERTS_PER_GROUP, d, EXPERT_FF)),
                  wspec((EXPERTS_PER_GROUP, d, EXPERT_FF)), wspec((EXPERTS_PER_GROUP, EXPERT_FF, d)),
                  vec, vec],
        out_specs=pl.BlockSpec(memory_space=pl.ANY),
        scratch_shapes=[pltpu.VMEM((tm, EXT_COLS), F32), pltpu.VMEM((tm, EXT_COLS), F32),
                        pltpu.VMEM((tm, d), F32), pltpu.VMEM((tm, d), F32),
                        pltpu.SemaphoreType.DMA((2,)), pltpu.SemaphoreType.DMA((2,))],
    )
    out = pl.pallas_call(
        functools.partial(_expert_kernel, tm=tm, n_tiles=n_tiles),
        out_shape=jax.ShapeDtypeStruct((t + tm, d), F32),
        grid_spec=grid_spec,
        compiler_params=_params(("arbitrary",)),
        name="moe_experts",
    )(src, dst, tile_group, ext, w["e_gate"], w["e_up"], w["e_down"], w["ln2_g"], w["ln2_b"])
    return out


def _moe_call(ext, info, w, tm=MOE_TILE):
    route, cnt = _rank_call(info)
    return _expert_call(ext, _dispatch_plan(route, cnt, ext.shape[0], tm), w, tm)


def _pad_cols(w, lo, total):
    return jnp.zeros((w.shape[0], total), w.dtype).at[:, lo:lo + w.shape[1]].set(w)


def _layer_weights(l, p):
    d = D_MODEL
    w_in = p["w_in"][l]
    sizes = (MLA_Q_RANK, MLA_KV_RANK, MLA_ROPE, DN_HEADS * (2 * DN_DK + DN_DV), DN_HEADS, DN_HEADS,
             DN_HEADS * DN_DV, HG_HEADS * HG_DK, HG_HEADS * HG_DK, HG_HEADS * HG_DV,
             HG_HEADS * HG_DV, 3 * d)
    offs = np.concatenate([[0], np.cumsum(sizes)])
    col = lambda i: w_in[:, offs[i]:offs[i + 1]]
    qk_w = MLA_NOPE + MLA_ROPE
    uq = p["mla_w_uq"][l].reshape(MLA_Q_RANK, MLA_HEADS, qk_w)
    uq = jnp.pad(uq, ((0, 0), (0, 0), (0, HEAD_PAD - qk_w))).reshape(MLA_Q_RANK, MLA_HEADS * HEAD_PAD)
    ukv = p["mla_w_ukv"][l].reshape(MLA_KV_RANK, MLA_HEADS, MLA_NOPE + MLA_V)
    uk = jnp.pad(ukv[:, :, :MLA_NOPE], ((0, 0), (0, 0), (0, HEAD_PAD - MLA_NOPE)))
    uk = uk.reshape(MLA_KV_RANK, MLA_HEADS * HEAD_PAD)
    uv = jnp.pad(ukv[:, :, MLA_NOPE:], ((0, 0), (0, 0), (0, HEAD_PAD - MLA_V)))
    uv = uv.reshape(MLA_KV_RANK, MLA_HEADS * HEAD_PAD)
    ba = jnp.concatenate([col(4), col(5)], axis=1)
    w_dn = jnp.concatenate([col(3), col(6), _pad_cols(ba, 0, LANES)], axis=1)
    w_hg = jnp.concatenate([col(7), col(8), col(9), col(10)], axis=1)
    router = jnp.concatenate([p["router_group_w"][l], p["router_expert_w"][l]], axis=1)
    router = _pad_cols(router, 0, LANES)
    r_hi = router.astype(BF16)
    r_b = jnp.concatenate([p["router_group_b"][l], p["router_expert_b"][l]])
    bf = lambda a: a.astype(BF16)
    return dict(
        wcq=bf(col(0)), wckv=bf(col(1)), wkr=bf(_pad_cols(col(2), ROPE_LO, LANES)),
        qn=p["mla_q_norm"][l].reshape(1, -1), kvn=p["mla_kv_norm"][l].reshape(1, -1),
        wuq=bf(uq), wuk=bf(uk), wuv=bf(uv),
        w_dn=bf(w_dn), w_hg=bf(w_hg), wgate=bf(col(11)),
        dn_conv=p["dn_conv"][l],
        alog128=_pad_cols(p["dn_a_log"][l].reshape(1, -1), DN_HEADS, LANES),
        dtb128=_pad_cols(p["dn_dt_bias"][l].reshape(1, -1), DN_HEADS, LANES),
        dn_onorm=p["dn_o_norm"][l].reshape(1, -1), hg_onorm=p["hg_o_norm"][l].reshape(1, -1),
        wbr_a=bf(p["w_br_a"][l]), wbr_b=bf(p["w_br_b"][l]), wbr_c=bf(p["w_br_c"][l]),
        wout=bf(p["w_out"][l]),
        ln1_g=p["ln1_g"][l].reshape(1, d), ln1_b=p["ln1_b"][l].reshape(1, d),
        r_cat=jnp.concatenate([r_hi, bf(router - r_hi.astype(F32))], axis=1),
        r_b=_pad_cols(r_b.reshape(1, -1), 0, LANES),
        e_gate=bf(p["exp_w_gate"][l]).reshape(N_GROUPS, EXPERTS_PER_GROUP, d, EXPERT_FF),
        e_up=bf(p["exp_w_up"][l]).reshape(N_GROUPS, EXPERTS_PER_GROUP, d, EXPERT_FF),
        e_down=bf(p["exp_w_down"][l]).reshape(N_GROUPS, EXPERTS_PER_GROUP, EXPERT_FF, d),
        ln2_g=p["ln2_g"][l].reshape(1, d), ln2_b=p["ln2_b"][l].reshape(1, d),
    )


def _forward(p):
    x = p["x"]
    batch, s, d = x.shape
    t = batch * s
    half = MLA_ROPE // 2
    inv_freq = ROPE_BASE ** (-jnp.arange(half, dtype=F32) / half)
    inv128 = _pad_cols(jnp.concatenate([inv_freq, inv_freq]).reshape(1, -1), ROPE_LO, LANES)
    tables = _rope_call(p["positions"].reshape(t, 1), inv128)
    h = _ln_call(x.reshape(t, d), p["ln_in_g"], p["ln_in_b"])
    for l in range(DEPTH):
        w = _layer_weights(l, p)
        q, k, v = _mla_proj_call(h, t, w, tables)
        oa = _flash_call(q, k, v, batch)
        x_dn = _proj_call(h, t, w["w_dn"], "proj_deltanet")
        ob = _dn_call(x_dn.reshape(batch, s, -1), w["dn_conv"], w["alog128"], w["dtb128"],
                      w["dn_onorm"])
        x_hg = _proj_call(h, t, w["w_hg"], "proj_hgrn2")
        oc = _hg_call(x_hg.reshape(batch, s, -1), p["hg_lower_bounds"], w["hg_onorm"], l)
        ext, info = _merge_call(h, oa, ob.reshape(t, -1), oc.reshape(t, -1), w)
        h = _moe_call(ext, info, w)
    return h[:t].reshape(batch, s, d)


def kernel(x, positions, ln_in_g, ln_in_b, hg_lower_bounds, w_in, mla_q_norm, mla_w_uq, mla_kv_norm, mla_w_ukv, dn_conv, dn_a_log, dn_dt_bias, dn_o_norm, hg_o_norm, w_br_a, w_br_b, w_br_c, w_out, ln1_g, ln1_b, router_group_w, router_group_b, router_expert_w, router_expert_b, exp_w_gate, exp_w_up, exp_w_down, ln2_g, ln2_b):
    return _forward(dict(
        x=x, positions=positions, ln_in_g=ln_in_g, ln_in_b=ln_in_b,
        hg_lower_bounds=hg_lower_bounds, w_in=w_in, mla_q_norm=mla_q_norm, mla_w_uq=mla_w_uq,
        mla_kv_norm=mla_kv_norm, mla_w_ukv=mla_w_ukv, dn_conv=dn_conv, dn_a_log=dn_a_log,
        dn_dt_bias=dn_dt_bias, dn_o_norm=dn_o_norm, hg_o_norm=hg_o_norm, w_br_a=w_br_a,
        w_br_b=w_br_b, w_br_c=w_br_c, w_out=w_out, ln1_g=ln1_g, ln1_b=ln1_b,
        router_group_w=router_group_w, router_group_b=router_group_b,
        router_expert_w=router_expert_w, router_expert_b=router_expert_b,
        exp_w_gate=exp_w_gate, exp_w_up=exp_w_up, exp_w_down=exp_w_down, ln2_g=ln2_g, ln2_b=ln2_b))
```
